```python
import math
import jax, jax.numpy as jnp
from jax import lax
import numpy as np

D_MODEL = 2048
BATCH = 8
SEQ = 2048
DEPTH = 1

MLSTM_WIDTH = D_MODEL // 2
MLSTM_HEADS = 4
MLSTM_V_DIM = MLSTM_WIDTH // MLSTM_HEADS
MLSTM_QK_DIM = MLSTM_V_DIM // 2
MLSTM_CHUNK = 64
CONV_WIDTH = 4
S5_WIDTH = D_MODEL - MLSTM_WIDTH
S5_GROUP = 16
S5_GROUPS = S5_WIDTH // S5_GROUP
S5_STATE = 64
DT_MIN = 1e-3
DT_MAX = 1e-1
D_MIX = MLSTM_WIDTH + S5_WIDTH
QK_COLS = 2 * MLSTM_HEADS * MLSTM_QK_DIM
V_COLS = MLSTM_HEADS * MLSTM_V_DIM
O_COLS = MLSTM_WIDTH
GATE_COLS = MLSTM_HEADS
IN_COLS = QK_COLS + V_COLS + O_COLS + 2 * GATE_COLS + S5_WIDTH
D_FF = ((8 * D_MODEL // 3 + 255) // 256) * 256
EPS = 1e-6

kernel_name = "macaron_mlstm_s5_hybrid"


def rms_norm(x, g):
    xf = x.astype(jnp.float32)
    y = xf * lax.rsqrt(jnp.mean(xf * xf, axis=-1, keepdims=True) + EPS)
    return (y * g.astype(jnp.float32)).astype(x.dtype)


def swiglu(x, w1, w3, w2):
    return (jax.nn.silu(x @ w1) * (x @ w3)) @ w2


def causal_depthwise_conv(x, w, b):
    k = w.shape[0]
    c = x.shape[-1]
    y = lax.conv_general_dilated(x, w[:, None, :], window_strides=(1,), padding=[(k - 1, 0)],
                                 dimension_numbers=('NWC', 'WIO', 'NWC'), feature_group_count=c)
    return y + b


def mlstm_chunkwise(q, k, v, i_pre, f_pre):
    bsz, seq, nh, dk = q.shape
    dv = v.shape[-1]
    nc = seq // MLSTM_CHUNK
    L = MLSTM_CHUNK

    def to_chunks(a):
        return a.reshape(bsz, nc, L, nh, a.shape[-1]).transpose(0, 3, 1, 2, 4)

    q = to_chunks(q) * (dk ** -0.5)
    k = to_chunks(k)
    v = to_chunks(v)
    li = i_pre.reshape(bsz, nc, L, nh).transpose(0, 3, 1, 2)
    lf = jax.nn.log_sigmoid(f_pre).reshape(bsz, nc, L, nh).transpose(0, 3, 1, 2)
    b = jnp.cumsum(lf, axis=-1)
    b_end = b[..., -1]

    w_end = b_end[..., None] - b + li
    m_loc = jnp.max(w_end, axis=-1)
    e_end = jnp.exp(w_end - m_loc[..., None])
    c_loc = jnp.einsum('bhcs,bhcsv,bhcsk->bhcvk', e_end, v, k)
    n_loc = jnp.einsum('bhcs,bhcsk->bhck', e_end, k)

    def step(carry, inp):
        c_st, n_st, m_st = carry
        c_l, n_l, m_l, bl = inp
        m_new = jnp.maximum(bl + m_st, m_l)
        a = jnp.exp(bl + m_st - m_new)
        g = jnp.exp(m_l - m_new)
        c_new = a[..., None, None] * c_st + g[..., None, None] * c_l
        n_new = a[..., None] * n_st + g[..., None] * n_l
        return (c_new, n_new, m_new), (c_st, n_st, m_st)

    init = (jnp.zeros((bsz, nh, dv, dk), jnp.float32),
            jnp.zeros((bsz, nh, dk), jnp.float32),
            jnp.zeros((bsz, nh), jnp.float32))
    xs = (jnp.moveaxis(c_loc, 2, 0), jnp.moveaxis(n_loc, 2, 0),
          jnp.moveaxis(m_loc, 2, 0), jnp.moveaxis(b_end, 2, 0))
    _, (c_prev, n_prev, m_prev) = lax.scan(step, init, xs)
    c_prev = jnp.moveaxis(c_prev, 0, 2)
    n_prev = jnp.moveaxis(n_prev, 0, 2)
    m_prev = jnp.moveaxis(m_prev, 0, 2)

    causal = jnp.tril(jnp.ones((L, L), dtype=bool))
    log_d = jnp.where(causal, b[..., :, None] - b[..., None, :] + li[..., None, :], -jnp.inf)
    inter_log = b + m_prev[..., None]
    m_t = jnp.maximum(inter_log, jnp.max(log_d, axis=-1))
    d_mat = jnp.exp(log_d - m_t[..., None])
    a_t = jnp.exp(inter_log - m_t)

    s = jnp.einsum('bhctk,bhcsk->bhcts', q, k) * d_mat
    num = jnp.einsum('bhcts,bhcsv->bhctv', s, v) + a_t[..., None] * jnp.einsum('bhctk,bhcvk->bhctv', q, c_prev)
    den = jnp.sum(s, axis=-1) + a_t * jnp.einsum('bhctk,bhck->bhct', q, n_prev)
    h = num / jnp.maximum(jnp.abs(den), jnp.exp(-m_t))[..., None]
    return h.transpose(0, 2, 3, 1, 4).reshape(bsz, seq, nh, dv)


def s5_branch(u, lam_re, lam_im, log_dt, b_re, b_im, c_re, c_im, d_skip, w_glu, b_glu):
    bsz, seq, _ = u.shape
    f32 = jnp.float32
    uf = u.astype(f32).reshape(bsz, seq, S5_GROUPS, S5_GROUP)
    lam = lax.complex(lam_re.astype(f32), lam_im.astype(f32))
    dt = jnp.exp(log_dt.astype(f32))[:, None]
    lam_bar = jnp.exp(lam * dt)
    b_c = lax.complex(b_re.astype(f32), b_im.astype(f32))
    b_bar = ((lam_bar - 1.0) / lam)[..., None] * b_c
    bu = jnp.einsum('bsgh,gph->bsgp', uf.astype(jnp.complex64), b_bar)
    a = jnp.broadcast_to(lam_bar[None, None], (1, seq, S5_GROUPS, S5_STATE))

    def combine(e1, e2):
        a1, x1 = e1
        a2, x2 = e2
        return a1 * a2, a2 * x1 + x2

    _, states = lax.associative_scan(combine, (a, bu), axis=1)
    c_c = lax.complex(c_re.astype(f32), c_im.astype(f32))
    y = jnp.real(jnp.einsum('bsgp,ghp->bsgh', states, c_c)) + d_skip.astype(f32) * uf
    y = jax.nn.gelu(y).reshape(bsz, seq, S5_WIDTH)
    y = y * jax.nn.sigmoid(y @ w_glu.astype(f32) + b_glu.astype(f32))
    return y.astype(u.dtype)


def hybrid_mixer(h, w_in, conv_w, conv_b, b_i, b_f, mlstm_norm, lam_re, lam_im, log_dt,
                 b_re, b_im, c_re, c_im, d_skip, w_glu, b_glu, w_out):
    bsz, seq, _ = h.shape
    proj = h @ w_in
    idx = [QK_COLS, QK_COLS + V_COLS, QK_COLS + V_COLS + O_COLS,
           QK_COLS + V_COLS + O_COLS + GATE_COLS, QK_COLS + V_COLS + O_COLS + 2 * GATE_COLS]
    qk, v, o, ig, fg, u = jnp.split(proj, idx, axis=-1)

    qk = jax.nn.silu(causal_depthwise_conv(qk, conv_w, conv_b))
    q, k = jnp.split(qk, 2, axis=-1)
    f32 = jnp.float32
    q = q.reshape(bsz, seq, MLSTM_HEADS, MLSTM_QK_DIM).astype(f32)
    k = k.reshape(bsz, seq, MLSTM_HEADS, MLSTM_QK_DIM).astype(f32)
    vh = v.reshape(bsz, seq, MLSTM_HEADS, MLSTM_V_DIM).astype(f32)
    i_pre = ig.astype(f32) + b_i.astype(f32)
    f_pre = fg.astype(f32) + b_f.astype(f32)
    hm = mlstm_chunkwise(q, k, vh, i_pre, f_pre)
    hm = rms_norm(hm, mlstm_norm.reshape(MLSTM_HEADS, MLSTM_V_DIM))
    hm = (hm.reshape(bsz, seq, MLSTM_WIDTH) * jax.nn.sigmoid(o.astype(f32))).astype(h.dtype)

    hs = s5_branch(u, lam_re, lam_im, log_dt, b_re, b_im, c_re, c_im, d_skip, w_glu, b_glu)

    return jnp.concatenate([hm, hs], axis=-1) @ w_out


def setup_inputs(seed: int = 0) -> dict:
    key = jax.random.key(seed)
    ks = jax.random.split(key, 32)
    f32 = jnp.float32
    L = DEPTH

    def nrm(k, shape, scale):
        return jax.random.normal(k, shape, f32) * scale

    def gain(k, shape):
        return 1.0 + 0.02 * jax.random.normal(k, shape, f32)

    lam_im = math.pi * jnp.arange(S5_STATE, dtype=f32)[None, None, :] + nrm(ks[13], (L, S5_GROUPS, S5_STATE), 0.01)
    b_f = jnp.linspace(3.0, 6.0, MLSTM_HEADS, dtype=f32)[None, :] + nrm(ks[10], (L, MLSTM_HEADS), 0.1)
    return {
        "x": nrm(ks[0], (BATCH, SEQ, D_MODEL), 1.0),
        "ffn1_norm": gain(ks[1], (L, D_MODEL)),
        "ffn1_w1": nrm(ks[2], (L, D_MODEL, D_FF), D_MODEL ** -0.5),
        "ffn1_w3": nrm(ks[3], (L, D_MODEL, D_FF), D_MODEL ** -0.5),
        "ffn1_w2": nrm(ks[4], (L, D_FF, D_MODEL), D_FF ** -0.5),
        "mix_norm": gain(ks[5], (L, D_MODEL)),
        "w_in": nrm(ks[6], (L, D_MODEL, IN_COLS), D_MODEL ** -0.5),
        "conv_w": nrm(ks[7], (L, CONV_WIDTH, QK_COLS), CONV_WIDTH ** -0.5),
        "conv_b": nrm(ks[8], (L, QK_COLS), 0.01),
        "b_i": nrm(ks[9], (L, MLSTM_HEADS), 0.1),
        "b_f": b_f,
        "mlstm_norm": gain(ks[11], (L, MLSTM_WIDTH)),
        "lam_re": -0.5 + nrm(ks[12], (L, S5_GROUPS, S5_STATE), 0.01),
        "lam_im": lam_im,
        "log_dt": jax.random.uniform(ks[14], (L, S5_GROUPS), f32, math.log(DT_MIN), math.log(DT_MAX)),
        "b_re": nrm(ks[15], (L, S5_GROUPS, S5_STATE, S5_GROUP), (2 * S5_GROUP) ** -0.5),
        "b_im": nrm(ks[16], (L, S5_GROUPS, S5_STATE, S5_GROUP), (2 * S5_GROUP) ** -0.5),
        "c_re": nrm(ks[17], (L, S5_GROUPS, S5_GROUP, S5_STATE), (2 * S5_STATE) ** -0.5),
        "c_im": nrm(ks[18], (L, S5_GROUPS, S5_GROUP, S5_STATE), (2 * S5_STATE) ** -0.5),
        "d_skip": nrm(ks[19], (L, S5_GROUPS, S5_GROUP), 1.0),
        "w_glu": nrm(ks[20], (L, S5_WIDTH, S5_WIDTH), S5_WIDTH ** -0.5),
        "b_glu": nrm(ks[21], (L, S5_WIDTH), 0.01),
        "w_out": nrm(ks[22], (L, D_MIX, D_MODEL), D_MIX ** -0.5),
        "ffn2_norm": gain(ks[23], (L, D_MODEL)),
        "ffn2_w1": nrm(ks[24], (L, D_MODEL, D_FF), D_MODEL ** -0.5),
        "ffn2_w3": nrm(ks[25], (L, D_MODEL, D_FF), D_MODEL ** -0.5),
        "ffn2_w2": nrm(ks[26], (L, D_FF, D_MODEL), D_FF ** -0.5),
        "final_norm": gain(ks[27], (D_MODEL,)),
    }


def reference(x, ffn1_norm, ffn1_w1, ffn1_w3, ffn1_w2, mix_norm, w_in, conv_w, conv_b, b_i, b_f,
              mlstm_norm, lam_re, lam_im, log_dt, b_re, b_im, c_re, c_im, d_skip, w_glu, b_glu,
              w_out, ffn2_norm, ffn2_w1, ffn2_w3, ffn2_w2, final_norm):
    for l in range(DEPTH):
        x = x + 0.5 * swiglu(rms_norm(x, ffn1_norm[l]), ffn1_w1[l], ffn1_w3[l], ffn1_w2[l])
        h = rms_norm(x, mix_norm[l])
        x = x + hybrid_mixer(h, w_in[l], conv_w[l], conv_b[l], b_i[l], b_f[l], mlstm_norm[l],
                             lam_re[l], lam_im[l], log_dt[l], b_re[l], b_im[l], c_re[l], c_im[l],
                             d_skip[l], w_glu[l], b_glu[l], w_out[l])
        x = x + 0.5 * swiglu(rms_norm(x, ffn2_norm[l]), ffn2_w1[l], ffn2_w3[l], ffn2_w2[l])
    return rms_norm(x, final_norm)
```

```python
import functools
import math

import jax
import jax.numpy as jnp
from jax import lax
from jax.experimental import pallas as pl
from jax.experimental.pallas import tpu as pltpu

F32 = jnp.float32
BF16 = jnp.bfloat16

EPS = 1e-6
MLSTM_HEADS = 4
S5_GROUP = 16
S5_STATE = 64
S5_GROUPS_PER_BLOCK = 8
LANES = 128
SUBLANES = 8
MLSTM_CHUNK = 256
VMEM_LIMIT_BYTES = 56 * 1024 * 1024


def _rms(x, g):
    return x * lax.rsqrt(jnp.mean(x * x, axis=-1, keepdims=True) + EPS) * g


def _params(sem):
    return pltpu.CompilerParams(dimension_semantics=sem, vmem_limit_bytes=VMEM_LIMIT_BYTES)


def _ffn_body(x_ref, g_ref, w1_ref, w3_ref, w2_ref, gf_ref, o_ref, xn_ref, *, n_ff, final_norm):
    j = pl.program_id(1)

    @pl.when(j == 0)
    def _():
        x = x_ref[...]
        xn_ref[...] = _rms(x, g_ref[...]).astype(BF16)
        o_ref[...] = x

    xn = xn_ref[...]
    a = jnp.dot(xn, w1_ref[...], preferred_element_type=F32)
    b = jnp.dot(xn, w3_ref[...], preferred_element_type=F32)
    h = (a * jax.nn.sigmoid(a)) * (b * 0.5)
    o_ref[...] += jnp.dot(h.astype(BF16), w2_ref[...], preferred_element_type=F32)

    if final_norm:
        @pl.when(j == n_ff - 1)
        def _():
            o_ref[...] = _rms(o_ref[...], gf_ref[...])


def _ffn(x, g, w1, w3, w2, gf, *, final_norm, tm=1024, tf=512):
    t, d = x.shape
    dff = w1.shape[1]
    n_ff = dff // tf
    body = functools.partial(_ffn_body, n_ff=n_ff, final_norm=final_norm)
    return pl.pallas_call(
        body,
        grid=(t // tm, n_ff),
        in_specs=[
            pl.BlockSpec((tm, d), lambda i, j: (i, 0)),
            pl.BlockSpec((1, d), lambda i, j: (0, 0)),
            pl.BlockSpec((d, tf), lambda i, j: (0, j)),
            pl.BlockSpec((d, tf), lambda i, j: (0, j)),
            pl.BlockSpec((tf, d), lambda i, j: (j, 0)),
            pl.BlockSpec((1, d), lambda i, j: (0, 0)),
        ],
        out_specs=pl.BlockSpec((tm, d), lambda i, j: (i, 0)),
        out_shape=jax.ShapeDtypeStruct((t, d), F32),
        scratch_shapes=[pltpu.VMEM((tm, d), BF16)],
        compiler_params=_params(("parallel", "arbitrary")),
        name="ffn_final" if final_norm else "ffn",
    )(x, g, w1, w3, w2, gf)


def _inproj_body(x_ref, g_ref, w_ref, wg_ref, bg_ref, qk_ref, v_ref, o_ref, u_ref, gates_ref, hn_ref):
    j = pl.program_id(1)

    @pl.when(j == 0)
    def _():
        hn = _rms(x_ref[...], g_ref[...]).astype(BF16)
        hn_ref[...] = hn
        gt = lax.dot_general(wg_ref[...], hn, (((1,), (1,)), ((), ())), preferred_element_type=F32)
        gt = gt + bg_ref[...]
        for h in range(MLSTM_HEADS):
            gates_ref[h] = gt[2 * h:2 * h + 2, :]

    res = jnp.dot(hn_ref[...], w_ref[...], preferred_element_type=F32)

    @pl.when(j == 0)
    def _():
        qk_ref[...] = res.astype(BF16)

    @pl.when(j == 1)
    def _():
        v_ref[...] = res.astype(BF16)

    @pl.when(j == 2)
    def _():
        o_ref[...] = res.astype(BF16)

    @pl.when(j == 3)
    def _():
        u_ref[...] = res


def _inproj(x, g, w_main, wg_t, bg, *, tm=512):
    t, d = x.shape
    w = w_main.shape[1] // 4
    blk = lambda: pl.BlockSpec((tm, w), lambda i, j: (i, 0))
    return pl.pallas_call(
        _inproj_body,
        grid=(t // tm, 4),
        in_specs=[
            pl.BlockSpec((tm, d), lambda i, j: (i, 0)),
            pl.BlockSpec((1, d), lambda i, j: (0, 0)),
            pl.BlockSpec((d, w), lambda i, j: (0, j)),
            pl.BlockSpec((2 * MLSTM_HEADS, d), lambda i, j: (0, 0)),
            pl.BlockSpec((2 * MLSTM_HEADS, 1), lambda i, j: (0, 0)),
        ],
        out_specs=[blk(), blk(), blk(), blk(),
                   pl.BlockSpec((MLSTM_HEADS, 2, tm), lambda i, j: (0, 0, i))],
        out_shape=[
            jax.ShapeDtypeStruct((t, w), BF16),
            jax.ShapeDtypeStruct((t, w), BF16),
            jax.ShapeDtypeStruct((t, w), BF16),
            jax.ShapeDtypeStruct((t, w), F32),
            jax.ShapeDtypeStruct((MLSTM_HEADS, 2, t), F32),
        ],
        scratch_shapes=[pltpu.VMEM((tm, d), BF16)],
        compiler_params=_params(("parallel", "arbitrary")),
        name="inproj",
    )(x, g, w_main, wg_t, bg)


def _split3(x):
    hi = x.astype(BF16)
    r1 = x - hi.astype(F32)
    mid = r1.astype(BF16)
    lo = (r1 - mid.astype(F32)).astype(BF16)
    return hi, mid, lo


def _log_sigmoid(x):
    return jnp.minimum(x, 0.0) - jnp.log(1.0 + jnp.exp(-jnp.abs(x)))


def _mlstm_body(q_ref, k_ref, v_ref, og_ref, g_ref, cwq_ref, cwk_ref, cbq_ref, cbk_ref, nw_ref,
                out_ref, xp_ref, qs_ref, ks_ref, tri_ref, ct_ref, n_ref, m_ref, *, seq, dk, dv):
    L = MLSTM_CHUNK
    nc = seq // L

    def conv_silu(x_ref, w_ref, b_ref, dst_ref, scale):
        xp_ref[0:SUBLANES, :] = jnp.zeros((SUBLANES, dk), F32)
        xp_ref[SUBLANES:, :] = x_ref[...].astype(F32)
        w = w_ref[...]
        acc = b_ref[...] + xp_ref[pl.ds(SUBLANES, seq), :] * w[3:4, :]
        for d in (1, 2, 3):
            acc = acc + xp_ref[pl.ds(SUBLANES - d, seq), :] * w[3 - d:4 - d, :]
        y = acc * jax.nn.sigmoid(acc)
        if scale != 1.0:
            y = y * scale
        dst_ref[...] = y.astype(BF16)

    conv_silu(q_ref, cwq_ref, cbq_ref, qs_ref, dk ** -0.5)
    conv_silu(k_ref, cwk_ref, cbk_ref, ks_ref, 1.0)

    rows = lax.broadcasted_iota(jnp.int32, (L, L), 0)
    cols = lax.broadcasted_iota(jnp.int32, (L, L), 1)
    causal = cols <= rows
    tri_ref[...] = jnp.where(rows <= cols, 1.0, 0.0).astype(BF16)
    er = lax.broadcasted_iota(jnp.int32, (SUBLANES, 2 * LANES), 0)
    ec = lax.broadcasted_iota(jnp.int32, (SUBLANES, 2 * LANES), 1)
    sel = jnp.where(((er < 3) & (ec < LANES)) | ((er >= 3) & (er < 6) & (ec >= LANES)), 1.0, 0.0).astype(BF16)
    ones_rows = jnp.ones((SUBLANES, L), BF16)

    ct_ref[...] = jnp.zeros_like(ct_ref)
    n_ref[...] = jnp.zeros_like(n_ref)
    m_ref[...] = jnp.zeros_like(m_ref)

    def chunk(c, carry):
        r0 = pl.multiple_of(c * L, L)
        q = qs_ref[pl.ds(r0, L), :]
        k = ks_ref[pl.ds(r0, L), :]
        v = v_ref[pl.ds(r0, L), :]
        g = g_ref[:, pl.ds(r0, L)]
        li = g[0:1, :]
        lf = _log_sigmoid(g[1:2, :])
        zrow = jnp.zeros((1, L), BF16)
        p = _split3(lf)
        parts = jnp.concatenate([p[0], p[1], p[2], zrow, zrow, zrow, zrow, zrow], axis=0)
        cs = jnp.dot(parts, tri_ref[...], preferred_element_type=F32)
        b_row = cs[0:1, :] + cs[1:2, :] + cs[2:3, :]
        d_row = li - b_row
        b_end = b_row[:, L - 1:L]
        pb = _split3(b_row)
        pd = _split3(d_row)
        at = jnp.concatenate([pb[0], pb[1], pb[2], pd[0], pd[1], pd[2], zrow, zrow], axis=0)
        colf = lax.dot_general(at, sel, (((0,), (0,)), ((), ())), preferred_element_type=F32)
        b_col = colf[:, 0:1]
        d_col = colf[:, LANES:LANES + 1]

        m_prev = m_ref[0:1, 0:1]
        log_d = b_col + d_row
        intra_max = jnp.max(jnp.where(causal, log_d, -jnp.inf), axis=-1, keepdims=True)
        inter_log = b_col + m_prev
        m_t = jnp.maximum(inter_log, intra_max)
        d_mat = jnp.where(causal, jnp.exp(log_d - m_t), 0.0)
        a_t = jnp.exp(inter_log - m_t)

        qk = lax.dot_general(q, k, (((1,), (1,)), ((), ())), preferred_element_type=F32)
        s = qk * d_mat
        ct = ct_ref[...]
        num = jnp.dot(s.astype(BF16), v, preferred_element_type=F32)
        num = num + a_t * jnp.dot(q, ct.astype(BF16), preferred_element_type=F32)
        qn = jnp.sum(q.astype(F32) * n_ref[0:1, :], axis=-1, keepdims=True)
        den = jnp.sum(s, axis=-1, keepdims=True) + a_t * qn
        h = num / jnp.maximum(jnp.abs(den), jnp.exp(-m_t))
        hn = h * lax.rsqrt(jnp.mean(h * h, axis=-1, keepdims=True) + EPS) * nw_ref[...]
        og = og_ref[pl.ds(r0, L), :].astype(F32)
        out_ref[pl.ds(r0, L), :] = (hn * jax.nn.sigmoid(og)).astype(BF16)

        w_end = b_end + d_row
        m_loc = jnp.max(w_end, axis=-1, keepdims=True)
        e_col = jnp.exp(b_end + d_col - m_loc)
        ke = (k.astype(F32) * e_col).astype(BF16)
        c_loc = lax.dot_general(ke, v, (((0,), (0,)), ((), ())), preferred_element_type=F32)
        n_loc = jnp.dot(ones_rows, ke, preferred_element_type=F32)[0:1, :]
        m_new = jnp.maximum(b_end + m_prev, m_loc)
        a = jnp.exp(b_end + m_prev - m_new)
        gsc = jnp.exp(m_loc - m_new)
        ct_ref[...] = a * ct + gsc * c_loc
        n_ref[...] = jnp.broadcast_to(a * n_ref[0:1, :] + gsc * n_loc, n_ref.shape)
        m_ref[...] = jnp.broadcast_to(m_new, m_ref.shape)
        return carry

    lax.fori_loop(0, nc, chunk, 0)


def _mlstm(qk, v, og, gates, conv_w, conv_b, norm_w, *, batch, seq):
    t, w = qk.shape
    nh = MLSTM_HEADS
    dk = w // (2 * nh)
    dv = v.shape[1] // nh
    L = MLSTM_CHUNK
    body = functools.partial(_mlstm_body, seq=seq, dk=dk, dv=dv)
    return pl.pallas_call(
        body,
        grid=(batch, nh),
        in_specs=[
            pl.BlockSpec((seq, dk), lambda b, h: (b, h)),
            pl.BlockSpec((seq, dk), lambda b, h: (b, nh + h)),
            pl.BlockSpec((seq, dv), lambda b, h: (b, h)),
            pl.BlockSpec((seq, dv), lambda b, h: (b, h)),
            pl.BlockSpec((None, 2, seq), lambda b, h: (h, 0, b)),
            pl.BlockSpec((conv_w.shape[0], dk), lambda b, h: (0, h)),
            pl.BlockSpec((conv_w.shape[0], dk), lambda b, h: (0, nh + h)),
            pl.BlockSpec((1, dk), lambda b, h: (0, h)),
            pl.BlockSpec((1, dk), lambda b, h: (0, nh + h)),
            pl.BlockSpec((1, dv), lambda b, h: (0, h)),
        ],
        out_specs=pl.BlockSpec((seq, dv), lambda b, h: (b, h)),
        out_shape=jax.ShapeDtypeStruct((t, nh * dv), BF16),
        scratch_shapes=[
            pltpu.VMEM((seq + SUBLANES, dk), F32),
            pltpu.VMEM((seq, dk), BF16),
            pltpu.VMEM((seq, dk), BF16),
            pltpu.VMEM((L, L), BF16),
            pltpu.VMEM((dk, dv), F32),
            pltpu.VMEM((SUBLANES, dk), F32),
            pltpu.VMEM((SUBLANES, LANES), F32),
        ],
        compiler_params=_params(("parallel", "parallel")),
        name="mlstm",
    )(qk, qk, v, og, gates, conv_w, conv_w, conv_b, conv_b, norm_w)


def _s5prep_body(lr_ref, li_ref, ldt_ref, br_ref, bi_ref, lbr_ref, lbi_ref, bbr_ref, bbi_ref):
    lr = lr_ref[...]
    li = li_ref[...]
    dt = jnp.exp(ldt_ref[...])
    mag = jnp.exp(lr * dt)
    ang = li * dt
    lbr = mag * jnp.cos(ang)
    lbi = mag * jnp.sin(ang)
    nr = lbr - 1.0
    den = lr * lr + li * li
    cr = (nr * lr + lbi * li) / den
    ci = (lbi * lr - nr * li) / den
    br = br_ref[...]
    bi = bi_ref[...]
    lbr_ref[...] = lbr
    lbi_ref[...] = lbi
    bbr_ref[...] = cr * br - ci * bi
    bbi_ref[...] = cr * bi + ci * br


def _s5prep(lr, li, ldt, br, bi):
    shp = jax.ShapeDtypeStruct(lr.shape, F32)
    return pl.pallas_call(_s5prep_body, out_shape=[shp, shp, shp, shp], name="s5prep")(lr, li, ldt, br, bi)


def _s5_body(u_ref, bm_ref, cm_ref, lr_ref, li_ref, dsk_ref, wglu_ref, bglu_ref, o_ref,
             usb_ref, bu_ref, y_ref, st_ref, *, ts, ngb):
    half = S5_GROUPS_PER_BLOCK * S5_STATE
    cb = S5_GROUPS_PER_BLOCK * S5_GROUP

    @pl.when(pl.program_id(0) == 0)
    def _():
        st_ref[...] = jnp.zeros_like(st_ref)

    def gather(s, c):
        usb_ref[pl.ds(pl.multiple_of(s * SUBLANES, SUBLANES), SUBLANES), :] = u_ref[:, s, :]
        return c

    lax.fori_loop(0, ts, gather, 0, unroll=8)

    for gb in range(ngb):
        ug = usb_ref[:, gb * cb:(gb + 1) * cb]
        bu_ref[...] = jnp.dot(ug.astype(BF16), bm_ref[gb], preferred_element_type=F32)
        lrb = jnp.broadcast_to(lr_ref[gb], (SUBLANES, half))
        lib = jnp.broadcast_to(li_ref[gb], (SUBLANES, half))

        def step(s, x):
            xr, xi = x
            r0 = pl.multiple_of(s * SUBLANES, SUBLANES)
            nr = lrb * xr - lib * xi + bu_ref[pl.ds(r0, SUBLANES), 0:half]
            ni = lrb * xi + lib * xr + bu_ref[pl.ds(r0, SUBLANES), half:2 * half]
            bu_ref[pl.ds(r0, SUBLANES), 0:half] = nr
            bu_ref[pl.ds(r0, SUBLANES), half:2 * half] = ni
            return nr, ni

        x0 = (st_ref[gb, :, 0:half], st_ref[gb, :, half:2 * half])
        xr, xi = lax.fori_loop(0, ts, step, x0, unroll=8)
        st_ref[gb, :, 0:half] = xr
        st_ref[gb, :, half:2 * half] = xi

        yg = jnp.dot(bu_ref[...].astype(BF16), cm_ref[gb], preferred_element_type=F32)
        yg = yg + dsk_ref[:, gb * cb:(gb + 1) * cb] * ug
        y_ref[:, gb * cb:(gb + 1) * cb] = jax.nn.gelu(yg)

    y = y_ref[...]
    z = jnp.dot(y.astype(BF16), wglu_ref[...], preferred_element_type=F32) + bglu_ref[...]
    usb_ref[...] = y * jax.nn.sigmoid(z)

    def scatter(s, c):
        o_ref[:, s, :] = usb_ref[pl.ds(pl.multiple_of(s * SUBLANES, SUBLANES), SUBLANES), :]
        return c

    lax.fori_loop(0, ts, scatter, 0, unroll=8)


def _s5(u, bm, cm, lbr, lbi, dsk, wglu, bglu, *, ts=64):
    batch, seq, width = u.shape
    ngb = bm.shape[0]
    half = S5_GROUPS_PER_BLOCK * S5_STATE
    body = functools.partial(_s5_body, ts=ts, ngb=ngb)
    full = lambda a: pl.BlockSpec(a.shape, lambda i, n=a.ndim: (0,) * n)
    return pl.pallas_call(
        body,
        grid=(seq // ts,),
        in_specs=[pl.BlockSpec((batch, ts, width), lambda i: (0, i, 0)),
                  full(bm), full(cm), full(lbr), full(lbi), full(dsk), full(wglu), full(bglu)],
        out_specs=pl.BlockSpec((batch, ts, width), lambda i: (0, i, 0)),
        out_shape=jax.ShapeDtypeStruct((batch, seq, width), F32),
        scratch_shapes=[
            pltpu.VMEM((ts * batch, width), F32),
            pltpu.VMEM((ts * batch, 2 * half), F32),
            pltpu.VMEM((ts * batch, width), F32),
            pltpu.VMEM((ngb, batch, 2 * half), F32),
        ],
        compiler_params=_params(("arbitrary",)),
        name="s5",
    )(u, bm, cm, lbr, lbi, dsk, wglu, bglu)


def _s5_weights(lam_re, lam_im, log_dt, b_re, b_im, c_re, c_im):
    g, p = lam_re.shape
    hch = b_re.shape[-1]
    gpb = S5_GROUPS_PER_BLOCK
    ngb = g // gpb
    rep = lambda a: jnp.repeat(a, hch, axis=0)
    to_rows = lambda a: jnp.transpose(a, (0, 2, 1)).reshape(g * hch, p)
    ldt = jnp.broadcast_to(log_dt[:, None], (g, p))
    lbr, lbi, bbr, bbi = _s5prep(rep(lam_re), rep(lam_im), rep(ldt), to_rows(b_re), to_rows(b_im))
    lbr = lbr[::hch].reshape(ngb, 1, gpb * p)
    lbi = lbi[::hch].reshape(ngb, 1, gpb * p)
    eye = jnp.eye(gpb, dtype=F32)

    def in_blockdiag(a):
        a = a.reshape(ngb, gpb, hch, p)
        return (a[:, :, :, None, :] * eye[None, :, None, :, None]).reshape(ngb, gpb * hch, gpb * p)

    def out_blockdiag(a):
        a = jnp.transpose(a, (0, 2, 1)).reshape(ngb, gpb, p, hch)
        return (a[:, :, :, None, :] * eye[None, :, None, :, None]).reshape(ngb, gpb * p, gpb * hch)

    bm = jnp.concatenate([in_blockdiag(bbr), in_blockdiag(bbi)], axis=2).astype(BF16)
    cm = jnp.concatenate([out_blockdiag(c_re), out_blockdiag(-c_im)], axis=1).astype(BF16)
    return bm, cm, lbr, lbi


def _outproj_body(x_ref, hm_ref, hs_ref, wa_ref, wb_ref, o_ref):
    acc = jnp.dot(hm_ref[...], wa_ref[...], preferred_element_type=F32)
    acc = acc + jnp.dot(hs_ref[...].astype(BF16), wb_ref[...], preferred_element_type=F32)
    o_ref[...] = x_ref[...] + acc


def _outproj(x, hm, hs, wa, wb, *, tm=512):
    t, d = x.shape
    w = hm.shape[1]
    return pl.pallas_call(
        _outproj_body,
        grid=(t // tm,),
        in_specs=[
            pl.BlockSpec((tm, d), lambda i: (i, 0)),
            pl.BlockSpec((tm, w), lambda i: (i, 0)),
            pl.BlockSpec((tm, w), lambda i: (i, 0)),
            pl.BlockSpec(wa.shape, lambda i: (0, 0)),
            pl.BlockSpec(wb.shape, lambda i: (0, 0)),
        ],
        out_specs=pl.BlockSpec((tm, d), lambda i: (i, 0)),
        out_shape=jax.ShapeDtypeStruct((t, d), F32),
        compiler_params=_params(("parallel",)),
        name="outproj",
    )(x, hm, hs, wa, wb)


def kernel(x, ffn1_norm, ffn1_w1, ffn1_w3, ffn1_w2, mix_norm, w_in, conv_w, conv_b, b_i, b_f, mlstm_norm, lam_re, lam_im, log_dt, b_re, b_im, c_re, c_im, d_skip, w_glu, b_glu, w_out, ffn2_norm, ffn2_w1, ffn2_w3, ffn2_w2, final_norm):
    batch, seq, d = x.shape
    depth = ffn1_norm.shape[0]
    nh = MLSTM_HEADS
    qk_cols = conv_w.shape[-1]
    mw = mlstm_norm.shape[-1]
    gate0 = qk_cols + 2 * mw
    sw = w_glu.shape[-1]
    row = lambda a: a.reshape(1, -1).astype(F32)
    xt = x.reshape(batch * seq, d)
    gfin = row(final_norm)
    for l in range(depth):
        xt = _ffn(xt, row(ffn1_norm[l]), ffn1_w1[l].astype(BF16), ffn1_w3[l].astype(BF16),
                  ffn1_w2[l].astype(BF16), gfin, final_norm=False)

        wl = w_in[l]
        w_main = jnp.concatenate([wl[:, :gate0], wl[:, gate0 + 2 * nh:]], axis=1).astype(BF16)
        wg = wl[:, gate0:gate0 + 2 * nh]
        wg_t = jnp.stack([wg[:, :nh], wg[:, nh:]], axis=-1).reshape(d, 2 * nh).T.astype(BF16)
        bg = jnp.stack([b_i[l], b_f[l]], axis=-1).reshape(2 * nh, 1).astype(F32)
        qk, v, og, u, gates = _inproj(xt, row(mix_norm[l]), w_main, wg_t, bg)

        hm = _mlstm(qk, v, og, gates, conv_w[l].astype(F32), row(conv_b[l]), row(mlstm_norm[l]),
                    batch=batch, seq=seq)

        bm, cm, lbr, lbi = _s5_weights(lam_re[l], lam_im[l], log_dt[l], b_re[l], b_im[l], c_re[l], c_im[l])
        hs = _s5(u.reshape(batch, seq, sw), bm, cm, lbr, lbi, row(d_skip[l]), w_glu[l].astype(BF16), row(b_glu[l]))

        wo = w_out[l].astype(BF16)
        xt = _outproj(xt, hm, hs.reshape(batch * seq, sw), wo[:mw], wo[mw:])

        xt = _ffn(xt, row(ffn2_norm[l]), ffn2_w1[l].astype(BF16), ffn2_w3[l].astype(BF16),
                  ffn2_w2[l].astype(BF16), gfin, final_norm=(l == depth - 1))
    return xt.reshape(batch, seq, d)
```

```python
import functools
import math

import jax
import jax.numpy as jnp
from jax import lax
from jax.experimental import pallas as pl
from jax.experimental.pallas import tpu as pltpu

F32 = jnp.float32
BF16 = jnp.bfloat16

EPS = 1e-6
MLSTM_HEADS = 4
S5_GROUP = 16
S5_STATE = 64
S5_GROUPS_PER_BLOCK = 8
LANES = 128
SUBLANES = 8
MLSTM_CHUNK = 256
VMEM_LIMIT_BYTES = 56 * 1024 * 1024


def _rms(x, g):
    return x * lax.rsqrt(jnp.mean(x * x, axis=-1, keepdims=True) + EPS) * g


def _params(sem):
    return pltpu.CompilerParams(dimension_semantics=sem, vmem_limit_bytes=VMEM_LIMIT_BYTES)


def _ffn_body(x_ref, g_ref, w1_ref, w3_ref, w2_ref, gf_ref, o_ref, xn_ref, *, n_ff, final_norm):
    j = pl.program_id(1)

    @pl.when(j == 0)
    def _():
        x = x_ref[...]
        xn_ref[...] = _rms(x, g_ref[...]).astype(BF16)
        o_ref[...] = x

    xn = xn_ref[...]
    a = jnp.dot(xn, w1_ref[...], preferred_element_type=F32)
    b = jnp.dot(xn, w3_ref[...], preferred_element_type=F32)
    h = (a * jax.nn.sigmoid(a)) * (b * 0.5)
    o_ref[...] += jnp.dot(h.astype(BF16), w2_ref[...], preferred_element_type=F32)

    if final_norm:
        @pl.when(j == n_ff - 1)
        def _():
            o_ref[...] = _rms(o_ref[...], gf_ref[...])


def _ffn(x, g, w1, w3, w2, gf, *, final_norm, tm=1024, tf=512):
    t, d = x.shape
    dff = w1.shape[1]
    n_ff = dff // tf
    body = functools.partial(_ffn_body, n_ff=n_ff, final_norm=final_norm)
    return pl.pallas_call(
        body,
        grid=(t // tm, n_ff),
        in_specs=[
            pl.BlockSpec((tm, d), lambda i, j: (i, 0)),
            pl.BlockSpec((1, d), lambda i, j: (0, 0)),
            pl.BlockSpec((d, tf), lambda i, j: (0, j)),
            pl.BlockSpec((d, tf), lambda i, j: (0, j)),
            pl.BlockSpec((tf, d), lambda i, j: (j, 0)),
            pl.BlockSpec((1, d), lambda i, j: (0, 0)),
        ],
        out_specs=pl.BlockSpec((tm, d), lambda i, j: (i, 0)),
        out_shape=jax.ShapeDtypeStruct((t, d), F32),
        scratch_shapes=[pltpu.VMEM((tm, d), BF16)],
        compiler_params=_params(("parallel", "arbitrary")),
        name="ffn_final" if final_norm else "ffn",
    )(x, g, w1, w3, w2, gf)


def _inproj_body(x_ref, g_ref, w_ref, wg_ref, bg_ref, qk_ref, v_ref, o_ref, u_ref, gates_ref, hn_ref):
    j = pl.program_id(1)

    @pl.when(j == 0)
    def _():
        hn = _rms(x_ref[...], g_ref[...]).astype(BF16)
        hn_ref[...] = hn
        gt = lax.dot_general(wg_ref[...], hn, (((1,), (1,)), ((), ())), preferred_element_type=F32)
        gt = gt + bg_ref[...]
        for h in range(MLSTM_HEADS):
            gates_ref[h] = gt[2 * h:2 * h + 2, :]

    res = jnp.dot(hn_ref[...], w_ref[...], preferred_element_type=F32)

    @pl.when(j == 0)
    def _():
        qk_ref[...] = res.astype(BF16)

    @pl.when(j == 1)
    def _():
        v_ref[...] = res.astype(BF16)

    @pl.when(j == 2)
    def _():
        o_ref[...] = res.astype(BF16)

    @pl.when(j == 3)
    def _():
        u_ref[...] = res


def _inproj(x, g, w_main, wg_t, bg, *, tm=1024):
    t, d = x.shape
    w = w_main.shape[1] // 4
    blk = lambda: pl.BlockSpec((tm, w), lambda i, j: (i, 0))
    return pl.pallas_call(
        _inproj_body,
        grid=(t // tm, 4),
        in_specs=[
            pl.BlockSpec((tm, d), lambda i, j: (i, 0)),
            pl.BlockSpec((1, d), lambda i, j: (0, 0)),
            pl.BlockSpec((d, w), lambda i, j: (0, j)),
            pl.BlockSpec((2 * MLSTM_HEADS, d), lambda i, j: (0, 0)),
            pl.BlockSpec((2 * MLSTM_HEADS, 1), lambda i, j: (0, 0)),
        ],
        out_specs=[blk(), blk(), blk(), blk(),
                   pl.BlockSpec((MLSTM_HEADS, 2, tm), lambda i, j: (0, 0, i))],
        out_shape=[
            jax.ShapeDtypeStruct((t, w), BF16),
            jax.ShapeDtypeStruct((t, w), BF16),
            jax.ShapeDtypeStruct((t, w), BF16),
            jax.ShapeDtypeStruct((t, w), F32),
            jax.ShapeDtypeStruct((MLSTM_HEADS, 2, t), F32),
        ],
        scratch_shapes=[pltpu.VMEM((tm, d), BF16)],
        compiler_params=_params(("parallel", "arbitrary")),
        name="inproj",
    )(x, g, w_main, wg_t, bg)


def _split3(x):
    hi = x.astype(BF16).astype(F32)
    r1 = x - hi
    mid = r1.astype(BF16).astype(F32)
    lo = (r1 - mid).astype(BF16).astype(F32)
    return [hi, mid, lo]


def _log_sigmoid(x):
    return jnp.minimum(x, 0.0) - jnp.log(1.0 + jnp.exp(-jnp.abs(x)))


def _mlstm_body(q_ref, k_ref, v_ref, og_ref, g_ref, cwq_ref, cwk_ref, cbq_ref, cbk_ref, nw_ref,
                out_ref, xp_ref, qs_ref, ks_ref, *, seq, dk, dv):
    L = MLSTM_CHUNK
    nc = seq // L

    def conv_silu(x_ref, w_ref, b_ref, dst_ref, scale):
        xp_ref[0:SUBLANES, :] = jnp.zeros((SUBLANES, dk), F32)
        xp_ref[SUBLANES:, :] = x_ref[...].astype(F32)
        w = w_ref[...]
        acc = b_ref[...] + xp_ref[pl.ds(SUBLANES, seq), :] * w[3:4, :]
        for d in (1, 2, 3):
            acc = acc + xp_ref[pl.ds(SUBLANES - d, seq), :] * w[3 - d:4 - d, :]
        y = acc * jax.nn.sigmoid(acc)
        if scale != 1.0:
            y = y * scale
        dst_ref[...] = y.astype(BF16)

    conv_silu(q_ref, cwq_ref, cbq_ref, qs_ref, dk ** -0.5)
    conv_silu(k_ref, cwk_ref, cbk_ref, ks_ref, 1.0)

    rows = lax.broadcasted_iota(jnp.int32, (L, L), 0)
    cols = lax.broadcasted_iota(jnp.int32, (L, L), 1)
    causal = cols <= rows
    tri = jnp.where(rows <= cols, 1.0, 0.0).astype(BF16)
    ones_rows = jnp.ones((SUBLANES, L), BF16)

    ncp = -(-nc // SUBLANES) * SUBLANES
    pad = [jnp.zeros((ncp - nc, L), F32)] if ncp > nc else []
    g = g_ref[...]
    li_all = jnp.concatenate([g[0:1, c * L:(c + 1) * L] for c in range(nc)] + pad, axis=0)
    lf_all = jnp.concatenate([_log_sigmoid(g[1:2, c * L:(c + 1) * L]) for c in range(nc)] + pad, axis=0)
    parts = jnp.concatenate(_split3(lf_all), axis=0).astype(BF16)
    cs = jnp.dot(parts, tri, preferred_element_type=F32)
    b_all = cs[0:ncp] + cs[ncp:2 * ncp] + cs[2 * ncp:3 * ncp]
    d_all = li_all - b_all
    at = jnp.concatenate(_split3(b_all) + _split3(d_all), axis=0).astype(BF16)
    er = lax.broadcasted_iota(jnp.int32, (6 * ncp, LANES), 0)
    ec = lax.broadcasted_iota(jnp.int32, (6 * ncp, LANES), 1)
    sel = jnp.where(ec == (er % ncp) + jnp.where(er >= 3 * ncp, ncp, 0), 1.0, 0.0).astype(BF16)
    cols_f = lax.dot_general(at, sel, (((0,), (0,)), ((), ())), preferred_element_type=F32)

    ct = jnp.zeros((dk, dv), F32)
    n_st = jnp.zeros((1, dk), F32)
    m_prev = jnp.zeros((1, 1), F32)
    nw = nw_ref[...]

    for c in range(nc):
        r0 = c * L
        q = qs_ref[pl.ds(r0, L), :]
        k = ks_ref[pl.ds(r0, L), :]
        v = v_ref[pl.ds(r0, L), :]
        d_row = d_all[c:c + 1, :]
        b_end = b_all[c:c + 1, L - 1:L]
        b_col = cols_f[:, c:c + 1]
        d_col = cols_f[:, ncp + c:ncp + c + 1]

        log_d = b_col + d_row
        intra_max = jnp.max(jnp.where(causal, log_d, -jnp.inf), axis=-1, keepdims=True)
        inter_log = b_col + m_prev
        m_t = jnp.maximum(inter_log, intra_max)
        d_mat = jnp.where(causal, jnp.exp(log_d - m_t), 0.0)
        a_t = jnp.exp(inter_log - m_t)

        qk = lax.dot_general(q, k, (((1,), (1,)), ((), ())), preferred_element_type=F32)
        s = qk * d_mat
        num = jnp.dot(s.astype(BF16), v, preferred_element_type=F32)
        num = num + a_t * jnp.dot(q, ct.astype(BF16), preferred_element_type=F32)
        qn = jnp.sum(q.astype(F32) * n_st, axis=-1, keepdims=True)
        den = jnp.sum(s, axis=-1, keepdims=True) + a_t * qn
        h = num / jnp.maximum(jnp.abs(den), jnp.exp(-m_t))
        hn = h * lax.rsqrt(jnp.mean(h * h, axis=-1, keepdims=True) + EPS) * nw
        og = og_ref[pl.ds(r0, L), :].astype(F32)
        out_ref[pl.ds(r0, L), :] = (hn * jax.nn.sigmoid(og)).astype(BF16)

        w_end = b_end + d_row
        m_loc = jnp.max(w_end, axis=-1, keepdims=True)
        e_col = jnp.exp(b_end + d_col - m_loc)
        ke = (k.astype(F32) * e_col).astype(BF16)
        c_loc = lax.dot_general(ke, v, (((0,), (0,)), ((), ())), preferred_element_type=F32)
        n_loc = jnp.dot(ones_rows, ke, preferred_element_type=F32)[0:1, :]
        m_new = jnp.maximum(b_end + m_prev, m_loc)
        a = jnp.exp(b_end + m_prev - m_new)
        gsc = jnp.exp(m_loc - m_new)
        ct = a * ct + gsc * c_loc
        n_st = a * n_st + gsc * n_loc
        m_prev = m_new


def _mlstm(qk, v, og, gates, conv_w, conv_b, norm_w, *, batch, seq):
    t, w = qk.shape
    nh = MLSTM_HEADS
    dk = w // (2 * nh)
    dv = v.shape[1] // nh
    L = MLSTM_CHUNK
    body = functools.partial(_mlstm_body, seq=seq, dk=dk, dv=dv)
    return pl.pallas_call(
        body,
        grid=(batch, nh),
        in_specs=[
            pl.BlockSpec((seq, dk), lambda b, h: (b, h)),
            pl.BlockSpec((seq, dk), lambda b, h: (b, nh + h)),
            pl.BlockSpec((seq, dv), lambda b, h: (b, h)),
            pl.BlockSpec((seq, dv), lambda b, h: (b, h)),
            pl.BlockSpec((None, 2, seq), lambda b, h: (h, 0, b)),
            pl.BlockSpec((conv_w.shape[0], dk), lambda b, h: (0, h)),
            pl.BlockSpec((conv_w.shape[0], dk), lambda b, h: (0, nh + h)),
            pl.BlockSpec((1, dk), lambda b, h: (0, h)),
            pl.BlockSpec((1, dk), lambda b, h: (0, nh + h)),
            pl.BlockSpec((1, dv), lambda b, h: (0, h)),
        ],
        out_specs=pl.BlockSpec((seq, dv), lambda b, h: (b, h)),
        out_shape=jax.ShapeDtypeStruct((t, nh * dv), BF16),
        scratch_shapes=[
            pltpu.VMEM((seq + SUBLANES, dk), F32),
            pltpu.VMEM((seq, dk), BF16),
            pltpu.VMEM((seq, dk), BF16),
        ],
        compiler_params=_params(("parallel", "parallel")),
        name="mlstm",
    )(qk, qk, v, og, gates, conv_w, conv_w, conv_b, conv_b, norm_w)


def _s5prep_body(lr_ref, li_ref, ldt_ref, br_ref, bi_ref, lbr_ref, lbi_ref, bbr_ref, bbi_ref):
    lr = lr_ref[...]
    li = li_ref[...]
    dt = jnp.exp(ldt_ref[...])
    mag = jnp.exp(lr * dt)
    ang = li * dt
    lbr = mag * jnp.cos(ang)
    lbi = mag * jnp.sin(ang)
    nr = lbr - 1.0
    den = lr * lr + li * li
    cr = (nr * lr + lbi * li) / den
    ci = (lbi * lr - nr * li) / den
    br = br_ref[...]
    bi = bi_ref[...]
    lbr_ref[...] = lbr
    lbi_ref[...] = lbi
    bbr_ref[...] = cr * br - ci * bi
    bbi_ref[...] = cr * bi + ci * br


def _s5prep(lr, li, ldt, br, bi):
    shp = jax.ShapeDtypeStruct(lr.shape, F32)
    return pl.pallas_call(_s5prep_body, out_shape=[shp, shp, shp, shp], name="s5prep")(lr, li, ldt, br, bi)


def _s5_body(u_ref, bm_ref, cm_ref, lr_ref, li_ref, dsk_ref, wglu_ref, bglu_ref, o_ref,
             usb_ref, bu_ref, y_ref, st_ref, *, ts, ngb):
    half = S5_GROUPS_PER_BLOCK * S5_STATE
    cb = S5_GROUPS_PER_BLOCK * S5_GROUP

    @pl.when(pl.program_id(0) == 0)
    def _():
        st_ref[...] = jnp.zeros_like(st_ref)

    for s in range(ts):
        usb_ref[s * SUBLANES:(s + 1) * SUBLANES, :] = u_ref[:, s, :]

    def project_in(gb):
        ug = usb_ref[:, gb * cb:(gb + 1) * cb]
        bu_ref[gb % 2] = jnp.dot(ug.astype(BF16), bm_ref[gb], preferred_element_type=F32)

    project_in(0)
    for gb in range(ngb):
        if gb + 1 < ngb:
            project_in(gb + 1)
        buf = bu_ref.at[gb % 2]
        lrb = jnp.broadcast_to(lr_ref[gb], (SUBLANES, half))
        lib = jnp.broadcast_to(li_ref[gb], (SUBLANES, half))
        xr = st_ref[gb, :, 0:half]
        xi = st_ref[gb, :, half:2 * half]
        for s in range(ts):
            rs = slice(s * SUBLANES, (s + 1) * SUBLANES)
            nr = lrb * xr - lib * xi + buf[rs, 0:half]
            ni = lrb * xi + lib * xr + buf[rs, half:2 * half]
            buf[rs, 0:half] = nr
            buf[rs, half:2 * half] = ni
            xr, xi = nr, ni
        st_ref[gb, :, 0:half] = xr
        st_ref[gb, :, half:2 * half] = xi

        yg = jnp.dot(buf[...].astype(BF16), cm_ref[gb], preferred_element_type=F32)
        yg = yg + dsk_ref[:, gb * cb:(gb + 1) * cb] * usb_ref[:, gb * cb:(gb + 1) * cb]
        y_ref[:, gb * cb:(gb + 1) * cb] = jax.nn.gelu(yg)

    y = y_ref[...]
    z = jnp.dot(y.astype(BF16), wglu_ref[...], preferred_element_type=F32) + bglu_ref[...]
    usb_ref[...] = y * jax.nn.sigmoid(z)

    for s in range(ts):
        o_ref[:, s, :] = usb_ref[s * SUBLANES:(s + 1) * SUBLANES, :]


def _s5(u, bm, cm, lbr, lbi, dsk, wglu, bglu, *, ts=64):
    batch, seq, width = u.shape
    ngb = bm.shape[0]
    half = S5_GROUPS_PER_BLOCK * S5_STATE
    body = functools.partial(_s5_body, ts=ts, ngb=ngb)
    full = lambda a: pl.BlockSpec(a.shape, lambda i, n=a.ndim: (0,) * n)
    return pl.pallas_call(
        body,
        grid=(seq // ts,),
        in_specs=[pl.BlockSpec((batch, ts, width), lambda i: (0, i, 0)),
                  full(bm), full(cm), full(lbr), full(lbi), full(dsk), full(wglu), full(bglu)],
        out_specs=pl.BlockSpec((batch, ts, width), lambda i: (0, i, 0)),
        out_shape=jax.ShapeDtypeStruct((batch, seq, width), F32),
        scratch_shapes=[
            pltpu.VMEM((ts * batch, width), F32),
            pltpu.VMEM((2, ts * batch, 2 * half), F32),
            pltpu.VMEM((ts * batch, width), F32),
            pltpu.VMEM((ngb, batch, 2 * half), F32),
        ],
        compiler_params=_params(("arbitrary",)),
        name="s5",
    )(u, bm, cm, lbr, lbi, dsk, wglu, bglu)


def _s5_weights(lam_re, lam_im, log_dt, b_re, b_im, c_re, c_im):
    g, p = lam_re.shape
    hch = b_re.shape[-1]
    gpb = S5_GROUPS_PER_BLOCK
    ngb = g // gpb
    rep = lambda a: jnp.repeat(a, hch, axis=0)
    to_rows = lambda a: jnp.transpose(a, (0, 2, 1)).reshape(g * hch, p)
    ldt = jnp.broadcast_to(log_dt[:, None], (g, p))
    lbr, lbi, bbr, bbi = _s5prep(rep(lam_re), rep(lam_im), rep(ldt), to_rows(b_re), to_rows(b_im))
    lbr = lbr[::hch].reshape(ngb, 1, gpb * p)
    lbi = lbi[::hch].reshape(ngb, 1, gpb * p)
    eye = jnp.eye(gpb, dtype=F32)

    def in_blockdiag(a):
        a = a.reshape(ngb, gpb, hch, p)
        return (a[:, :, :, None, :] * eye[None, :, None, :, None]).reshape(ngb, gpb * hch, gpb * p)

    def out_blockdiag(a):
        a = jnp.transpose(a, (0, 2, 1)).reshape(ngb, gpb, p, hch)
        return (a[:, :, :, None, :] * eye[None, :, None, :, None]).reshape(ngb, gpb * p, gpb * hch)

    bm = jnp.concatenate([in_blockdiag(bbr), in_blockdiag(bbi)], axis=2).astype(BF16)
    cm = jnp.concatenate([out_blockdiag(c_re), out_blockdiag(-c_im)], axis=1).astype(BF16)
    return bm, cm, lbr, lbi


def _outproj_body(x_ref, hm_ref, hs_ref, wa_ref, wb_ref, o_ref):
    acc = jnp.dot(hm_ref[...], wa_ref[...], preferred_element_type=F32)
    acc = acc + jnp.dot(hs_ref[...].astype(BF16), wb_ref[...], preferred_element_type=F32)
    o_ref[...] = x_ref[...] + acc


def _outproj(x, hm, hs, wa, wb, *, tm=512):
    t, d = x.shape
    w = hm.shape[1]
    return pl.pallas_call(
        _outproj_body,
        grid=(t // tm,),
        in_specs=[
            pl.BlockSpec((tm, d), lambda i: (i, 0)),
            pl.BlockSpec((tm, w), lambda i: (i, 0)),
            pl.BlockSpec((tm, w), lambda i: (i, 0)),
            pl.BlockSpec(wa.shape, lambda i: (0, 0)),
            pl.BlockSpec(wb.shape, lambda i: (0, 0)),
        ],
        out_specs=pl.BlockSpec((tm, d), lambda i: (i, 0)),
        out_shape=jax.ShapeDtypeStruct((t, d), F32),
        compiler_params=_params(("parallel",)),
        name="outproj",
    )(x, hm, hs, wa, wb)


def kernel(x, ffn1_norm, ffn1_w1, ffn1_w3, ffn1_w2, mix_norm, w_in, conv_w, conv_b, b_i, b_f, mlstm_norm, lam_re, lam_im, log_dt, b_re, b_im, c_re, c_im, d_skip, w_glu, b_glu, w_out, ffn2_norm, ffn2_w1, ffn2_w3, ffn2_w2, final_norm):
    batch, seq, d = x.shape
    depth = ffn1_norm.shape[0]
    nh = MLSTM_HEADS
    qk_cols = conv_w.shape[-1]
    mw = mlstm_norm.shape[-1]
    gate0 = qk_cols + 2 * mw
    sw = w_glu.shape[-1]
    row = lambda a: a.reshape(1, -1).astype(F32)
    xt = x.reshape(batch * seq, d)
    gfin = row(final_norm)
    for l in range(depth):
        xt = _ffn(xt, row(ffn1_norm[l]), ffn1_w1[l].astype(BF16), ffn1_w3[l].astype(BF16),
                  ffn1_w2[l].astype(BF16), gfin, final_norm=False)

        wl = w_in[l]
        w_main = jnp.concatenate([wl[:, :gate0], wl[:, gate0 + 2 * nh:]], axis=1).astype(BF16)
        wg = wl[:, gate0:gate0 + 2 * nh]
        wg_t = jnp.stack([wg[:, :nh], wg[:, nh:]], axis=-1).reshape(d, 2 * nh).T.astype(BF16)
        bg = jnp.stack([b_i[l], b_f[l]], axis=-1).reshape(2 * nh, 1).astype(F32)
        qk, v, og, u, gates = _inproj(xt, row(mix_norm[l]), w_main, wg_t, bg)

        hm = _mlstm(qk, v, og, gates, conv_w[l].astype(F32), row(conv_b[l]), row(mlstm_norm[l]),
                    batch=batch, seq=seq)

        bm, cm, lbr, lbi = _s5_weights(lam_re[l], lam_im[l], log_dt[l], b_re[l], b_im[l], c_re[l], c_im[l])
        hs = _s5(u.reshape(batch, seq, sw), bm, cm, lbr, lbi, row(d_skip[l]), w_glu[l].astype(BF16), row(b_glu[l]))

        wo = w_out[l].astype(BF16)
        xt = _outproj(xt, hm, hs.reshape(batch * seq, sw), wo[:mw], wo[mw:])

        xt = _ffn(xt, row(ffn2_norm[l]), ffn2_w1[l].astype(BF16), ffn2_w3[l].astype(BF16),
                  ffn2_w2[l].astype(BF16), gfin, final_norm=(l == depth - 1))
    return xt.reshape(batch, seq, d)
```

```python
import functools

import jax
import jax.numpy as jnp
import numpy as np
from jax import lax
from jax.experimental import pallas as pl
from jax.experimental.pallas import tpu as pltpu

F32 = jnp.float32
BF16 = jnp.bfloat16

EPS = 1e-6
MLSTM_HEADS = 4
S5_GROUP = 16
S5_STATE = 64
S5_GROUPS_PER_BLOCK = 8
LANES = 128
SUBLANES = 8
MLSTM_CHUNK = 256
VMEM_LIMIT_BYTES = 56 * 1024 * 1024


def _rms(x, g):
    return x * lax.rsqrt(jnp.mean(x * x, axis=-1, keepdims=True) + EPS) * g


def _params(sem):
    return pltpu.CompilerParams(dimension_semantics=sem, vmem_limit_bytes=VMEM_LIMIT_BYTES)


def _ffn_body(x_ref, g_ref, w1_ref, w3_ref, w2_ref, gf_ref, o_ref, xn_ref, *, n_ff, final_norm):
    j = pl.program_id(1)

    @pl.when(j == 0)
    def _():
        x = x_ref[...]
        xn_ref[...] = _rms(x, g_ref[...]).astype(BF16)
        o_ref[...] = x

    xn = xn_ref[...]
    a = jnp.dot(xn, w1_ref[...].astype(BF16), preferred_element_type=F32)
    b = jnp.dot(xn, w3_ref[...].astype(BF16), preferred_element_type=F32)
    h = (a * jax.nn.sigmoid(a)) * (b * 0.5)
    o_ref[...] += jnp.dot(h.astype(BF16), w2_ref[...].astype(BF16), preferred_element_type=F32)

    if final_norm:
        @pl.when(j == n_ff - 1)
        def _():
            o_ref[...] = _rms(o_ref[...], gf_ref[...])


def _ffn(x, g, w1, w3, w2, gf, *, final_norm, tm=1024, tf=256):
    t, d = x.shape
    dff = w1.shape[1]
    n_ff = dff // tf
    body = functools.partial(_ffn_body, n_ff=n_ff, final_norm=final_norm)
    return pl.pallas_call(
        body,
        grid=(t // tm, n_ff),
        in_specs=[
            pl.BlockSpec((tm, d), lambda i, j: (i, 0)),
            pl.BlockSpec((1, d), lambda i, j: (0, 0)),
            pl.BlockSpec((d, tf), lambda i, j: (0, j)),
            pl.BlockSpec((d, tf), lambda i, j: (0, j)),
            pl.BlockSpec((tf, d), lambda i, j: (j, 0)),
            pl.BlockSpec((1, d), lambda i, j: (0, 0)),
        ],
        out_specs=pl.BlockSpec((tm, d), lambda i, j: (i, 0)),
        out_shape=jax.ShapeDtypeStruct((t, d), F32),
        scratch_shapes=[pltpu.VMEM((tm, d), BF16)],
        compiler_params=_params(("parallel", "arbitrary")),
        name="ffn_final" if final_norm else "ffn",
    )(x, g, w1, w3, w2, gf)


def _inproj_body(x_ref, g_ref, wa_ref, wu_ref, wg_ref, bg_ref, qk_ref, v_ref, o_ref, u_ref, gates_ref, hn_ref,
                 *, nsub):
    j = pl.program_id(1)

    @pl.when(j == 0)
    def _():
        hn = _rms(x_ref[...], g_ref[...]).astype(BF16)
        hn_ref[...] = hn
        gt = lax.dot_general(wg_ref[...], hn, (((1,), (1,)), ((), ())), preferred_element_type=F32)
        gt = gt + bg_ref[...]
        for h in range(MLSTM_HEADS):
            gates_ref[h] = gt[2 * h:2 * h + 2, :]

    def project(w_ref):
        return jnp.dot(hn_ref[...], w_ref[...].astype(BF16), preferred_element_type=F32)

    @pl.when(j < nsub)
    def _():
        qk_ref[...] = project(wa_ref).astype(BF16)

    @pl.when((j >= nsub) & (j < 2 * nsub))
    def _():
        v_ref[...] = project(wa_ref).astype(BF16)

    @pl.when((j >= 2 * nsub) & (j < 3 * nsub))
    def _():
        o_ref[...] = project(wa_ref).astype(BF16)

    @pl.when(j >= 3 * nsub)
    def _():
        u_ref[...] = project(wu_ref)


def _inproj(x, g, w_in, w_u, wg_t, bg, *, tm=1024, tw=512):
    t, d = x.shape
    w = w_u.shape[1]
    nsub = w // tw
    body = functools.partial(_inproj_body, nsub=nsub)

    def blk(k):
        return pl.BlockSpec((tm, tw), lambda i, j: (i, jnp.clip(j - k * nsub, 0, nsub - 1)))

    return pl.pallas_call(
        body,
        grid=(t // tm, 4 * nsub),
        in_specs=[
            pl.BlockSpec((tm, d), lambda i, j: (i, 0)),
            pl.BlockSpec((1, d), lambda i, j: (0, 0)),
            pl.BlockSpec((d, tw), lambda i, j: (0, jnp.minimum(j, 3 * nsub - 1))),
            pl.BlockSpec((d, tw), lambda i, j: (0, jnp.clip(j - 3 * nsub, 0, nsub - 1))),
            pl.BlockSpec((2 * MLSTM_HEADS, d), lambda i, j: (0, 0)),
            pl.BlockSpec((2 * MLSTM_HEADS, 1), lambda i, j: (0, 0)),
        ],
        out_specs=[blk(0), blk(1), blk(2), blk(3),
                   pl.BlockSpec((MLSTM_HEADS, 2, tm), lambda i, j: (0, 0, i))],
        out_shape=[
            jax.ShapeDtypeStruct((t, w), BF16),
            jax.ShapeDtypeStruct((t, w), BF16),
            jax.ShapeDtypeStruct((t, w), BF16),
            jax.ShapeDtypeStruct((t, w), F32),
            jax.ShapeDtypeStruct((MLSTM_HEADS, 2, t), F32),
        ],
        scratch_shapes=[pltpu.VMEM((tm, d), BF16)],
        compiler_params=_params(("parallel", "arbitrary")),
        name="inproj",
    )(x, g, w_in, w_u, wg_t, bg)


def _split3(x):
    hi = x.astype(BF16).astype(F32)
    r1 = x - hi
    mid = r1.astype(BF16).astype(F32)
    lo = (r1 - mid).astype(BF16).astype(F32)
    return [hi, mid, lo]


def _log_sigmoid(x):
    return jnp.minimum(x, 0.0) - jnp.log(1.0 + jnp.exp(-jnp.abs(x)))


def _mlstm_body(q_ref, k_ref, v_ref, og_ref, g_ref, cwq_ref, cwk_ref, cbq_ref, cbk_ref, nw_ref,
                sel_ref, out_ref, xp_ref, qs_ref, ks_ref, cols_ref, *, seq, dk, dv):
    L = MLSTM_CHUNK
    nc = seq // L

    def conv_silu(x_ref, w_ref, b_ref, dst_ref, scale):
        xp_ref[0:SUBLANES, :] = jnp.zeros((SUBLANES, dk), F32)
        xp_ref[SUBLANES:, :] = x_ref[...].astype(F32)
        w = w_ref[...]
        acc = b_ref[...] + xp_ref[pl.ds(SUBLANES, seq), :] * w[3:4, :]
        for d in (1, 2, 3):
            acc = acc + xp_ref[pl.ds(SUBLANES - d, seq), :] * w[3 - d:4 - d, :]
        y = acc * jax.nn.sigmoid(acc)
        if scale != 1.0:
            y = y * scale
        dst_ref[...] = y.astype(BF16)

    conv_silu(q_ref, cwq_ref, cbq_ref, qs_ref, dk ** -0.5)
    conv_silu(k_ref, cwk_ref, cbk_ref, ks_ref, 1.0)

    rows = lax.broadcasted_iota(jnp.int32, (L, L), 0)
    cols = lax.broadcasted_iota(jnp.int32, (L, L), 1)
    causal = cols <= rows
    tri = jnp.where(rows <= cols, 1.0, 0.0).astype(BF16)

    ncp = -(-nc // SUBLANES) * SUBLANES
    pad = [jnp.zeros((ncp - nc, L), F32)] if ncp > nc else []
    g = g_ref[...]
    li_all = jnp.concatenate([g[0:1, c * L:(c + 1) * L] for c in range(nc)] + pad, axis=0)
    lf_all = jnp.concatenate([_log_sigmoid(g[1:2, c * L:(c + 1) * L]) for c in range(nc)] + pad, axis=0)
    parts = jnp.concatenate(_split3(lf_all), axis=0).astype(BF16)
    cs = jnp.dot(parts, tri, preferred_element_type=F32)
    b_all = cs[0:ncp] + cs[ncp:2 * ncp] + cs[2 * ncp:3 * ncp]
    d_all = li_all - b_all
    lane = lax.broadcasted_iota(jnp.int32, (ncp, L), 1)
    cm_all = d_all
    shift = 1
    while shift < L:
        cm_all = jnp.maximum(cm_all, jnp.where(lane >= shift, pltpu.roll(cm_all, shift, 1), -jnp.inf))
        shift *= 2
    at = jnp.concatenate(_split3(b_all) + _split3(d_all) + _split3(cm_all), axis=0).astype(BF16)
    cols_ref[...] = lax.dot_general(at, sel_ref[...], (((0,), (0,)), ((), ())), preferred_element_type=F32)

    ext = 2 * LANES
    rep = lambda a: jnp.concatenate([a] * (dv // LANES), axis=1)
    ones_l = jnp.ones((L, ext), BF16)
    mean_w = jnp.full((dv, LANES), 1.0 / dv, BF16)
    ct_ext = jnp.zeros((dk, dv + ext), F32)
    m_prev = jnp.zeros((1, 1), F32)
    nw = nw_ref[...]

    for c in range(nc):
        r0 = c * L
        q = qs_ref[pl.ds(r0, L), :]
        k = ks_ref[pl.ds(r0, L), :]
        v_ext = jnp.concatenate([v_ref[pl.ds(r0, L), :], ones_l], axis=1)
        d_row = d_all[c:c + 1, :]
        b_end = b_all[c:c + 1, L - 1:L]
        b_c = cols_ref[:, c * LANES:(c + 1) * LANES]
        d_c = cols_ref[:, (nc + c) * LANES:(nc + c + 1) * LANES]
        cm_c = cols_ref[:, (2 * nc + c) * LANES:(2 * nc + c + 1) * LANES]

        m_t = b_c + jnp.maximum(m_prev, cm_c)
        a_t = jnp.exp(b_c + m_prev - m_t)
        e_mt = jnp.exp(-m_t)
        arg = jnp.concatenate([b_c - m_t] * (L // LANES), axis=1) + d_row
        d_mat = jnp.where(causal, jnp.exp(arg), 0.0)

        qk = lax.dot_general(q, k, (((1,), (1,)), ((), ())), preferred_element_type=F32)
        s_b = (qk * d_mat).astype(BF16)
        sv = jnp.dot(s_b, v_ext, preferred_element_type=F32)
        qc = jnp.dot(q, ct_ext.astype(BF16), preferred_element_type=F32)
        num = sv[:, :dv] + rep(a_t) * qc[:, :dv]
        den = sv[:, dv:dv + LANES] + a_t * qc[:, dv:dv + LANES]
        inv = 1.0 / jnp.maximum(jnp.abs(den), e_mt)
        h = num * rep(inv)
        msq = jnp.dot((h * h).astype(BF16), mean_w, preferred_element_type=F32)
        hn = h * rep(lax.rsqrt(msq + EPS)) * nw
        og = og_ref[pl.ds(r0, L), :].astype(F32)
        out_ref[pl.ds(r0, L), :] = (hn * jax.nn.sigmoid(og)).astype(BF16)

        w_end = b_end + d_row
        m_loc = jnp.max(w_end, axis=-1, keepdims=True)
        e_c = jnp.exp(b_end + d_c - m_loc)
        ke = (k.astype(F32) * e_c).astype(BF16)
        c_loc = lax.dot_general(ke, v_ext, (((0,), (0,)), ((), ())), preferred_element_type=F32)
        m_new = jnp.maximum(b_end + m_prev, m_loc)
        ct_ext = jnp.exp(b_end + m_prev - m_new) * ct_ext + jnp.exp(m_loc - m_new) * c_loc
        m_prev = m_new


def _mlstm_selector(nc):
    ncp = -(-nc // SUBLANES) * SUBLANES
    sel = np.zeros((9 * ncp, 3 * nc * LANES), np.float32)
    for kind in range(3):
        for part in range(3):
            for c in range(nc):
                sel[(kind * 3 + part) * ncp + c, (kind * nc + c) * LANES:(kind * nc + c + 1) * LANES] = 1.0
    return jnp.asarray(sel, BF16)


def _mlstm(qk, v, og, gates, conv_w, conv_b, norm_w, *, batch, seq):
    t, w = qk.shape
    nh = MLSTM_HEADS
    dk = w // (2 * nh)
    dv = v.shape[1] // nh
    assert dk == LANES and dv % LANES == 0 and seq % MLSTM_CHUNK == 0 and MLSTM_CHUNK % LANES == 0
    nc = seq // MLSTM_CHUNK
    sel = _mlstm_selector(nc)
    body = functools.partial(_mlstm_body, seq=seq, dk=dk, dv=dv)
    return pl.pallas_call(
        body,
        grid=(batch, nh),
        in_specs=[
            pl.BlockSpec((seq, dk), lambda b, h: (b, h)),
            pl.BlockSpec((seq, dk), lambda b, h: (b, nh + h)),
            pl.BlockSpec((seq, dv), lambda b, h: (b, h)),
            pl.BlockSpec((seq, dv), lambda b, h: (b, h)),
            pl.BlockSpec((None, 2, seq), lambda b, h: (h, 0, b)),
            pl.BlockSpec((conv_w.shape[0], dk), lambda b, h: (0, h)),
            pl.BlockSpec((conv_w.shape[0], dk), lambda b, h: (0, nh + h)),
            pl.BlockSpec((1, dk), lambda b, h: (0, h)),
            pl.BlockSpec((1, dk), lambda b, h: (0, nh + h)),
            pl.BlockSpec((1, dv), lambda b, h: (0, h)),
            pl.BlockSpec(sel.shape, lambda b, h: (0, 0)),
        ],
        out_specs=pl.BlockSpec((seq, dv), lambda b, h: (b, h)),
        out_shape=jax.ShapeDtypeStruct((t, nh * dv), BF16),
        scratch_shapes=[
            pltpu.VMEM((seq + SUBLANES, dk), F32),
            pltpu.VMEM((seq, dk), BF16),
            pltpu.VMEM((seq, dk), BF16),
            pltpu.VMEM((MLSTM_CHUNK, 3 * nc * LANES), F32),
        ],
        compiler_params=_params(("parallel", "parallel")),
        name="mlstm",
    )(qk, qk, v, og, gates, conv_w, conv_w, conv_b, conv_b, norm_w, sel)


def _s5prep_body(lr_ref, li_ref, ldt_ref, br_ref, bi_ref, lbr_ref, lbi_ref, bbr_ref, bbi_ref):
    lr = lr_ref[...]
    li = li_ref[...]
    dt = jnp.exp(ldt_ref[...])
    mag = jnp.exp(lr * dt)
    ang = li * dt
    lbr = mag * jnp.cos(ang)
    lbi = mag * jnp.sin(ang)
    nr = lbr - 1.0
    den = lr * lr + li * li
    cr = (nr * lr + lbi * li) / den
    ci = (lbi * lr - nr * li) / den
    br = br_ref[...]
    bi = bi_ref[...]
    lbr_ref[...] = lbr
    lbi_ref[...] = lbi
    bbr_ref[...] = cr * br - ci * bi
    bbi_ref[...] = cr * bi + ci * br


def _s5prep(lr, li, ldt, br, bi):
    shp = jax.ShapeDtypeStruct(lr.shape, F32)
    return pl.pallas_call(_s5prep_body, out_shape=[shp, shp, shp, shp], name="s5prep")(lr, li, ldt, br, bi)


def _s5_body(u_ref, bm_ref, cm_ref, lr_ref, li_ref, dsk_ref, wglu_ref, bglu_ref, o_ref,
             usb_ref, bu_ref, y_ref, st_ref, *, ts, ngb):
    half = S5_GROUPS_PER_BLOCK * S5_STATE
    cb = S5_GROUPS_PER_BLOCK * S5_GROUP

    @pl.when(pl.program_id(0) == 0)
    def _():
        st_ref[...] = jnp.zeros_like(st_ref)

    for s in range(ts):
        usb_ref[s * SUBLANES:(s + 1) * SUBLANES, :] = u_ref[:, s, :]

    def project_in(gb):
        ug = usb_ref[:, gb * cb:(gb + 1) * cb]
        bu_ref[gb % 2] = jnp.dot(ug.astype(BF16), bm_ref[gb], preferred_element_type=F32)

    project_in(0)
    for gb in range(ngb):
        if gb + 1 < ngb:
            project_in(gb + 1)
        buf = bu_ref.at[gb % 2]
        lrb = jnp.broadcast_to(lr_ref[gb], (SUBLANES, half))
        lib = jnp.broadcast_to(li_ref[gb], (SUBLANES, half))
        xr = st_ref[gb, :, 0:half]
        xi = st_ref[gb, :, half:2 * half]
        for s in range(ts):
            rs = slice(s * SUBLANES, (s + 1) * SUBLANES)
            nr = lrb * xr - lib * xi + buf[rs, 0:half]
            ni = lrb * xi + lib * xr + buf[rs, half:2 * half]
            buf[rs, 0:half] = nr
            buf[rs, half:2 * half] = ni
            xr, xi = nr, ni
        st_ref[gb, :, 0:half] = xr
        st_ref[gb, :, half:2 * half] = xi

        yg = jnp.dot(buf[...].astype(BF16), cm_ref[gb], preferred_element_type=F32)
        yg = yg + dsk_ref[:, gb * cb:(gb + 1) * cb] * usb_ref[:, gb * cb:(gb + 1) * cb]
        y_ref[:, gb * cb:(gb + 1) * cb] = jax.nn.gelu(yg)

    y = y_ref[...]
    z = jnp.dot(y.astype(BF16), wglu_ref[...], preferred_element_type=F32) + bglu_ref[...]
    usb_ref[...] = y * jax.nn.sigmoid(z)

    for s in range(ts):
        o_ref[:, s, :] = usb_ref[s * SUBLANES:(s + 1) * SUBLANES, :]


def _s5(u, bm, cm, lbr, lbi, dsk, wglu, bglu, *, ts=64):
    batch, seq, width = u.shape
    ngb = bm.shape[0]
    half = S5_GROUPS_PER_BLOCK * S5_STATE
    body = functools.partial(_s5_body, ts=ts, ngb=ngb)
    full = lambda a: pl.BlockSpec(a.shape, lambda i, n=a.ndim: (0,) * n)
    return pl.pallas_call(
        body,
        grid=(seq // ts,),
        in_specs=[pl.BlockSpec((batch, ts, width), lambda i: (0, i, 0)),
                  full(bm), full(cm), full(lbr), full(lbi), full(dsk), full(wglu), full(bglu)],
        out_specs=pl.BlockSpec((batch, ts, width), lambda i: (0, i, 0)),
        out_shape=jax.ShapeDtypeStruct((batch, seq, width), F32),
        scratch_shapes=[
            pltpu.VMEM((ts * batch, width), F32),
            pltpu.VMEM((2, ts * batch, 2 * half), F32),
            pltpu.VMEM((ts * batch, width), F32),
            pltpu.VMEM((ngb, batch, 2 * half), F32),
        ],
        compiler_params=_params(("arbitrary",)),
        name="s5",
    )(u, bm, cm, lbr, lbi, dsk, wglu, bglu)


def _s5_weights(lam_re, lam_im, log_dt, b_re, b_im, c_re, c_im):
    g, p = lam_re.shape
    hch = b_re.shape[-1]
    gpb = S5_GROUPS_PER_BLOCK
    ngb = g // gpb
    rep = lambda a: jnp.repeat(a, hch, axis=0)
    to_rows = lambda a: jnp.transpose(a, (0, 2, 1)).reshape(g * hch, p)
    ldt = jnp.broadcast_to(log_dt[:, None], (g, p))
    lbr, lbi, bbr, bbi = _s5prep(rep(lam_re), rep(lam_im), rep(ldt), to_rows(b_re), to_rows(b_im))
    lbr = lbr[::hch].reshape(ngb, 1, gpb * p)
    lbi = lbi[::hch].reshape(ngb, 1, gpb * p)
    eye = jnp.eye(gpb, dtype=F32)

    def in_blockdiag(a):
        a = a.reshape(ngb, gpb, hch, p)
        return (a[:, :, :, None, :] * eye[None, :, None, :, None]).reshape(ngb, gpb * hch, gpb * p)

    def out_blockdiag(a):
        a = jnp.transpose(a, (0, 2, 1)).reshape(ngb, gpb, p, hch)
        return (a[:, :, :, None, :] * eye[None, :, None, :, None]).reshape(ngb, gpb * p, gpb * hch)

    bm = jnp.concatenate([in_blockdiag(bbr), in_blockdiag(bbi)], axis=2).astype(BF16)
    cm = jnp.concatenate([out_blockdiag(c_re), out_blockdiag(-c_im)], axis=1).astype(BF16)
    return bm, cm, lbr, lbi


def _outproj_body(x_ref, hm_ref, hs_ref, wa_ref, wb_ref, o_ref):
    acc = jnp.dot(hm_ref[...], wa_ref[...], preferred_element_type=F32)
    acc = acc + jnp.dot(hs_ref[...].astype(BF16), wb_ref[...], preferred_element_type=F32)
    o_ref[...] = x_ref[...] + acc


def _outproj(x, hm, hs, wa, wb, *, tm=512):
    t, d = x.shape
    w = hm.shape[1]
    return pl.pallas_call(
        _outproj_body,
        grid=(t // tm,),
        in_specs=[
            pl.BlockSpec((tm, d), lambda i: (i, 0)),
            pl.BlockSpec((tm, w), lambda i: (i, 0)),
            pl.BlockSpec((tm, w), lambda i: (i, 0)),
            pl.BlockSpec(wa.shape, lambda i: (0, 0)),
            pl.BlockSpec(wb.shape, lambda i: (0, 0)),
        ],
        out_specs=pl.BlockSpec((tm, d), lambda i: (i, 0)),
        out_shape=jax.ShapeDtypeStruct((t, d), F32),
        compiler_params=_params(("parallel",)),
        name="outproj",
    )(x, hm, hs, wa, wb)


def kernel(x, ffn1_norm, ffn1_w1, ffn1_w3, ffn1_w2, mix_norm, w_in, conv_w, conv_b, b_i, b_f, mlstm_norm, lam_re, lam_im, log_dt, b_re, b_im, c_re, c_im, d_skip, w_glu, b_glu, w_out, ffn2_norm, ffn2_w1, ffn2_w3, ffn2_w2, final_norm):
    batch, seq, d = x.shape
    depth = ffn1_norm.shape[0]
    nh = MLSTM_HEADS
    qk_cols = conv_w.shape[-1]
    mw = mlstm_norm.shape[-1]
    gate0 = qk_cols + 2 * mw
    sw = w_glu.shape[-1]
    row = lambda a: a.reshape(1, -1).astype(F32)
    xt = x.reshape(batch * seq, d)
    gfin = row(final_norm)
    for l in range(depth):
        xt = _ffn(xt, row(ffn1_norm[l]), ffn1_w1[l], ffn1_w3[l], ffn1_w2[l], gfin, final_norm=False)

        wl = w_in[l]
        w_u = wl[:, gate0 + 2 * nh:]
        wg = wl[:, gate0:gate0 + 2 * nh]
        wg_t = jnp.stack([wg[:, :nh], wg[:, nh:]], axis=-1).reshape(d, 2 * nh).T.astype(BF16)
        bg = jnp.stack([b_i[l], b_f[l]], axis=-1).reshape(2 * nh, 1).astype(F32)
        qk, v, og, u, gates = _inproj(xt, row(mix_norm[l]), wl, w_u, wg_t, bg)

        hm = _mlstm(qk, v, og, gates, conv_w[l].astype(F32), row(conv_b[l]), row(mlstm_norm[l]),
                    batch=batch, seq=seq)

        bm, cm, lbr, lbi = _s5_weights(lam_re[l], lam_im[l], log_dt[l], b_re[l], b_im[l], c_re[l], c_im[l])
        hs = _s5(u.reshape(batch, seq, sw), bm, cm, lbr, lbi, row(d_skip[l]), w_glu[l].astype(BF16), row(b_glu[l]))

        wo = w_out[l].astype(BF16)
        xt = _outproj(xt, hm, hs.reshape(batch * seq, sw), wo[:mw], wo[mw:])

        xt = _ffn(xt, row(ffn2_norm[l]), ffn2_w1[l], ffn2_w3[l], ffn2_w2[l], gfin, final_norm=(l == depth - 1))
    return xt.reshape(batch, seq, d)
```

```python
import functools

import jax
import jax.numpy as jnp
import numpy as np
from jax import lax
from jax.experimental import pallas as pl
from jax.experimental.pallas import tpu as pltpu

F32 = jnp.float32
BF16 = jnp.bfloat16

EPS = 1e-6
MLSTM_HEADS = 4
S5_GROUP = 16
S5_STATE = 64
S5_GROUPS_PER_BLOCK = 8
LANES = 128
SUBLANES = 8
MLSTM_CHUNK = 256
VMEM_LIMIT_BYTES = 58 * 1024 * 1024


def _rms(x, g):
    return x * lax.rsqrt(jnp.mean(x * x, axis=-1, keepdims=True) + EPS) * g


def _params(sem):
    return pltpu.CompilerParams(dimension_semantics=sem, vmem_limit_bytes=VMEM_LIMIT_BYTES)


def _row_cast_split(n_rows, n_i, n_ff):
    kj = 1
    while kj * 2 <= min(n_ff, 8) and (n_rows // (n_i * kj * 2)) % 16 == 0 and n_rows % (n_i * kj * 2) == 0:
        kj *= 2
    assert n_rows % (n_i * kj) == 0 and (n_rows // (n_i * kj)) % 16 == 0
    return kj


def _ffn_body(*refs, n_ff, final_norm, n_tile, row_kj, gate0, gate_w):
    n_row = len(row_kj)
    x_ref, g_ref, w1_ref, w3_ref, w2_ref, gf_ref = refs[:6]
    tile_src = refs[6:6 + n_tile]
    row_src = refs[6 + n_tile:6 + n_tile + n_row]
    o_ref = refs[6 + n_tile + n_row]
    tile_dst = refs[7 + n_tile + n_row:7 + 2 * n_tile + n_row]
    row_dst = refs[7 + 2 * n_tile + n_row:-1]
    xn_ref = refs[-1]
    j = pl.program_id(1)

    @pl.when(j == 0)
    def _():
        x = x_ref[...]
        xn_ref[...] = _rms(x, g_ref[...]).astype(BF16)
        o_ref[...] = x

    xn = xn_ref[...]
    a = jnp.dot(xn, w1_ref[...], preferred_element_type=F32)
    b = jnp.dot(xn, w3_ref[...], preferred_element_type=F32)
    h = (a * jax.nn.sigmoid(a)) * (b * 0.5)
    o_ref[...] += jnp.dot(h.astype(BF16), w2_ref[...], preferred_element_type=F32)

    if final_norm:
        @pl.when(j == n_ff - 1)
        def _():
            o_ref[...] = _rms(o_ref[...], gf_ref[...])

    for src, dst in zip(tile_src, tile_dst):
        dst[...] = src[...].astype(BF16)
    for r, (src, kj) in enumerate(zip(row_src, row_kj)):
        @pl.when(j < kj)
        def _(src=src, r=r):
            if r == 0 and gate0 is not None:
                main_ref, gate_ref = row_dst[0], row_dst[1]
                main_ref[:, :gate0] = src[:, :gate0].astype(BF16)
                main_ref[:, gate0:] = src[:, gate0 + gate_w:].astype(BF16)
                gate_ref[...] = src[:, gate0:gate0 + LANES].astype(BF16)
            else:
                row_dst[r + (1 if gate0 is not None else 0)][...] = src[...].astype(BF16)


def _ffn(x, g, w1, w3, w2, gf, *, final_norm, tm=1024, tf=512, layer=0, cast_tiles=(), cast_rows=(), gate0=None,
         gate_w=0):
    t, d = x.shape
    dff = w1.shape[1]
    n_i, n_ff = t // tm, dff // tf
    rb = d // n_i
    in_specs = [
        pl.BlockSpec((tm, d), lambda i, j: (i, 0)),
        pl.BlockSpec((1, d), lambda i, j: (0, 0)),
        pl.BlockSpec((d, tf), lambda i, j: (0, j)),
        pl.BlockSpec((d, tf), lambda i, j: (0, j)),
        pl.BlockSpec((tf, d), lambda i, j: (j, 0)),
        pl.BlockSpec((1, d), lambda i, j: (0, 0)),
    ]
    out_specs = [pl.BlockSpec((tm, d), lambda i, j: (i, 0))]
    out_shape = [jax.ShapeDtypeStruct((t, d), F32)]
    for w in cast_tiles:
        if w.shape[1:] == (d, dff):
            in_specs.append(pl.BlockSpec((None, rb, tf), lambda i, j: (layer, i, j)))
            out_specs.append(pl.BlockSpec((rb, tf), lambda i, j: (i, j)))
        else:
            assert w.shape[1:] == (dff, d) and rb % LANES == 0
            in_specs.append(pl.BlockSpec((None, tf, rb), lambda i, j: (layer, j, i)))
            out_specs.append(pl.BlockSpec((tf, rb), lambda i, j: (j, i)))
        out_shape.append(jax.ShapeDtypeStruct(w.shape[1:], BF16))
    row_kj = []
    for r, w in enumerate(cast_rows):
        n_rows, n_cols = w.shape[1:]
        kj = _row_cast_split(n_rows, n_i, n_ff)
        row_kj.append(kj)
        rr = n_rows // (n_i * kj)
        idx = lambda i, j, kj=kj: (i * kj + jnp.minimum(j, kj - 1), 0)
        in_specs.append(pl.BlockSpec((None, rr, n_cols), lambda i, j, idx=idx: (layer,) + idx(i, j)))
        if r == 0 and gate0 is not None:
            out_specs += [pl.BlockSpec((rr, n_cols - gate_w), idx), pl.BlockSpec((rr, LANES), idx)]
            out_shape += [jax.ShapeDtypeStruct((n_rows, n_cols - gate_w), BF16),
                          jax.ShapeDtypeStruct((n_rows, LANES), BF16)]
        else:
            out_specs.append(pl.BlockSpec((rr, n_cols), idx))
            out_shape.append(jax.ShapeDtypeStruct((n_rows, n_cols), BF16))
    body = functools.partial(_ffn_body, n_ff=n_ff, final_norm=final_norm, n_tile=len(cast_tiles),
                             row_kj=tuple(row_kj), gate0=gate0, gate_w=gate_w)
    outs = pl.pallas_call(
        body,
        grid=(n_i, n_ff),
        in_specs=in_specs,
        out_specs=out_specs,
        out_shape=out_shape,
        scratch_shapes=[pltpu.VMEM((tm, d), BF16)],
        compiler_params=_params(("parallel", "arbitrary")),
        name="ffn_final" if final_norm else "ffn",
    )(x, g, w1, w3, w2, gf, *cast_tiles, *cast_rows)
    return outs[0], outs[1:]


def _inproj_body(x_ref, g_ref, w_ref, wg_ref, bg_ref, qk_ref, v_ref, o_ref, u_ref, gates_ref, hn_ref):
    j = pl.program_id(1)
    nh = MLSTM_HEADS

    @pl.when(j == 0)
    def _():
        hn = _rms(x_ref[...], g_ref[...]).astype(BF16)
        hn_ref[...] = hn
        gt = lax.dot_general(wg_ref[...], hn, (((1,), (1,)), ((), ())), preferred_element_type=F32)
        gt = gt + bg_ref[...]
        for h in range(nh):
            gates_ref[h] = jnp.concatenate([gt[h:h + 1, :], gt[nh + h:nh + h + 1, :]], axis=0)

    res = jnp.dot(hn_ref[...], w_ref[...], preferred_element_type=F32)

    @pl.when(j == 0)
    def _():
        qk_ref[...] = res.astype(BF16)

    @pl.when(j == 1)
    def _():
        v_ref[...] = res.astype(BF16)

    @pl.when(j == 2)
    def _():
        o_ref[...] = res.astype(BF16)

    @pl.when(j == 3)
    def _():
        u_ref[...] = res


def _inproj(x, g, w_main, wg_t, bg, *, tm=1024):
    t, d = x.shape
    w = w_main.shape[1] // 4
    blk = lambda: pl.BlockSpec((tm, w), lambda i, j: (i, 0))
    return pl.pallas_call(
        _inproj_body,
        grid=(t // tm, 4),
        in_specs=[
            pl.BlockSpec((tm, d), lambda i, j: (i, 0)),
            pl.BlockSpec((1, d), lambda i, j: (0, 0)),
            pl.BlockSpec((d, w), lambda i, j: (0, j)),
            pl.BlockSpec((2 * MLSTM_HEADS, d), lambda i, j: (0, 0)),
            pl.BlockSpec((2 * MLSTM_HEADS, 1), lambda i, j: (0, 0)),
        ],
        out_specs=[blk(), blk(), blk(), blk(),
                   pl.BlockSpec((MLSTM_HEADS, 2, tm), lambda i, j: (0, 0, i))],
        out_shape=[
            jax.ShapeDtypeStruct((t, w), BF16),
            jax.ShapeDtypeStruct((t, w), BF16),
            jax.ShapeDtypeStruct((t, w), BF16),
            jax.ShapeDtypeStruct((t, w), F32),
            jax.ShapeDtypeStruct((MLSTM_HEADS, 2, t), F32),
        ],
        scratch_shapes=[pltpu.VMEM((tm, d), BF16)],
        compiler_params=_params(("parallel", "arbitrary")),
        name="inproj",
    )(x, g, w_main, wg_t, bg)


def _split3(x):
    hi = x.astype(BF16).astype(F32)
    r1 = x - hi
    mid = r1.astype(BF16).astype(F32)
    lo = (r1 - mid).astype(BF16).astype(F32)
    return [hi, mid, lo]


def _log_sigmoid(x):
    return jnp.minimum(x, 0.0) - jnp.log(1.0 + jnp.exp(-jnp.abs(x)))


def _mlstm_body(q_ref, k_ref, v_ref, og_ref, g_ref, cwq_ref, cwk_ref, cbq_ref, cbk_ref, nw_ref,
                sel_ref, out_ref, xp_ref, qs_ref, ks_ref, cols_ref, *, seq, dk, dv):
    L = MLSTM_CHUNK
    nc = seq // L

    def conv_silu(x_ref, w_ref, b_ref, dst_ref, scale):
        xp_ref[0:SUBLANES, :] = jnp.zeros((SUBLANES, dk), F32)
        xp_ref[SUBLANES:, :] = x_ref[...].astype(F32)
        w = w_ref[...]
        acc = b_ref[...] + xp_ref[pl.ds(SUBLANES, seq), :] * w[3:4, :]
        for d in (1, 2, 3):
            acc = acc + xp_ref[pl.ds(SUBLANES - d, seq), :] * w[3 - d:4 - d, :]
        y = acc * jax.nn.sigmoid(acc)
        if scale != 1.0:
            y = y * scale
        dst_ref[...] = y.astype(BF16)

    conv_silu(q_ref, cwq_ref, cbq_ref, qs_ref, dk ** -0.5)
    conv_silu(k_ref, cwk_ref, cbk_ref, ks_ref, 1.0)

    rows = lax.broadcasted_iota(jnp.int32, (L, L), 0)
    cols = lax.broadcasted_iota(jnp.int32, (L, L), 1)
    causal = cols <= rows
    tri = jnp.where(rows <= cols, 1.0, 0.0).astype(BF16)

    ncp = -(-nc // SUBLANES) * SUBLANES
    pad = [jnp.zeros((ncp - nc, L), F32)] if ncp > nc else []
    g = g_ref[...]
    li_all = jnp.concatenate([g[0:1, c * L:(c + 1) * L] for c in range(nc)] + pad, axis=0)
    lf_all = jnp.concatenate([_log_sigmoid(g[1:2, c * L:(c + 1) * L]) for c in range(nc)] + pad, axis=0)
    parts = jnp.concatenate(_split3(lf_all), axis=0).astype(BF16)
    cs = jnp.dot(parts, tri, preferred_element_type=F32)
    b_all = cs[0:ncp] + cs[ncp:2 * ncp] + cs[2 * ncp:3 * ncp]
    d_all = li_all - b_all
    lane = lax.broadcasted_iota(jnp.int32, (ncp, L), 1)
    cm_all = d_all
    shift = 1
    while shift < L:
        cm_all = jnp.maximum(cm_all, jnp.where(lane >= shift, pltpu.roll(cm_all, shift, 1), -jnp.inf))
        shift *= 2
    at = jnp.concatenate(_split3(b_all) + _split3(d_all) + _split3(cm_all), axis=0).astype(BF16)
    cols_ref[...] = lax.dot_general(at, sel_ref[...], (((0,), (0,)), ((), ())), preferred_element_type=F32)

    ext = 2 * LANES
    rep = lambda a: jnp.concatenate([a] * (dv // LANES), axis=1)
    ones_l = jnp.ones((L, ext), BF16)
    mean_w = jnp.full((dv, LANES), 1.0 / dv, BF16)
    ct_ext = jnp.zeros((dk, dv + ext), F32)
    m_prev = jnp.zeros((1, 1), F32)
    nw = nw_ref[...]

    for c in range(nc):
        r0 = c * L
        q = qs_ref[pl.ds(r0, L), :]
        k = ks_ref[pl.ds(r0, L), :]
        v_ext = jnp.concatenate([v_ref[pl.ds(r0, L), :], ones_l], axis=1)
        d_row = d_all[c:c + 1, :]
        b_end = b_all[c:c + 1, L - 1:L]
        b_c = cols_ref[:, c * LANES:(c + 1) * LANES]
        d_c = cols_ref[:, (nc + c) * LANES:(nc + c + 1) * LANES]
        cm_c = cols_ref[:, (2 * nc + c) * LANES:(2 * nc + c + 1) * LANES]

        m_t = b_c + jnp.maximum(m_prev, cm_c)
        a_t = jnp.exp(b_c + m_prev - m_t)
        e_mt = jnp.exp(-m_t)
        arg = jnp.concatenate([b_c - m_t] * (L // LANES), axis=1) + d_row
        d_mat = jnp.where(causal, jnp.exp(arg), 0.0)

        qk = lax.dot_general(q, k, (((1,), (1,)), ((), ())), preferred_element_type=F32)
        s_b = (qk * d_mat).astype(BF16)
        sv = jnp.dot(s_b, v_ext, preferred_element_type=F32)
        qc = jnp.dot(q, ct_ext.astype(BF16), preferred_element_type=F32)
        num = sv[:, :dv] + rep(a_t) * qc[:, :dv]
        den = sv[:, dv:dv + LANES] + a_t * qc[:, dv:dv + LANES]
        inv = 1.0 / jnp.maximum(jnp.abs(den), e_mt)
        h = num * rep(inv)
        msq = jnp.dot((h * h).astype(BF16), mean_w, preferred_element_type=F32)
        hn = h * rep(lax.rsqrt(msq + EPS)) * nw
        og = og_ref[pl.ds(r0, L), :].astype(F32)
        out_ref[pl.ds(r0, L), :] = (hn * jax.nn.sigmoid(og)).astype(BF16)

        w_end = b_end + d_row
        m_loc = jnp.max(w_end, axis=-1, keepdims=True)
        e_c = jnp.exp(b_end + d_c - m_loc)
        ke = (k.astype(F32) * e_c).astype(BF16)
        c_loc = lax.dot_general(ke, v_ext, (((0,), (0,)), ((), ())), preferred_element_type=F32)
        m_new = jnp.maximum(b_end + m_prev, m_loc)
        ct_ext = jnp.exp(b_end + m_prev - m_new) * ct_ext + jnp.exp(m_loc - m_new) * c_loc
        m_prev = m_new


def _mlstm_selector(nc):
    ncp = -(-nc // SUBLANES) * SUBLANES
    sel = np.zeros((9 * ncp, 3 * nc * LANES), np.float32)
    for kind in range(3):
        for part in range(3):
            for c in range(nc):
                sel[(kind * 3 + part) * ncp + c, (kind * nc + c) * LANES:(kind * nc + c + 1) * LANES] = 1.0
    return jnp.asarray(sel, BF16)


def _mlstm(qk, v, og, gates, conv_w, conv_b, norm_w, *, batch, seq):
    t, w = qk.shape
    nh = MLSTM_HEADS
    dk = w // (2 * nh)
    dv = v.shape[1] // nh
    assert dk == LANES and dv % LANES == 0 and seq % MLSTM_CHUNK == 0 and MLSTM_CHUNK % LANES == 0
    nc = seq // MLSTM_CHUNK
    sel = _mlstm_selector(nc)
    body = functools.partial(_mlstm_body, seq=seq, dk=dk, dv=dv)
    return pl.pallas_call(
        body,
        grid=(batch, nh),
        in_specs=[
            pl.BlockSpec((seq, dk), lambda b, h: (b, h)),
            pl.BlockSpec((seq, dk), lambda b, h: (b, nh + h)),
            pl.BlockSpec((seq, dv), lambda b, h: (b, h)),
            pl.BlockSpec((seq, dv), lambda b, h: (b, h)),
            pl.BlockSpec((None, 2, seq), lambda b, h: (h, 0, b)),
            pl.BlockSpec((conv_w.shape[0], dk), lambda b, h: (0, h)),
            pl.BlockSpec((conv_w.shape[0], dk), lambda b, h: (0, nh + h)),
            pl.BlockSpec((1, dk), lambda b, h: (0, h)),
            pl.BlockSpec((1, dk), lambda b, h: (0, nh + h)),
            pl.BlockSpec((1, dv), lambda b, h: (0, h)),
            pl.BlockSpec(sel.shape, lambda b, h: (0, 0)),
        ],
        out_specs=pl.BlockSpec((seq, dv), lambda b, h: (b, h)),
        out_shape=jax.ShapeDtypeStruct((t, nh * dv), BF16),
        scratch_shapes=[
            pltpu.VMEM((seq + SUBLANES, dk), F32),
            pltpu.VMEM((seq, dk), BF16),
            pltpu.VMEM((seq, dk), BF16),
            pltpu.VMEM((MLSTM_CHUNK, 3 * nc * LANES), F32),
        ],
        compiler_params=_params(("parallel", "parallel")),
        name="mlstm",
    )(qk, qk, v, og, gates, conv_w, conv_w, conv_b, conv_b, norm_w, sel)


def _s5prep_body(lr_ref, li_ref, ldt_ref, br_ref, bi_ref, lbr_ref, lbi_ref, bbr_ref, bbi_ref):
    lr = lr_ref[...]
    li = li_ref[...]
    dt = jnp.exp(ldt_ref[...])
    mag = jnp.exp(lr * dt)
    ang = li * dt
    lbr = mag * jnp.cos(ang)
    lbi = mag * jnp.sin(ang)
    nr = lbr - 1.0
    den = lr * lr + li * li
    cr = (nr * lr + lbi * li) / den
    ci = (lbi * lr - nr * li) / den
    br = br_ref[...]
    bi = bi_ref[...]
    lbr_ref[...] = lbr
    lbi_ref[...] = lbi
    bbr_ref[...] = cr * br - ci * bi
    bbi_ref[...] = cr * bi + ci * br


def _s5prep(lr, li, ldt, br, bi):
    shp = jax.ShapeDtypeStruct(lr.shape, F32)
    return pl.pallas_call(_s5prep_body, out_shape=[shp, shp, shp, shp], name="s5prep")(lr, li, ldt, br, bi)


def _s5_body(u_ref, bm_ref, cm_ref, lr_ref, li_ref, dsk_ref, wglu_ref, bglu_ref, o_ref,
             usb_ref, bu_ref, y_ref, st_ref, *, ts, ngb):
    half = S5_GROUPS_PER_BLOCK * S5_STATE
    cb = S5_GROUPS_PER_BLOCK * S5_GROUP

    @pl.when(pl.program_id(0) == 0)
    def _():
        st_ref[...] = jnp.zeros_like(st_ref)

    for s in range(ts):
        usb_ref[s * SUBLANES:(s + 1) * SUBLANES, :] = u_ref[:, s, :]

    def project_in(gb):
        ug = usb_ref[:, gb * cb:(gb + 1) * cb]
        bu_ref[gb % 2] = jnp.dot(ug.astype(BF16), bm_ref[gb], preferred_element_type=F32)

    project_in(0)
    for gb in range(ngb):
        if gb + 1 < ngb:
            project_in(gb + 1)
        buf = bu_ref.at[gb % 2]
        lrb = jnp.broadcast_to(lr_ref[gb], (SUBLANES, half))
        lib = jnp.broadcast_to(li_ref[gb], (SUBLANES, half))
        xr = st_ref[gb, :, 0:half]
        xi = st_ref[gb, :, half:2 * half]
        for s in range(ts):
            rs = slice(s * SUBLANES, (s + 1) * SUBLANES)
            nr = lrb * xr - lib * xi + buf[rs, 0:half]
            ni = lrb * xi + lib * xr + buf[rs, half:2 * half]
            buf[rs, 0:half] = nr
            buf[rs, half:2 * half] = ni
            xr, xi = nr, ni
        st_ref[gb, :, 0:half] = xr
        st_ref[gb, :, half:2 * half] = xi

        yg = jnp.dot(buf[...].astype(BF16), cm_ref[gb], preferred_element_type=F32)
        yg = yg + dsk_ref[:, gb * cb:(gb + 1) * cb] * usb_ref[:, gb * cb:(gb + 1) * cb]
        y_ref[:, gb * cb:(gb + 1) * cb] = jax.nn.gelu(yg)

    y = y_ref[...]
    z = jnp.dot(y.astype(BF16), wglu_ref[...], preferred_element_type=F32) + bglu_ref[...]
    usb_ref[...] = y * jax.nn.sigmoid(z)

    for s in range(ts):
        o_ref[:, s, :] = usb_ref[s * SUBLANES:(s + 1) * SUBLANES, :]


def _s5(u, bm, cm, lbr, lbi, dsk, wglu, bglu, *, ts=64):
    batch, seq, width = u.shape
    ngb = bm.shape[0]
    half = S5_GROUPS_PER_BLOCK * S5_STATE
    body = functools.partial(_s5_body, ts=ts, ngb=ngb)
    full = lambda a: pl.BlockSpec(a.shape, lambda i, n=a.ndim: (0,) * n)
    return pl.pallas_call(
        body,
        grid=(seq // ts,),
        in_specs=[pl.BlockSpec((batch, ts, width), lambda i: (0, i, 0)),
                  full(bm), full(cm), full(lbr), full(lbi), full(dsk), full(wglu), full(bglu)],
        out_specs=pl.BlockSpec((batch, ts, width), lambda i: (0, i, 0)),
        out_shape=jax.ShapeDtypeStruct((batch, seq, width), F32),
        scratch_shapes=[
            pltpu.VMEM((ts * batch, width), F32),
            pltpu.VMEM((2, ts * batch, 2 * half), F32),
            pltpu.VMEM((ts * batch, width), F32),
            pltpu.VMEM((ngb, batch, 2 * half), F32),
        ],
        compiler_params=_params(("arbitrary",)),
        name="s5",
    )(u, bm, cm, lbr, lbi, dsk, wglu, bglu)


def _s5_weights(lam_re, lam_im, log_dt, b_re, b_im, c_re, c_im):
    g, p = lam_re.shape
    hch = b_re.shape[-1]
    gpb = S5_GROUPS_PER_BLOCK
    ngb = g // gpb
    rep = lambda a: jnp.repeat(a, hch, axis=0)
    to_rows = lambda a: jnp.transpose(a, (0, 2, 1)).reshape(g * hch, p)
    ldt = jnp.broadcast_to(log_dt[:, None], (g, p))
    lbr, lbi, bbr, bbi = _s5prep(rep(lam_re), rep(lam_im), rep(ldt), to_rows(b_re), to_rows(b_im))
    lbr = lbr[::hch].reshape(ngb, 1, gpb * p)
    lbi = lbi[::hch].reshape(ngb, 1, gpb * p)
    eye = jnp.eye(gpb, dtype=F32)

    def in_blockdiag(a):
        a = a.reshape(ngb, gpb, hch, p)
        return (a[:, :, :, None, :] * eye[None, :, None, :, None]).reshape(ngb, gpb * hch, gpb * p)

    def out_blockdiag(a):
        a = jnp.transpose(a, (0, 2, 1)).reshape(ngb, gpb, p, hch)
        return (a[:, :, :, None, :] * eye[None, :, None, :, None]).reshape(ngb, gpb * p, gpb * hch)

    bm = jnp.concatenate([in_blockdiag(bbr), in_blockdiag(bbi)], axis=2).astype(BF16)
    cm = jnp.concatenate([out_blockdiag(c_re), out_blockdiag(-c_im)], axis=1).astype(BF16)
    return bm, cm, lbr, lbi


def _outproj_body(x_ref, hm_ref, hs_ref, wa_ref, wb_ref, o_ref):
    acc = jnp.dot(hm_ref[...], wa_ref[...], preferred_element_type=F32)
    acc = acc + jnp.dot(hs_ref[...].astype(BF16), wb_ref[...], preferred_element_type=F32)
    o_ref[...] = x_ref[...] + acc


def _outproj(x, hm, hs, wo, *, tm=512):
    t, d = x.shape
    w = hm.shape[1]
    return pl.pallas_call(
        _outproj_body,
        grid=(t // tm,),
        in_specs=[
            pl.BlockSpec((tm, d), lambda i: (i, 0)),
            pl.BlockSpec((tm, w), lambda i: (i, 0)),
            pl.BlockSpec((tm, w), lambda i: (i, 0)),
            pl.BlockSpec((w, d), lambda i: (0, 0)),
            pl.BlockSpec((hs.shape[1], d), lambda i: (w // hs.shape[1], 0)),
        ],
        out_specs=pl.BlockSpec((tm, d), lambda i: (i, 0)),
        out_shape=jax.ShapeDtypeStruct((t, d), F32),
        compiler_params=_params(("parallel",)),
        name="outproj",
    )(x, hm, hs, wo, wo)


def kernel(x, ffn1_norm, ffn1_w1, ffn1_w3, ffn1_w2, mix_norm, w_in, conv_w, conv_b, b_i, b_f, mlstm_norm, lam_re, lam_im, log_dt, b_re, b_im, c_re, c_im, d_skip, w_glu, b_glu, w_out, ffn2_norm, ffn2_w1, ffn2_w3, ffn2_w2, final_norm):
    batch, seq, d = x.shape
    depth = ffn1_norm.shape[0]
    nh = MLSTM_HEADS
    qk_cols = conv_w.shape[-1]
    mw = mlstm_norm.shape[-1]
    gate0 = qk_cols + 2 * mw
    sw = w_glu.shape[-1]
    row = lambda a: a.reshape(1, -1).astype(F32)
    xt = x.reshape(batch * seq, d)
    gfin = row(final_norm)
    for l in range(depth):
        xt, (w1b, w3b, w2b, w_main, wg_cols, wo, wglu) = _ffn(
            xt, row(ffn1_norm[l]), ffn1_w1[l].astype(BF16), ffn1_w3[l].astype(BF16), ffn1_w2[l].astype(BF16), gfin,
            final_norm=False, layer=l, cast_tiles=(ffn2_w1, ffn2_w3, ffn2_w2), cast_rows=(w_in, w_out, w_glu),
            gate0=gate0, gate_w=2 * nh)

        wg_t = wg_cols[:, :2 * nh].T
        bg = jnp.concatenate([b_i[l], b_f[l]]).reshape(2 * nh, 1).astype(F32)
        qk, v, og, u, gates = _inproj(xt, row(mix_norm[l]), w_main, wg_t, bg)

        hm = _mlstm(qk, v, og, gates, conv_w[l].astype(F32), row(conv_b[l]), row(mlstm_norm[l]),
                    batch=batch, seq=seq)

        bm, cm, lbr, lbi = _s5_weights(lam_re[l], lam_im[l], log_dt[l], b_re[l], b_im[l], c_re[l], c_im[l])
        hs = _s5(u.reshape(batch, seq, sw), bm, cm, lbr, lbi, row(d_skip[l]), wglu, row(b_glu[l]))

        xt = _outproj(xt, hm, hs.reshape(batch * seq, sw), wo)

        xt, _ = _ffn(xt, row(ffn2_norm[l]), w1b, w3b, w2b, gfin, final_norm=(l == depth - 1))
    return xt.reshape(batch, seq, d)
```

```python
import functools

import jax
import jax.numpy as jnp
import numpy as np
from jax import lax
from jax.experimental import pallas as pl
from jax.experimental.pallas import tpu as pltpu

F32 = jnp.float32
BF16 = jnp.bfloat16

EPS = 1e-6
MLSTM_HEADS = 4
S5_GROUP = 16
S5_STATE = 64
S5_GROUPS_PER_BLOCK = 8
S5_LIFT = 2
LANES = 128
SUBLANES = 8
MLSTM_CHUNK = 256
VMEM_LIMIT_BYTES = 58 * 1024 * 1024


def _rms(x, g):
    return x * lax.rsqrt(jnp.mean(x * x, axis=-1, keepdims=True) + EPS) * g


def _params(sem):
    return pltpu.CompilerParams(dimension_semantics=sem, vmem_limit_bytes=VMEM_LIMIT_BYTES)


def _row_cast_split(n_rows, n_i, n_ff):
    kj = 1
    while kj * 2 <= min(n_ff, 8) and (n_rows // (n_i * kj * 2)) % 16 == 0 and n_rows % (n_i * kj * 2) == 0:
        kj *= 2
    assert n_rows % (n_i * kj) == 0 and (n_rows // (n_i * kj)) % 16 == 0
    return kj


def _ffn_body(*refs, n_ff, final_norm, n_tile, row_kj, gate0, gate_w):
    n_row = len(row_kj)
    x_ref, g_ref, w1_ref, w3_ref, w2_ref, gf_ref = refs[:6]
    tile_src = refs[6:6 + n_tile]
    row_src = refs[6 + n_tile:6 + n_tile + n_row]
    o_ref = refs[6 + n_tile + n_row]
    tile_dst = refs[7 + n_tile + n_row:7 + 2 * n_tile + n_row]
    row_dst = refs[7 + 2 * n_tile + n_row:-1]
    xn_ref = refs[-1]
    j = pl.program_id(1)

    @pl.when(j == 0)
    def _():
        x = x_ref[...]
        xn_ref[...] = _rms(x, g_ref[...]).astype(BF16)
        o_ref[...] = x

    xn = xn_ref[...]
    a = jnp.dot(xn, w1_ref[...], preferred_element_type=F32)
    b = jnp.dot(xn, w3_ref[...], preferred_element_type=F32)
    h = (a * jax.nn.sigmoid(a)) * (b * 0.5)
    o_ref[...] += jnp.dot(h.astype(BF16), w2_ref[...], preferred_element_type=F32)

    if final_norm:
        @pl.when(j == n_ff - 1)
        def _():
            o_ref[...] = _rms(o_ref[...], gf_ref[...])

    for src, dst in zip(tile_src, tile_dst):
        dst[...] = src[...].astype(BF16)
    for r, (src, kj) in enumerate(zip(row_src, row_kj)):
        @pl.when(j < kj)
        def _(src=src, r=r):
            if r == 0 and gate0 is not None:
                main_ref, gate_ref = row_dst[0], row_dst[1]
                main_ref[:, :gate0] = src[:, :gate0].astype(BF16)
                main_ref[:, gate0:] = src[:, gate0 + gate_w:].astype(BF16)
                gate_ref[...] = src[:, gate0:gate0 + LANES].astype(BF16)
            else:
                row_dst[r + (1 if gate0 is not None else 0)][...] = src[...].astype(BF16)


def _ffn(x, g, w1, w3, w2, gf, *, final_norm, tm=1024, tf=512, layer=0, cast_tiles=(), cast_rows=(), gate0=None,
         gate_w=0):
    t, d = x.shape
    dff = w1.shape[1]
    n_i, n_ff = t // tm, dff // tf
    rb = d // n_i
    in_specs = [
        pl.BlockSpec((tm, d), lambda i, j: (i, 0)),
        pl.BlockSpec((1, d), lambda i, j: (0, 0)),
        pl.BlockSpec((d, tf), lambda i, j: (0, j)),
        pl.BlockSpec((d, tf), lambda i, j: (0, j)),
        pl.BlockSpec((tf, d), lambda i, j: (j, 0)),
        pl.BlockSpec((1, d), lambda i, j: (0, 0)),
    ]
    out_specs = [pl.BlockSpec((tm, d), lambda i, j: (i, 0))]
    out_shape = [jax.ShapeDtypeStruct((t, d), F32)]
    for w in cast_tiles:
        if w.shape[1:] == (d, dff):
            in_specs.append(pl.BlockSpec((None, rb, tf), lambda i, j: (layer, i, j)))
            out_specs.append(pl.BlockSpec((rb, tf), lambda i, j: (i, j)))
        else:
            assert w.shape[1:] == (dff, d) and rb % LANES == 0
            in_specs.append(pl.BlockSpec((None, tf, rb), lambda i, j: (layer, j, i)))
            out_specs.append(pl.BlockSpec((tf, rb), lambda i, j: (j, i)))
        out_shape.append(jax.ShapeDtypeStruct(w.shape[1:], BF16))
    row_kj = []
    for r, w in enumerate(cast_rows):
        n_rows, n_cols = w.shape[1:]
        kj = _row_cast_split(n_rows, n_i, n_ff)
        row_kj.append(kj)
        rr = n_rows // (n_i * kj)
        idx = lambda i, j, kj=kj: (i * kj + jnp.minimum(j, kj - 1), 0)
        in_specs.append(pl.BlockSpec((None, rr, n_cols), lambda i, j, idx=idx: (layer,) + idx(i, j)))
        if r == 0 and gate0 is not None:
            out_specs += [pl.BlockSpec((rr, n_cols - gate_w), idx), pl.BlockSpec((rr, LANES), idx)]
            out_shape += [jax.ShapeDtypeStruct((n_rows, n_cols - gate_w), BF16),
                          jax.ShapeDtypeStruct((n_rows, LANES), BF16)]
        else:
            out_specs.append(pl.BlockSpec((rr, n_cols), idx))
            out_shape.append(jax.ShapeDtypeStruct((n_rows, n_cols), BF16))
    body = functools.partial(_ffn_body, n_ff=n_ff, final_norm=final_norm, n_tile=len(cast_tiles),
                             row_kj=tuple(row_kj), gate0=gate0, gate_w=gate_w)
    outs = pl.pallas_call(
        body,
        grid=(n_i, n_ff),
        in_specs=in_specs,
        out_specs=out_specs,
        out_shape=out_shape,
        scratch_shapes=[pltpu.VMEM((tm, d), BF16)],
        compiler_params=_params(("parallel", "arbitrary")),
        name="ffn_final" if final_norm else "ffn",
    )(x, g, w1, w3, w2, gf, *cast_tiles, *cast_rows)
    return outs[0], outs[1:]


def _inproj_body(x_ref, g_ref, w_ref, wg_ref, bg_ref, qk_ref, v_ref, o_ref, u_ref, gates_ref, hn_ref):
    j = pl.program_id(1)
    nh = MLSTM_HEADS

    @pl.when(j == 0)
    def _():
        hn = _rms(x_ref[...], g_ref[...]).astype(BF16)
        hn_ref[...] = hn
        gt = lax.dot_general(wg_ref[...], hn, (((1,), (1,)), ((), ())), preferred_element_type=F32)
        gt = gt + bg_ref[...]
        for h in range(nh):
            gates_ref[h] = jnp.concatenate([gt[h:h + 1, :], gt[nh + h:nh + h + 1, :]], axis=0)

    res = jnp.dot(hn_ref[...], w_ref[...], preferred_element_type=F32)

    @pl.when(j == 0)
    def _():
        qk_ref[...] = res.astype(BF16)

    @pl.when(j == 1)
    def _():
        v_ref[...] = res.astype(BF16)

    @pl.when(j == 2)
    def _():
        o_ref[...] = res.astype(BF16)

    @pl.when(j == 3)
    def _():
        u_ref[...] = res


def _inproj(x, g, w_main, wg_t, bg, *, tm=1024):
    t, d = x.shape
    w = w_main.shape[1] // 4
    blk = lambda: pl.BlockSpec((tm, w), lambda i, j: (i, 0))
    return pl.pallas_call(
        _inproj_body,
        grid=(t // tm, 4),
        in_specs=[
            pl.BlockSpec((tm, d), lambda i, j: (i, 0)),
            pl.BlockSpec((1, d), lambda i, j: (0, 0)),
            pl.BlockSpec((d, w), lambda i, j: (0, j)),
            pl.BlockSpec((2 * MLSTM_HEADS, d), lambda i, j: (0, 0)),
            pl.BlockSpec((2 * MLSTM_HEADS, 1), lambda i, j: (0, 0)),
        ],
        out_specs=[blk(), blk(), blk(), blk(),
                   pl.BlockSpec((MLSTM_HEADS, 2, tm), lambda i, j: (0, 0, i))],
        out_shape=[
            jax.ShapeDtypeStruct((t, w), BF16),
            jax.ShapeDtypeStruct((t, w), BF16),
            jax.ShapeDtypeStruct((t, w), BF16),
            jax.ShapeDtypeStruct((t, w), F32),
            jax.ShapeDtypeStruct((MLSTM_HEADS, 2, t), F32),
        ],
        scratch_shapes=[pltpu.VMEM((tm, d), BF16)],
        compiler_params=_params(("parallel", "arbitrary")),
        name="inproj",
    )(x, g, w_main, wg_t, bg)


def _split3(x):
    hi = x.astype(BF16).astype(F32)
    r1 = x - hi
    mid = r1.astype(BF16).astype(F32)
    lo = (r1 - mid).astype(BF16).astype(F32)
    return [hi, mid, lo]


def _log_sigmoid(x):
    return jnp.minimum(x, 0.0) - jnp.log(1.0 + jnp.exp(-jnp.abs(x)))


def _mlstm_body(q_ref, k_ref, v_ref, og_ref, g_ref, cwq_ref, cwk_ref, cbq_ref, cbk_ref, nw_ref,
                sel_ref, out_ref, xp_ref, qs_ref, ks_ref, cols_ref, *, seq, dk, dv):
    L = MLSTM_CHUNK
    nc = seq // L

    def conv_silu(x_ref, w_ref, b_ref, dst_ref, scale):
        xp_ref[0:SUBLANES, :] = jnp.zeros((SUBLANES, dk), F32)
        xp_ref[SUBLANES:, :] = x_ref[...].astype(F32)
        w = w_ref[...]
        acc = b_ref[...] + xp_ref[pl.ds(SUBLANES, seq), :] * w[3:4, :]
        for d in (1, 2, 3):
            acc = acc + xp_ref[pl.ds(SUBLANES - d, seq), :] * w[3 - d:4 - d, :]
        y = acc * jax.nn.sigmoid(acc)
        if scale != 1.0:
            y = y * scale
        dst_ref[...] = y.astype(BF16)

    conv_silu(q_ref, cwq_ref, cbq_ref, qs_ref, dk ** -0.5)
    conv_silu(k_ref, cwk_ref, cbk_ref, ks_ref, 1.0)

    rows = lax.broadcasted_iota(jnp.int32, (L, L), 0)
    cols = lax.broadcasted_iota(jnp.int32, (L, L), 1)
    causal = cols <= rows
    tri = jnp.where(rows <= cols, 1.0, 0.0).astype(BF16)

    ncp = -(-nc // SUBLANES) * SUBLANES
    pad = [jnp.zeros((ncp - nc, L), F32)] if ncp > nc else []
    g = g_ref[...]
    li_all = jnp.concatenate([g[0:1, c * L:(c + 1) * L] for c in range(nc)] + pad, axis=0)
    lf_all = jnp.concatenate([_log_sigmoid(g[1:2, c * L:(c + 1) * L]) for c in range(nc)] + pad, axis=0)
    parts = jnp.concatenate(_split3(lf_all), axis=0).astype(BF16)
    cs = jnp.dot(parts, tri, preferred_element_type=F32)
    b_all = cs[0:ncp] + cs[ncp:2 * ncp] + cs[2 * ncp:3 * ncp]
    d_all = li_all - b_all
    lane = lax.broadcasted_iota(jnp.int32, (ncp, L), 1)
    cm_all = d_all
    shift = 1
    while shift < L:
        cm_all = jnp.maximum(cm_all, jnp.where(lane >= shift, pltpu.roll(cm_all, shift, 1), -jnp.inf))
        shift *= 2
    at = jnp.concatenate(_split3(b_all) + _split3(d_all) + _split3(cm_all), axis=0).astype(BF16)
    cols_ref[...] = lax.dot_general(at, sel_ref[...], (((0,), (0,)), ((), ())), preferred_element_type=F32)

    ext = 2 * LANES
    rep = lambda a: jnp.concatenate([a] * (dv // LANES), axis=1)
    ones_l = jnp.ones((L, ext), BF16)
    mean_w = jnp.full((dv, LANES), 1.0 / dv, BF16)
    ct_ext = jnp.zeros((dk, dv + ext), F32)
    m_prev = jnp.zeros((1, 1), F32)
    nw = nw_ref[...]

    for c in range(nc):
        r0 = c * L
        q = qs_ref[pl.ds(r0, L), :]
        k = ks_ref[pl.ds(r0, L), :]
        v_ext = jnp.concatenate([v_ref[pl.ds(r0, L), :], ones_l], axis=1)
        d_row = d_all[c:c + 1, :]
        b_end = b_all[c:c + 1, L - 1:L]
        b_c = cols_ref[:, c * LANES:(c + 1) * LANES]
        d_c = cols_ref[:, (nc + c) * LANES:(nc + c + 1) * LANES]
        cm_c = cols_ref[:, (2 * nc + c) * LANES:(2 * nc + c + 1) * LANES]

        m_t = b_c + jnp.maximum(m_prev, cm_c)
        a_t = jnp.exp(b_c + m_prev - m_t)
        e_mt = jnp.exp(-m_t)
        arg = jnp.concatenate([b_c - m_t] * (L // LANES), axis=1) + d_row
        d_mat = jnp.where(causal, jnp.exp(arg), 0.0)

        qk = lax.dot_general(q, k, (((1,), (1,)), ((), ())), preferred_element_type=F32)
        s_b = (qk * d_mat).astype(BF16)
        sv = jnp.dot(s_b, v_ext, preferred_element_type=F32)
        qc = jnp.dot(q, ct_ext.astype(BF16), preferred_element_type=F32)
        num = sv[:, :dv] + rep(a_t) * qc[:, :dv]
        den = sv[:, dv:dv + LANES] + a_t * qc[:, dv:dv + LANES]
        inv = 1.0 / jnp.maximum(jnp.abs(den), e_mt)
        h = num * rep(inv)
        msq = jnp.dot((h * h).astype(BF16), mean_w, preferred_element_type=F32)
        hn = h * rep(lax.rsqrt(msq + EPS)) * nw
        og = og_ref[pl.ds(r0, L), :].astype(F32)
        out_ref[pl.ds(r0, L), :] = (hn * jax.nn.sigmoid(og)).astype(BF16)

        w_end = b_end + d_row
        m_loc = jnp.max(w_end, axis=-1, keepdims=True)
        e_c = jnp.exp(b_end + d_c - m_loc)
        ke = (k.astype(F32) * e_c).astype(BF16)
        c_loc = lax.dot_general(ke, v_ext, (((0,), (0,)), ((), ())), preferred_element_type=F32)
        m_new = jnp.maximum(b_end + m_prev, m_loc)
        ct_ext = jnp.exp(b_end + m_prev - m_new) * ct_ext + jnp.exp(m_loc - m_new) * c_loc
        m_prev = m_new


def _mlstm_selector(nc):
    ncp = -(-nc // SUBLANES) * SUBLANES
    sel = np.zeros((9 * ncp, 3 * nc * LANES), np.float32)
    for kind in range(3):
        for part in range(3):
            for c in range(nc):
                sel[(kind * 3 + part) * ncp + c, (kind * nc + c) * LANES:(kind * nc + c + 1) * LANES] = 1.0
    return jnp.asarray(sel, BF16)


def _mlstm(qk, v, og, gates, conv_w, conv_b, norm_w, *, batch, seq):
    t, w = qk.shape
    nh = MLSTM_HEADS
    dk = w // (2 * nh)
    dv = v.shape[1] // nh
    assert dk == LANES and dv % LANES == 0 and seq % MLSTM_CHUNK == 0 and MLSTM_CHUNK % LANES == 0
    nc = seq // MLSTM_CHUNK
    sel = _mlstm_selector(nc)
    body = functools.partial(_mlstm_body, seq=seq, dk=dk, dv=dv)
    return pl.pallas_call(
        body,
        grid=(batch, nh),
        in_specs=[
            pl.BlockSpec((seq, dk), lambda b, h: (b, h)),
            pl.BlockSpec((seq, dk), lambda b, h: (b, nh + h)),
            pl.BlockSpec((seq, dv), lambda b, h: (b, h)),
            pl.BlockSpec((seq, dv), lambda b, h: (b, h)),
            pl.BlockSpec((None, 2, seq), lambda b, h: (h, 0, b)),
            pl.BlockSpec((conv_w.shape[0], dk), lambda b, h: (0, h)),
            pl.BlockSpec((conv_w.shape[0], dk), lambda b, h: (0, nh + h)),
            pl.BlockSpec((1, dk), lambda b, h: (0, h)),
            pl.BlockSpec((1, dk), lambda b, h: (0, nh + h)),
            pl.BlockSpec((1, dv), lambda b, h: (0, h)),
            pl.BlockSpec(sel.shape, lambda b, h: (0, 0)),
        ],
        out_specs=pl.BlockSpec((seq, dv), lambda b, h: (b, h)),
        out_shape=jax.ShapeDtypeStruct((t, nh * dv), BF16),
        scratch_shapes=[
            pltpu.VMEM((seq + SUBLANES, dk), F32),
            pltpu.VMEM((seq, dk), BF16),
            pltpu.VMEM((seq, dk), BF16),
            pltpu.VMEM((MLSTM_CHUNK, 3 * nc * LANES), F32),
        ],
        compiler_params=_params(("parallel", "parallel")),
        name="mlstm",
    )(qk, qk, v, og, gates, conv_w, conv_w, conv_b, conv_b, norm_w, sel)


def _cmul(ar, ai, br, bi):
    return ar * br - ai * bi, ar * bi + ai * br


def _s5prep_body(lr_ref, li_ref, ldt_ref, br_ref, bi_ref, cr_ref, ci_ref,
                 lpr_ref, lpi_ref, pbr_ref, pbi_ref, qcr_ref, qci_ref):
    lr = lr_ref[...]
    li = li_ref[...]
    dt = jnp.exp(ldt_ref[...])
    mag = jnp.exp(lr * dt)
    ang = li * dt
    lbr = mag * jnp.cos(ang)
    lbi = mag * jnp.sin(ang)
    nr = lbr - 1.0
    den = lr * lr + li * li
    cr = (nr * lr + lbi * li) / den
    ci = (lbi * lr - nr * li) / den
    pr, pi_ = _cmul(cr, ci, br_ref[...], bi_ref[...])
    qr, qi = cr_ref[...], ci_ref[...]
    for m in range(S5_LIFT):
        pbr_ref[m] = pr
        pbi_ref[m] = pi_
        pr, pi_ = _cmul(pr, pi_, lbr, lbi)
        qr, qi = _cmul(qr, qi, lbr, lbi)
        qcr_ref[m] = qr
        qci_ref[m] = qi
    wr, wi = lbr, lbi
    for _ in range(S5_LIFT - 1):
        wr, wi = _cmul(wr, wi, lbr, lbi)
    lpr_ref[...] = wr
    lpi_ref[...] = wi


def _s5prep(lr, li, ldt, br, bi, cr, ci):
    one = jax.ShapeDtypeStruct(lr.shape, F32)
    stk = jax.ShapeDtypeStruct((S5_LIFT,) + lr.shape, F32)
    return pl.pallas_call(_s5prep_body, out_shape=[one, one, stk, stk, stk, stk], name="s5prep")(
        lr, li, ldt, br, bi, cr, ci)


def _s5dmat_body(bm_ref, c0_ref, d_ref):
    for lag in range(S5_LIFT):
        d_ref[lag] = jnp.dot(bm_ref[S5_LIFT - 1 - lag], c0_ref[...], preferred_element_type=F32)


def _s5dmat(bm4, cm0):
    ngb, r, cb, st = bm4.shape
    return pl.pallas_call(
        _s5dmat_body,
        grid=(ngb,),
        in_specs=[pl.BlockSpec((None, r, cb, st), lambda g: (g, 0, 0, 0)),
                  pl.BlockSpec((None, st, cb), lambda g: (g, 0, 0))],
        out_specs=pl.BlockSpec((None, r, cb, cb), lambda g: (g, 0, 0, 0)),
        out_shape=jax.ShapeDtypeStruct((ngb, r, cb, cb), F32),
        name="s5dmat",
    )(bm4, cm0)


def _s5_body(u_ref, bm_ref, cm_ref, dm_ref, lr_ref, li_ref, dsk_ref, wglu_ref, bglu_ref, o_ref,
             usb_ref, sb_ref, y_ref, st_ref, *, ts, ngb):
    R = S5_LIFT
    nk = ts // R
    rows = nk * SUBLANES
    half = S5_GROUPS_PER_BLOCK * S5_STATE
    cb = S5_GROUPS_PER_BLOCK * S5_GROUP

    @pl.when(pl.program_id(0) == 0)
    def _():
        st_ref[...] = jnp.zeros_like(st_ref)

    def srow(s):
        return (s % R) * rows + (s // R) * SUBLANES

    for s in range(ts):
        usb_ref[srow(s):srow(s) + SUBLANES, :] = u_ref[:, s, :]

    def lhs(gb):
        return jnp.concatenate([usb_ref[r * rows:(r + 1) * rows, gb * cb:(gb + 1) * cb] for r in range(R)],
                               axis=1).astype(BF16)

    def project_in(gb):
        sb_ref[gb % 2, SUBLANES:, :] = jnp.dot(lhs(gb), bm_ref[gb], preferred_element_type=F32)

    project_in(0)
    for gb in range(ngb):
        if gb + 1 < ngb:
            project_in(gb + 1)
        buf = sb_ref.at[gb % 2]
        lrb = jnp.broadcast_to(lr_ref[gb], (SUBLANES, half))
        lib = jnp.broadcast_to(li_ref[gb], (SUBLANES, half))
        xr = st_ref[gb, :, 0:half]
        xi = st_ref[gb, :, half:2 * half]
        buf[0:SUBLANES, 0:half] = xr
        buf[0:SUBLANES, half:2 * half] = xi
        for k in range(nk):
            rs = slice((k + 1) * SUBLANES, (k + 2) * SUBLANES)
            nr = lrb * xr - lib * xi + buf[rs, 0:half]
            ni = lrb * xi + lib * xr + buf[rs, half:2 * half]
            buf[rs, 0:half] = nr
            buf[rs, half:2 * half] = ni
            xr, xi = nr, ni
        st_ref[gb, :, 0:half] = xr
        st_ref[gb, :, half:2 * half] = xi

        yo = jnp.dot(buf[0:rows, :].astype(BF16), cm_ref[gb], preferred_element_type=F32)
        yd = jnp.dot(lhs(gb), dm_ref[gb], preferred_element_type=F32)
        for r in range(R):
            ug = usb_ref[r * rows:(r + 1) * rows, gb * cb:(gb + 1) * cb]
            yg = yo[:, r * cb:(r + 1) * cb] + yd[:, r * cb:(r + 1) * cb] + dsk_ref[:, gb * cb:(gb + 1) * cb] * ug
            y_ref[r * rows:(r + 1) * rows, gb * cb:(gb + 1) * cb] = jax.nn.gelu(yg)

    y = y_ref[...]
    z = jnp.dot(y.astype(BF16), wglu_ref[...], preferred_element_type=F32) + bglu_ref[...]
    usb_ref[...] = y * jax.nn.sigmoid(z)

    for s in range(ts):
        o_ref[:, s, :] = usb_ref[srow(s):srow(s) + SUBLANES, :]


def _s5(u, bm, cm, dm, lpr, lpi, dsk, wglu, bglu, *, ts=64):
    batch, seq, width = u.shape
    ngb = bm.shape[0]
    half = S5_GROUPS_PER_BLOCK * S5_STATE
    assert batch == SUBLANES and ts % S5_LIFT == 0 and seq % ts == 0
    body = functools.partial(_s5_body, ts=ts, ngb=ngb)
    full = lambda a: pl.BlockSpec(a.shape, lambda i, n=a.ndim: (0,) * n)
    return pl.pallas_call(
        body,
        grid=(seq // ts,),
        in_specs=[pl.BlockSpec((batch, ts, width), lambda i: (0, i, 0)),
                  full(bm), full(cm), full(dm), full(lpr), full(lpi), full(dsk), full(wglu), full(bglu)],
        out_specs=pl.BlockSpec((batch, ts, width), lambda i: (0, i, 0)),
        out_shape=jax.ShapeDtypeStruct((batch, seq, width), F32),
        scratch_shapes=[
            pltpu.VMEM((ts * batch, width), F32),
            pltpu.VMEM((2, (ts // S5_LIFT + 1) * batch, 2 * half), F32),
            pltpu.VMEM((ts * batch, width), F32),
            pltpu.VMEM((ngb, batch, 2 * half), F32),
        ],
        compiler_params=_params(("arbitrary",)),
        name="s5",
    )(u, bm, cm, dm, lpr, lpi, dsk, wglu, bglu)


def _s5_weights(lam_re, lam_im, log_dt, b_re, b_im, c_re, c_im):
    g, p = lam_re.shape
    hch = b_re.shape[-1]
    gpb = S5_GROUPS_PER_BLOCK
    ngb = g // gpb
    R = S5_LIFT
    rep = lambda a: jnp.repeat(a, hch, axis=0)
    to_rows = lambda a: jnp.transpose(a, (0, 2, 1)).reshape(g * hch, p)
    ldt = jnp.broadcast_to(log_dt[:, None], (g, p))
    lpr, lpi, pbr, pbi, qcr, qci = _s5prep(rep(lam_re), rep(lam_im), rep(ldt), to_rows(b_re), to_rows(b_im),
                                           c_re.reshape(g * hch, p), c_im.reshape(g * hch, p))
    lpr = lpr[::hch].reshape(ngb, 1, gpb * p)
    lpi = lpi[::hch].reshape(ngb, 1, gpb * p)
    eye = jnp.eye(gpb, dtype=F32)

    def in_blockdiag(a):
        a = a.reshape(R, ngb, gpb, hch, p)
        a = (a[:, :, :, :, None, :] * eye[None, None, :, None, :, None]).reshape(R, ngb, gpb * hch, gpb * p)
        return jnp.transpose(a, (1, 0, 2, 3))

    def out_blockdiag(a):
        a = jnp.transpose(a.reshape(R, ngb, gpb, hch, p), (0, 1, 2, 4, 3))
        a = (a[:, :, :, :, None, :] * eye[None, None, :, None, :, None]).reshape(R, ngb, gpb * p, gpb * hch)
        return jnp.transpose(a, (1, 2, 0, 3))

    bm4 = jnp.concatenate([in_blockdiag(pbr[::-1]), in_blockdiag(pbi[::-1])], axis=3).astype(BF16)
    cm = jnp.concatenate([out_blockdiag(qcr), out_blockdiag(-qci)], axis=1)
    cm = cm.reshape(ngb, 2 * gpb * p, R * gpb * hch).astype(BF16)
    c0 = jnp.concatenate([out_blockdiag(jnp.broadcast_to(c_re.reshape(1, g * hch, p), (R, g * hch, p)))[:, :, 0],
                          out_blockdiag(jnp.broadcast_to(-c_im.reshape(1, g * hch, p), (R, g * hch, p)))[:, :, 0]],
                         axis=1).astype(BF16)
    dl = _s5dmat(bm4, c0)
    zero = jnp.zeros_like(dl[:, 0])
    dm = jnp.concatenate([jnp.concatenate([dl[:, r - rp] if r >= rp else zero for r in range(R)], axis=2)
                          for rp in range(R)], axis=1).astype(BF16)
    return bm4.reshape(ngb, R * gpb * hch, 2 * gpb * p), cm, dm, lpr, lpi


def _outproj_body(x_ref, hm_ref, hs_ref, wa_ref, wb_ref, o_ref):
    acc = jnp.dot(hm_ref[...], wa_ref[...], preferred_element_type=F32)
    acc = acc + jnp.dot(hs_ref[...].astype(BF16), wb_ref[...], preferred_element_type=F32)
    o_ref[...] = x_ref[...] + acc


def _outproj(x, hm, hs, wo, *, tm=512):
    t, d = x.shape
    w = hm.shape[1]
    return pl.pallas_call(
        _outproj_body,
        grid=(t // tm,),
        in_specs=[
            pl.BlockSpec((tm, d), lambda i: (i, 0)),
            pl.BlockSpec((tm, w), lambda i: (i, 0)),
            pl.BlockSpec((tm, w), lambda i: (i, 0)),
            pl.BlockSpec((w, d), lambda i: (0, 0)),
            pl.BlockSpec((hs.shape[1], d), lambda i: (w // hs.shape[1], 0)),
        ],
        out_specs=pl.BlockSpec((tm, d), lambda i: (i, 0)),
        out_shape=jax.ShapeDtypeStruct((t, d), F32),
        compiler_params=_params(("parallel",)),
        name="outproj",
    )(x, hm, hs, wo, wo)


def kernel(x, ffn1_norm, ffn1_w1, ffn1_w3, ffn1_w2, mix_norm, w_in, conv_w, conv_b, b_i, b_f, mlstm_norm, lam_re, lam_im, log_dt, b_re, b_im, c_re, c_im, d_skip, w_glu, b_glu, w_out, ffn2_norm, ffn2_w1, ffn2_w3, ffn2_w2, final_norm):
    batch, seq, d = x.shape
    depth = ffn1_norm.shape[0]
    nh = MLSTM_HEADS
    qk_cols = conv_w.shape[-1]
    mw = mlstm_norm.shape[-1]
    gate0 = qk_cols + 2 * mw
    sw = w_glu.shape[-1]
    row = lambda a: a.reshape(1, -1).astype(F32)
    xt = x.reshape(batch * seq, d)
    gfin = row(final_norm)
    for l in range(depth):
        xt, (w1b, w3b, w2b, w_main, wg_cols, wo, wglu) = _ffn(
            xt, row(ffn1_norm[l]), ffn1_w1[l].astype(BF16), ffn1_w3[l].astype(BF16), ffn1_w2[l].astype(BF16), gfin,
            final_norm=False, layer=l, cast_tiles=(ffn2_w1, ffn2_w3, ffn2_w2), cast_rows=(w_in, w_out, w_glu),
            gate0=gate0, gate_w=2 * nh)

        wg_t = wg_cols[:, :2 * nh].T
        bg = jnp.concatenate([b_i[l], b_f[l]]).reshape(2 * nh, 1).astype(F32)
        qk, v, og, u, gates = _inproj(xt, row(mix_norm[l]), w_main, wg_t, bg)

        hm = _mlstm(qk, v, og, gates, conv_w[l].astype(F32), row(conv_b[l]), row(mlstm_norm[l]),
                    batch=batch, seq=seq)

        bm, cm, dm, lpr, lpi = _s5_weights(lam_re[l], lam_im[l], log_dt[l], b_re[l], b_im[l], c_re[l], c_im[l])
        hs = _s5(u.reshape(batch, seq, sw), bm, cm, dm, lpr, lpi, row(d_skip[l]), wglu, row(b_glu[l]))

        xt = _outproj(xt, hm, hs.reshape(batch * seq, sw), wo)

        xt, _ = _ffn(xt, row(ffn2_norm[l]), w1b, w3b, w2b, gfin, final_norm=(l == depth - 1))
    return xt.reshape(batch, seq, d)
```

```python
import functools

import jax
import jax.numpy as jnp
import numpy as np
from jax import lax
from jax.experimental import pallas as pl
from jax.experimental.pallas import tpu as pltpu

F32 = jnp.float32
BF16 = jnp.bfloat16

EPS = 1e-6
MLSTM_HEADS = 4
S5_GROUP = 16
S5_STATE = 64
S5_GROUPS_PER_BLOCK = 8
S5_LIFT = 2
LANES = 128
SUBLANES = 8
MLSTM_CHUNK = 256
VMEM_LIMIT_BYTES = 58 * 1024 * 1024


def _rms(x, g):
    return x * lax.rsqrt(jnp.mean(x * x, axis=-1, keepdims=True) + EPS) * g


def _params(sem):
    return pltpu.CompilerParams(dimension_semantics=sem, vmem_limit_bytes=VMEM_LIMIT_BYTES)


def _row_cast_split(n_rows, n_i, n_ff):
    kj = 1
    while kj * 2 <= min(n_ff, 8) and (n_rows // (n_i * kj * 2)) % 16 == 0 and n_rows % (n_i * kj * 2) == 0:
        kj *= 2
    assert n_rows % (n_i * kj) == 0 and (n_rows // (n_i * kj)) % 16 == 0
    return kj


def _ffn_body(*refs, n_ff, final_norm, n_tile, row_kj, gate0, gate_w):
    n_row = len(row_kj)
    x_ref, g_ref, w1_ref, w3_ref, w2_ref, gf_ref = refs[:6]
    tile_src = refs[6:6 + n_tile]
    row_src = refs[6 + n_tile:6 + n_tile + n_row]
    o_ref = refs[6 + n_tile + n_row]
    tile_dst = refs[7 + n_tile + n_row:7 + 2 * n_tile + n_row]
    row_dst = refs[7 + 2 * n_tile + n_row:-1]
    xn_ref = refs[-1]
    j = pl.program_id(1)

    @pl.when(j == 0)
    def _():
        x = x_ref[...]
        xn_ref[...] = _rms(x, g_ref[...]).astype(BF16)
        o_ref[...] = x

    xn = xn_ref[...]
    a = jnp.dot(xn, w1_ref[...], preferred_element_type=F32)
    b = jnp.dot(xn, w3_ref[...], preferred_element_type=F32)
    h = (a * jax.nn.sigmoid(a)) * (b * 0.5)
    o_ref[...] += jnp.dot(h.astype(BF16), w2_ref[...], preferred_element_type=F32)

    if final_norm:
        @pl.when(j == n_ff - 1)
        def _():
            o_ref[...] = _rms(o_ref[...], gf_ref[...])

    for src, dst in zip(tile_src, tile_dst):
        dst[...] = src[...].astype(BF16)
    for r, (src, kj) in enumerate(zip(row_src, row_kj)):
        @pl.when(j < kj)
        def _(src=src, r=r):
            if r == 0 and gate0 is not None:
                main_ref, gate_ref = row_dst[0], row_dst[1]
                main_ref[:, :gate0] = src[:, :gate0].astype(BF16)
                main_ref[:, gate0:] = src[:, gate0 + gate_w:].astype(BF16)
                gate_ref[...] = src[:, gate0:gate0 + LANES].astype(BF16)
            else:
                row_dst[r + (1 if gate0 is not None else 0)][...] = src[...].astype(BF16)


def _ffn(x, g, w1, w3, w2, gf, *, final_norm, tm=1024, tf=512, layer=0, cast_tiles=(), cast_rows=(), gate0=None,
         gate_w=0):
    t, d = x.shape
    dff = w1.shape[1]
    n_i, n_ff = t // tm, dff // tf
    rb = d // n_i
    in_specs = [
        pl.BlockSpec((tm, d), lambda i, j: (i, 0)),
        pl.BlockSpec((1, d), lambda i, j: (0, 0)),
        pl.BlockSpec((d, tf), lambda i, j: (0, j)),
        pl.BlockSpec((d, tf), lambda i, j: (0, j)),
        pl.BlockSpec((tf, d), lambda i, j: (j, 0)),
        pl.BlockSpec((1, d), lambda i, j: (0, 0)),
    ]
    out_specs = [pl.BlockSpec((tm, d), lambda i, j: (i, 0))]
    out_shape = [jax.ShapeDtypeStruct((t, d), F32)]
    for w in cast_tiles:
        if w.shape[1:] == (d, dff):
            in_specs.append(pl.BlockSpec((None, rb, tf), lambda i, j: (layer, i, j)))
            out_specs.append(pl.BlockSpec((rb, tf), lambda i, j: (i, j)))
        else:
            assert w.shape[1:] == (dff, d) and rb % LANES == 0
            in_specs.append(pl.BlockSpec((None, tf, rb), lambda i, j: (layer, j, i)))
            out_specs.append(pl.BlockSpec((tf, rb), lambda i, j: (j, i)))
        out_shape.append(jax.ShapeDtypeStruct(w.shape[1:], BF16))
    row_kj = []
    for r, w in enumerate(cast_rows):
        n_rows, n_cols = w.shape[1:]
        kj = _row_cast_split(n_rows, n_i, n_ff)
        row_kj.append(kj)
        rr = n_rows // (n_i * kj)
        idx = lambda i, j, kj=kj: (i * kj + jnp.minimum(j, kj - 1), 0)
        in_specs.append(pl.BlockSpec((None, rr, n_cols), lambda i, j, idx=idx: (layer,) + idx(i, j)))
        if r == 0 and gate0 is not None:
            out_specs += [pl.BlockSpec((rr, n_cols - gate_w), idx), pl.BlockSpec((rr, LANES), idx)]
            out_shape += [jax.ShapeDtypeStruct((n_rows, n_cols - gate_w), BF16),
                          jax.ShapeDtypeStruct((n_rows, LANES), BF16)]
        else:
            out_specs.append(pl.BlockSpec((rr, n_cols), idx))
            out_shape.append(jax.ShapeDtypeStruct((n_rows, n_cols), BF16))
    body = functools.partial(_ffn_body, n_ff=n_ff, final_norm=final_norm, n_tile=len(cast_tiles),
                             row_kj=tuple(row_kj), gate0=gate0, gate_w=gate_w)
    outs = pl.pallas_call(
        body,
        grid=(n_i, n_ff),
        in_specs=in_specs,
        out_specs=out_specs,
        out_shape=out_shape,
        scratch_shapes=[pltpu.VMEM((tm, d), BF16)],
        compiler_params=_params(("parallel", "arbitrary")),
        name="ffn_final" if final_norm else "ffn",
    )(x, g, w1, w3, w2, gf, *cast_tiles, *cast_rows)
    return outs[0], outs[1:]


def _inproj_body(x_ref, g_ref, w_ref, wg_ref, bg_ref, qk_ref, v_ref, o_ref, u_ref, gates_ref, hn_ref):
    j = pl.program_id(1)
    nh = MLSTM_HEADS

    @pl.when(j == 0)
    def _():
        hn = _rms(x_ref[...], g_ref[...]).astype(BF16)
        hn_ref[...] = hn
        gt = lax.dot_general(wg_ref[...], hn, (((1,), (1,)), ((), ())), preferred_element_type=F32)
        gt = gt + bg_ref[...]
        for h in range(nh):
            gates_ref[h] = jnp.concatenate([gt[h:h + 1, :], gt[nh + h:nh + h + 1, :]], axis=0)

    res = jnp.dot(hn_ref[...], w_ref[...], preferred_element_type=F32)

    @pl.when(j == 0)
    def _():
        qk_ref[...] = res.astype(BF16)

    @pl.when(j == 1)
    def _():
        v_ref[...] = res.astype(BF16)

    @pl.when(j == 2)
    def _():
        o_ref[...] = res.astype(BF16)

    @pl.when(j == 3)
    def _():
        u_ref[...] = res


def _inproj(x, g, w_main, wg_t, bg, *, tm=1024):
    t, d = x.shape
    w = w_main.shape[1] // 4
    blk = lambda: pl.BlockSpec((tm, w), lambda i, j: (i, 0))
    return pl.pallas_call(
        _inproj_body,
        grid=(t // tm, 4),
        in_specs=[
            pl.BlockSpec((tm, d), lambda i, j: (i, 0)),
            pl.BlockSpec((1, d), lambda i, j: (0, 0)),
            pl.BlockSpec((d, w), lambda i, j: (0, j)),
            pl.BlockSpec((2 * MLSTM_HEADS, d), lambda i, j: (0, 0)),
            pl.BlockSpec((2 * MLSTM_HEADS, 1), lambda i, j: (0, 0)),
        ],
        out_specs=[blk(), blk(), blk(), blk(),
                   pl.BlockSpec((MLSTM_HEADS, 2, tm), lambda i, j: (0, 0, i))],
        out_shape=[
            jax.ShapeDtypeStruct((t, w), BF16),
            jax.ShapeDtypeStruct((t, w), BF16),
            jax.ShapeDtypeStruct((t, w), BF16),
            jax.ShapeDtypeStruct((t, w), F32),
            jax.ShapeDtypeStruct((MLSTM_HEADS, 2, t), F32),
        ],
        scratch_shapes=[pltpu.VMEM((tm, d), BF16)],
        compiler_params=_params(("parallel", "arbitrary")),
        name="inproj",
    )(x, g, w_main, wg_t, bg)


def _split3(x):
    hi = x.astype(BF16).astype(F32)
    r1 = x - hi
    mid = r1.astype(BF16).astype(F32)
    lo = (r1 - mid).astype(BF16).astype(F32)
    return [hi, mid, lo]


def _log_sigmoid(x):
    return jnp.minimum(x, 0.0) - jnp.log(1.0 + jnp.exp(-jnp.abs(x)))


def _mlstm_body(q_ref, k_ref, v_ref, og_ref, g_ref, cwq_ref, cwk_ref, cbq_ref, cbk_ref, nw_ref,
                sel_ref, out_ref, xp_ref, qs_ref, ks_ref, cols_ref, *, seq, dk, dv):
    L = MLSTM_CHUNK
    nc = seq // L

    def conv_silu(x_ref, w_ref, b_ref, dst_ref, scale):
        xp_ref[0:SUBLANES, :] = jnp.zeros((SUBLANES, dk), F32)
        xp_ref[SUBLANES:, :] = x_ref[...].astype(F32)
        w = w_ref[...]
        acc = b_ref[...] + xp_ref[pl.ds(SUBLANES, seq), :] * w[3:4, :]
        for d in (1, 2, 3):
            acc = acc + xp_ref[pl.ds(SUBLANES - d, seq), :] * w[3 - d:4 - d, :]
        y = acc * jax.nn.sigmoid(acc)
        if scale != 1.0:
            y = y * scale
        dst_ref[...] = y.astype(BF16)

    conv_silu(q_ref, cwq_ref, cbq_ref, qs_ref, dk ** -0.5)
    conv_silu(k_ref, cwk_ref, cbk_ref, ks_ref, 1.0)

    rows = lax.broadcasted_iota(jnp.int32, (L, L), 0)
    cols = lax.broadcasted_iota(jnp.int32, (L, L), 1)
    causal = cols <= rows
    tri = jnp.where(rows <= cols, 1.0, 0.0).astype(BF16)

    ncp = -(-nc // SUBLANES) * SUBLANES
    pad = [jnp.zeros((ncp - nc, L), F32)] if ncp > nc else []
    g = g_ref[...]
    li_all = jnp.concatenate([g[0:1, c * L:(c + 1) * L] for c in range(nc)] + pad, axis=0)
    lf_all = jnp.concatenate([_log_sigmoid(g[1:2, c * L:(c + 1) * L]) for c in range(nc)] + pad, axis=0)
    parts = jnp.concatenate(_split3(lf_all), axis=0).astype(BF16)
    cs = jnp.dot(parts, tri, preferred_element_type=F32)
    b_all = cs[0:ncp] + cs[ncp:2 * ncp] + cs[2 * ncp:3 * ncp]
    d_all = li_all - b_all
    lane = lax.broadcasted_iota(jnp.int32, (ncp, L), 1)
    cm_all = d_all
    shift = 1
    while shift < L:
        cm_all = jnp.maximum(cm_all, jnp.where(lane >= shift, pltpu.roll(cm_all, shift, 1), -jnp.inf))
        shift *= 2
    at = jnp.concatenate(_split3(b_all) + _split3(d_all) + _split3(cm_all), axis=0).astype(BF16)
    cols_ref[...] = lax.dot_general(at, sel_ref[...], (((0,), (0,)), ((), ())), preferred_element_type=F32)

    ext = 2 * LANES
    rep = lambda a: jnp.concatenate([a] * (dv // LANES), axis=1)
    ones_l = jnp.ones((L, ext), BF16)
    mean_w = jnp.full((dv, LANES), 1.0 / dv, BF16)
    ct_ext = jnp.zeros((dk, dv + ext), F32)
    m_prev = jnp.zeros((1, 1), F32)
    nw = nw_ref[...]

    for c in range(nc):
        r0 = c * L
        q = qs_ref[pl.ds(r0, L), :]
        k = ks_ref[pl.ds(r0, L), :]
        v_ext = jnp.concatenate([v_ref[pl.ds(r0, L), :], ones_l], axis=1)
        d_row = d_all[c:c + 1, :]
        b_end = b_all[c:c + 1, L - 1:L]
        b_c = cols_ref[:, c * LANES:(c + 1) * LANES]
        d_c = cols_ref[:, (nc + c) * LANES:(nc + c + 1) * LANES]
        cm_c = cols_ref[:, (2 * nc + c) * LANES:(2 * nc + c + 1) * LANES]

        m_t = b_c + jnp.maximum(m_prev, cm_c)
        a_t = jnp.exp(b_c + m_prev - m_t)
        e_mt = jnp.exp(-m_t)
        arg = jnp.concatenate([b_c - m_t] * (L // LANES), axis=1) + d_row
        d_mat = jnp.where(causal, jnp.exp(arg), 0.0)

        qk = lax.dot_general(q, k, (((1,), (1,)), ((), ())), preferred_element_type=F32)
        s_b = (qk * d_mat).astype(BF16)
        sv = jnp.dot(s_b, v_ext, preferred_element_type=F32)
        qc = jnp.dot(q, ct_ext.astype(BF16), preferred_element_type=F32)
        num = sv[:, :dv] + rep(a_t) * qc[:, :dv]
        den = sv[:, dv:dv + LANES] + a_t * qc[:, dv:dv + LANES]
        inv = 1.0 / jnp.maximum(jnp.abs(den), e_mt)
        h = num * rep(inv)
        msq = jnp.dot((h * h).astype(BF16), mean_w, preferred_element_type=F32)
        hn = h * rep(lax.rsqrt(msq + EPS)) * nw
        og = og_ref[pl.ds(r0, L), :].astype(F32)
        out_ref[pl.ds(r0, L), :] = (hn * jax.nn.sigmoid(og)).astype(BF16)

        w_end = b_end + d_row
        m_loc = jnp.max(w_end, axis=-1, keepdims=True)
        e_c = jnp.exp(b_end + d_c - m_loc)
        ke = (k.astype(F32) * e_c).astype(BF16)
        c_loc = lax.dot_general(ke, v_ext, (((0,), (0,)), ((), ())), preferred_element_type=F32)
        m_new = jnp.maximum(b_end + m_prev, m_loc)
        ct_ext = jnp.exp(b_end + m_prev - m_new) * ct_ext + jnp.exp(m_loc - m_new) * c_loc
        m_prev = m_new


def _mlstm_selector(nc):
    ncp = -(-nc // SUBLANES) * SUBLANES
    sel = np.zeros((9 * ncp, 3 * nc * LANES), np.float32)
    for kind in range(3):
        for part in range(3):
            for c in range(nc):
                sel[(kind * 3 + part) * ncp + c, (kind * nc + c) * LANES:(kind * nc + c + 1) * LANES] = 1.0
    return jnp.asarray(sel, BF16)


def _mlstm(qk, v, og, gates, conv_w, conv_b, norm_w, *, batch, seq):
    t, w = qk.shape
    nh = MLSTM_HEADS
    dk = w // (2 * nh)
    dv = v.shape[1] // nh
    assert dk == LANES and dv % LANES == 0 and seq % MLSTM_CHUNK == 0 and MLSTM_CHUNK % LANES == 0
    nc = seq // MLSTM_CHUNK
    sel = _mlstm_selector(nc)
    body = functools.partial(_mlstm_body, seq=seq, dk=dk, dv=dv)
    return pl.pallas_call(
        body,
        grid=(batch, nh),
        in_specs=[
            pl.BlockSpec((seq, dk), lambda b, h: (b, h)),
            pl.BlockSpec((seq, dk), lambda b, h: (b, nh + h)),
            pl.BlockSpec((seq, dv), lambda b, h: (b, h)),
            pl.BlockSpec((seq, dv), lambda b, h: (b, h)),
            pl.BlockSpec((None, 2, seq), lambda b, h: (h, 0, b)),
            pl.BlockSpec((conv_w.shape[0], dk), lambda b, h: (0, h)),
            pl.BlockSpec((conv_w.shape[0], dk), lambda b, h: (0, nh + h)),
            pl.BlockSpec((1, dk), lambda b, h: (0, h)),
            pl.BlockSpec((1, dk), lambda b, h: (0, nh + h)),
            pl.BlockSpec((1, dv), lambda b, h: (0, h)),
            pl.BlockSpec(sel.shape, lambda b, h: (0, 0)),
        ],
        out_specs=pl.BlockSpec((seq, dv), lambda b, h: (b, h)),
        out_shape=jax.ShapeDtypeStruct((t, nh * dv), BF16),
        scratch_shapes=[
            pltpu.VMEM((seq + SUBLANES, dk), F32),
            pltpu.VMEM((seq, dk), BF16),
            pltpu.VMEM((seq, dk), BF16),
            pltpu.VMEM((MLSTM_CHUNK, 3 * nc * LANES), F32),
        ],
        compiler_params=_params(("parallel", "parallel")),
        name="mlstm",
    )(qk, qk, v, og, gates, conv_w, conv_w, conv_b, conv_b, norm_w, sel)


def _cmul(ar, ai, br, bi):
    return ar * br - ai * bi, ar * bi + ai * br


def _s5pack_body(lr_ref, li_ref, ldt_ref, br_ref, bi_ref, cr_ref, ci_ref, tile_ref,
                 bm_ref, cm_ref, dm_ref, lpr_ref, lpi_ref):
    R = S5_LIFT
    cb = S5_GROUPS_PER_BLOCK * S5_GROUP
    half = S5_GROUPS_PER_BLOCK * S5_STATE
    lr = lr_ref[...]
    li = li_ref[...]
    dt = jnp.exp(ldt_ref[...])
    mag = jnp.exp(lr * dt)
    ang = li * dt
    lbr = mag * jnp.cos(ang)
    lbi = mag * jnp.sin(ang)
    nr = lbr - 1.0
    den = lr * lr + li * li
    cr = (nr * lr + lbi * li) / den
    ci = (lbi * lr - nr * li) / den
    pr, pi_ = _cmul(cr, ci, br_ref[...], bi_ref[...])
    qr, qi = cr_ref[...], ci_ref[...]

    row = lax.broadcasted_iota(jnp.int32, (cb, half), 0)
    lane = lax.broadcasted_iota(jnp.int32, (cb, half), 1)
    same_group = (row // S5_GROUP) == (lane // S5_STATE)
    tile = tile_ref[...]

    def blockdiag(a):
        return jnp.where(same_group, jnp.dot(a.astype(BF16), tile, preferred_element_type=F32), 0.0)

    def pair(re, im):
        return jnp.concatenate([blockdiag(re), blockdiag(im)], axis=1).astype(BF16)

    c0 = pair(qr, -qi)
    lag_rows = []
    for m in range(R):
        rows_m = pair(pr, pi_)
        bm_ref[(R - 1 - m) * cb:(R - m) * cb, :] = rows_m
        lag_rows.append(rows_m)
        pr, pi_ = _cmul(pr, pi_, lbr, lbi)
        qr, qi = _cmul(qr, qi, lbr, lbi)
        cm_ref[m * cb:(m + 1) * cb, :] = pair(qr, -qi)
    dlag = [lax.dot_general(lag_rows[m], c0, (((1,), (1,)), ((), ())), preferred_element_type=F32).astype(BF16)
            for m in range(R)]
    zero = jnp.zeros((cb, cb), BF16)
    for rp in range(R):
        for r in range(R):
            dm_ref[rp * cb:(rp + 1) * cb, r * cb:(r + 1) * cb] = dlag[r - rp] if r >= rp else zero

    wr, wi = lbr, lbi
    for _ in range(R - 1):
        wr, wi = _cmul(wr, wi, lbr, lbi)
    first = same_group & ((row % S5_GROUP) == 0)

    def lane_vector(a):
        t = sum(jnp.dot(part.astype(BF16), tile, preferred_element_type=F32) for part in _split3(a))
        return jnp.sum(jnp.where(first, t, 0.0), axis=0, keepdims=True)

    lpr_ref[...] = lane_vector(wr)
    lpi_ref[...] = lane_vector(wi)


def _s5_weights(lam_re, lam_im, log_dt, b_re, b_im, c_re, c_im):
    g, p = lam_re.shape
    hch = b_re.shape[-1]
    gpb = S5_GROUPS_PER_BLOCK
    ngb = g // gpb
    R = S5_LIFT
    cb, half = gpb * hch, gpb * p
    rep = lambda a: jnp.repeat(a, hch, axis=0)
    to_rows = lambda a: jnp.transpose(a, (0, 2, 1)).reshape(g * hch, p)
    ldt = jnp.broadcast_to(log_dt[:, None], (g, p))
    tile = jnp.asarray(np.tile(np.eye(p, dtype=np.float32), (1, gpb)), BF16)
    blk = pl.BlockSpec((cb, p), lambda i: (i, 0))
    return pl.pallas_call(
        _s5pack_body,
        grid=(ngb,),
        in_specs=[blk] * 7 + [pl.BlockSpec((p, half), lambda i: (0, 0))],
        out_specs=[pl.BlockSpec((None, R * cb, 2 * half), lambda i: (i, 0, 0)),
                   pl.BlockSpec((None, R * cb, 2 * half), lambda i: (i, 0, 0)),
                   pl.BlockSpec((None, R * cb, R * cb), lambda i: (i, 0, 0)),
                   pl.BlockSpec((None, 1, half), lambda i: (i, 0, 0)),
                   pl.BlockSpec((None, 1, half), lambda i: (i, 0, 0))],
        out_shape=[jax.ShapeDtypeStruct((ngb, R * cb, 2 * half), BF16),
                   jax.ShapeDtypeStruct((ngb, R * cb, 2 * half), BF16),
                   jax.ShapeDtypeStruct((ngb, R * cb, R * cb), BF16),
                   jax.ShapeDtypeStruct((ngb, 1, half), F32),
                   jax.ShapeDtypeStruct((ngb, 1, half), F32)],
        name="s5pack",
    )(rep(lam_re), rep(lam_im), rep(ldt), to_rows(b_re), to_rows(b_im),
      c_re.reshape(g * hch, p), c_im.reshape(g * hch, p), tile)


def _s5_body(u_ref, bm_ref, cm_ref, dm_ref, lr_ref, li_ref, dsk_ref, wglu_ref, bglu_ref, o_ref,
             usb_ref, sb_ref, y_ref, st_ref, *, ts, ngb):
    R = S5_LIFT
    nk = ts // R
    rows = nk * SUBLANES
    half = S5_GROUPS_PER_BLOCK * S5_STATE
    cb = S5_GROUPS_PER_BLOCK * S5_GROUP

    @pl.when(pl.program_id(0) == 0)
    def _():
        st_ref[...] = jnp.zeros_like(st_ref)

    def srow(s):
        return (s % R) * rows + (s // R) * SUBLANES

    for s in range(ts):
        usb_ref[srow(s):srow(s) + SUBLANES, :] = u_ref[:, s, :]

    def lhs(gb):
        return jnp.concatenate([usb_ref[r * rows:(r + 1) * rows, gb * cb:(gb + 1) * cb] for r in range(R)],
                               axis=1).astype(BF16)

    def project_in(gb):
        sb_ref[gb % 2, SUBLANES:, :] = jnp.dot(lhs(gb), bm_ref[gb], preferred_element_type=F32)

    project_in(0)
    for gb in range(ngb):
        if gb + 1 < ngb:
            project_in(gb + 1)
        buf = sb_ref.at[gb % 2]
        lrb = jnp.broadcast_to(lr_ref[gb], (SUBLANES, half))
        lib = jnp.broadcast_to(li_ref[gb], (SUBLANES, half))
        xr = st_ref[gb, :, 0:half]
        xi = st_ref[gb, :, half:2 * half]
        buf[0:SUBLANES, 0:half] = xr
        buf[0:SUBLANES, half:2 * half] = xi
        for k in range(nk):
            rs = slice((k + 1) * SUBLANES, (k + 2) * SUBLANES)
            nr = lrb * xr - lib * xi + buf[rs, 0:half]
            ni = lrb * xi + lib * xr + buf[rs, half:2 * half]
            buf[rs, 0:half] = nr
            buf[rs, half:2 * half] = ni
            xr, xi = nr, ni
        st_ref[gb, :, 0:half] = xr
        st_ref[gb, :, half:2 * half] = xi

        yo = lax.dot_general(buf[0:rows, :].astype(BF16), cm_ref[gb], (((1,), (1,)), ((), ())),
                             preferred_element_type=F32)
        yd = jnp.dot(lhs(gb), dm_ref[gb], preferred_element_type=F32)
        for r in range(R):
            ug = usb_ref[r * rows:(r + 1) * rows, gb * cb:(gb + 1) * cb]
            yg = yo[:, r * cb:(r + 1) * cb] + yd[:, r * cb:(r + 1) * cb] + dsk_ref[:, gb * cb:(gb + 1) * cb] * ug
            y_ref[r * rows:(r + 1) * rows, gb * cb:(gb + 1) * cb] = jax.nn.gelu(yg)

    y = y_ref[...]
    z = jnp.dot(y.astype(BF16), wglu_ref[...], preferred_element_type=F32) + bglu_ref[...]
    usb_ref[...] = y * jax.nn.sigmoid(z)

    for s in range(ts):
        o_ref[:, s, :] = usb_ref[srow(s):srow(s) + SUBLANES, :]


def _s5(u, bm, cm, dm, lpr, lpi, dsk, wglu, bglu, *, ts=64):
    batch, seq, width = u.shape
    ngb = bm.shape[0]
    half = S5_GROUPS_PER_BLOCK * S5_STATE
    assert batch == SUBLANES and ts % S5_LIFT == 0 and seq % ts == 0
    body = functools.partial(_s5_body, ts=ts, ngb=ngb)
    full = lambda a: pl.BlockSpec(a.shape, lambda i, n=a.ndim: (0,) * n)
    return pl.pallas_call(
        body,
        grid=(seq // ts,),
        in_specs=[pl.BlockSpec((batch, ts, width), lambda i: (0, i, 0)),
                  full(bm), full(cm), full(dm), full(lpr), full(lpi), full(dsk), full(wglu), full(bglu)],
        out_specs=pl.BlockSpec((batch, ts, width), lambda i: (0, i, 0)),
        out_shape=jax.ShapeDtypeStruct((batch, seq, width), F32),
        scratch_shapes=[
            pltpu.VMEM((ts * batch, width), F32),
            pltpu.VMEM((2, (ts // S5_LIFT + 1) * batch, 2 * half), F32),
            pltpu.VMEM((ts * batch, width), F32),
            pltpu.VMEM((ngb, batch, 2 * half), F32),
        ],
        compiler_params=_params(("arbitrary",)),
        name="s5",
    )(u, bm, cm, dm, lpr, lpi, dsk, wglu, bglu)


def _outproj_body(x_ref, hm_ref, hs_ref, wa_ref, wb_ref, o_ref):
    acc = jnp.dot(hm_ref[...], wa_ref[...], preferred_element_type=F32)
    acc = acc + jnp.dot(hs_ref[...].astype(BF16), wb_ref[...], preferred_element_type=F32)
    o_ref[...] = x_ref[...] + acc


def _outproj(x, hm, hs, wo, *, tm=512):
    t, d = x.shape
    w = hm.shape[1]
    return pl.pallas_call(
        _outproj_body,
        grid=(t // tm,),
        in_specs=[
            pl.BlockSpec((tm, d), lambda i: (i, 0)),
            pl.BlockSpec((tm, w), lambda i: (i, 0)),
            pl.BlockSpec((tm, w), lambda i: (i, 0)),
            pl.BlockSpec((w, d), lambda i: (0, 0)),
            pl.BlockSpec((hs.shape[1], d), lambda i: (w // hs.shape[1], 0)),
        ],
        out_specs=pl.BlockSpec((tm, d), lambda i: (i, 0)),
        out_shape=jax.ShapeDtypeStruct((t, d), F32),
        compiler_params=_params(("parallel",)),
        name="outproj",
    )(x, hm, hs, wo, wo)


def kernel(x, ffn1_norm, ffn1_w1, ffn1_w3, ffn1_w2, mix_norm, w_in, conv_w, conv_b, b_i, b_f, mlstm_norm, lam_re, lam_im, log_dt, b_re, b_im, c_re, c_im, d_skip, w_glu, b_glu, w_out, ffn2_norm, ffn2_w1, ffn2_w3, ffn2_w2, final_norm):
    batch, seq, d = x.shape
    depth = ffn1_norm.shape[0]
    nh = MLSTM_HEADS
    qk_cols = conv_w.shape[-1]
    mw = mlstm_norm.shape[-1]
    gate0 = qk_cols + 2 * mw
    sw = w_glu.shape[-1]
    row = lambda a: a.reshape(1, -1).astype(F32)
    xt = x.reshape(batch * seq, d)
    gfin = row(final_norm)
    for l in range(depth):
        xt, (w1b, w3b, w2b, w_main, wg_cols, wo, wglu) = _ffn(
            xt, row(ffn1_norm[l]), ffn1_w1[l].astype(BF16), ffn1_w3[l].astype(BF16), ffn1_w2[l].astype(BF16), gfin,
            final_norm=False, layer=l, cast_tiles=(ffn2_w1, ffn2_w3, ffn2_w2), cast_rows=(w_in, w_out, w_glu),
            gate0=gate0, gate_w=2 * nh)

        wg_t = wg_cols[:, :2 * nh].T
        bg = jnp.concatenate([b_i[l], b_f[l]]).reshape(2 * nh, 1).astype(F32)
        qk, v, og, u, gates = _inproj(xt, row(mix_norm[l]), w_main, wg_t, bg)

        hm = _mlstm(qk, v, og, gates, conv_w[l].astype(F32), row(conv_b[l]), row(mlstm_norm[l]),
                    batch=batch, seq=seq)

        bm, cm, dm, lpr, lpi = _s5_weights(lam_re[l], lam_im[l], log_dt[l], b_re[l], b_im[l], c_re[l], c_im[l])
        hs = _s5(u.reshape(batch, seq, sw), bm, cm, dm, lpr, lpi, row(d_skip[l]), wglu, row(b_glu[l]))

        xt = _outproj(xt, hm, hs.reshape(batch * seq, sw), wo)

        xt, _ = _ffn(xt, row(ffn2_norm[l]), w1b, w3b, w2b, gfin, final_norm=(l == depth - 1))
    return xt.reshape(batch, seq, d)
```

```python
import functools

import jax
import jax.numpy as jnp
import numpy as np
from jax import lax
from jax.experimental import pallas as pl
from jax.experimental.pallas import tpu as pltpu

F32 = jnp.float32
BF16 = jnp.bfloat16

EPS = 1e-6
MLSTM_HEADS = 4
S5_GROUP = 16
S5_STATE = 64
S5_GROUPS_PER_BLOCK = 8
S5_LIFT = 2
LANES = 128
SUBLANES = 8
MLSTM_CHUNK = 256
TR_SPLIT = 4
VMEM_LIMIT_BYTES = 58 * 1024 * 1024


def _rms(x, g):
    return x * lax.rsqrt(jnp.mean(x * x, axis=-1, keepdims=True) + EPS) * g


def _params(sem):
    return pltpu.CompilerParams(dimension_semantics=sem, vmem_limit_bytes=VMEM_LIMIT_BYTES)


def _row_cast_split(n_rows, n_i, n_ff):
    kj = 1
    while kj * 2 <= min(n_ff, 8) and (n_rows // (n_i * kj * 2)) % 16 == 0 and n_rows % (n_i * kj * 2) == 0:
        kj *= 2
    assert n_rows % (n_i * kj) == 0 and (n_rows // (n_i * kj)) % 16 == 0
    return kj


def _ffn_body(*refs, n_ff, final_norm, n_tile, row_kj, tr_kj, gate0, gate_w):
    n_row = len(row_kj)
    n_tr = 2 if tr_kj else 0
    x_ref, g_ref, w1_ref, w3_ref, w2_ref, gf_ref = refs[:6]
    refs = refs[1:]
    tile_src = refs[5:5 + n_tile]
    row_src = refs[5 + n_tile:5 + n_tile + n_row]
    o_ref = refs[5 + n_tile + n_row + n_tr]
    tile_dst = refs[6 + n_tile + n_row + n_tr:6 + 2 * n_tile + n_row + n_tr]
    row_dst = refs[6 + 2 * n_tile + n_row + n_tr:6 + 2 * n_tile + 2 * n_row + n_tr]
    xn_ref = refs[-1]
    j = pl.program_id(1)

    @pl.when(j == 0)
    def _():
        x = x_ref[...]
        xn_ref[...] = _rms(x, g_ref[...]).astype(BF16)
        o_ref[...] = x

    xn = xn_ref[...]
    a = jnp.dot(xn, w1_ref[...], preferred_element_type=F32)
    b = jnp.dot(xn, w3_ref[...], preferred_element_type=F32)
    h = (a * jax.nn.sigmoid(a)) * (b * 0.5)
    o_ref[...] += jnp.dot(h.astype(BF16), w2_ref[...], preferred_element_type=F32)

    if final_norm:
        @pl.when(j == n_ff - 1)
        def _():
            o_ref[...] = _rms(o_ref[...], gf_ref[...])

    for src, dst in zip(tile_src, tile_dst):
        dst[...] = src[...].astype(BF16)
    for r, (src, kj) in enumerate(zip(row_src, row_kj)):
        @pl.when(j < kj)
        def _(src=src, r=r):
            row_dst[r][...] = src[...].astype(BF16)

    if tr_kj:
        ta_ref, tb_ref = refs[5 + n_tile + n_row], refs[6 + n_tile + n_row]
        main_ref = refs[-2]
        m = (pl.program_id(0) * tr_kj + j) // TR_SPLIT

        @pl.when((j < tr_kj) & (m < gate0 // LANES))
        def _():
            main_ref[...] = ta_ref[...].T.astype(BF16)

        @pl.when((j < tr_kj) & (m >= gate0 // LANES))
        def _():
            shifted = jnp.concatenate([ta_ref[gate_w:, :], tb_ref[:gate_w, :]], axis=0)
            main_ref[...] = shifted.T.astype(BF16)


def _ffn(x, g, w1, w3, w2, gf, *, final_norm, tm=1024, tf=512, layer=0, cast_tiles=(), cast_rows=(), w_in_t=None,
         gate0=None, gate_w=0):
    t, d = x.shape
    dff = w1.shape[1]
    n_i, n_ff = t // tm, dff // tf
    rb = d // n_i
    in_specs = [
        pl.BlockSpec((tm, d), lambda i, j: (i, 0)),
        pl.BlockSpec((1, d), lambda i, j: (0, 0)),
        pl.BlockSpec((d, tf), lambda i, j: (0, j)),
        pl.BlockSpec((d, tf), lambda i, j: (0, j)),
        pl.BlockSpec((tf, d), lambda i, j: (j, 0)),
        pl.BlockSpec((1, d), lambda i, j: (0, 0)),
    ]
    out_specs = [pl.BlockSpec((tm, d), lambda i, j: (i, 0))]
    out_shape = [jax.ShapeDtypeStruct((t, d), F32)]
    for w in cast_tiles:
        if w.shape[1:] == (d, dff):
            in_specs.append(pl.BlockSpec((None, rb, tf), lambda i, j: (layer, i, j)))
            out_specs.append(pl.BlockSpec((rb, tf), lambda i, j: (i, j)))
        else:
            assert w.shape[1:] == (dff, d) and rb % LANES == 0
            in_specs.append(pl.BlockSpec((None, tf, rb), lambda i, j: (layer, j, i)))
            out_specs.append(pl.BlockSpec((tf, rb), lambda i, j: (j, i)))
        out_shape.append(jax.ShapeDtypeStruct(w.shape[1:], BF16))
    row_kj = []
    for r, w in enumerate(cast_rows):
        n_rows, n_cols = w.shape[1:]
        kj = _row_cast_split(n_rows, n_i, n_ff)
        row_kj.append(kj)
        rr = n_rows // (n_i * kj)
        idx = lambda i, j, kj=kj: (i * kj + jnp.minimum(j, kj - 1), 0)
        in_specs.append(pl.BlockSpec((None, rr, n_cols), lambda i, j, idx=idx: (layer,) + idx(i, j)))
        out_specs.append(pl.BlockSpec((rr, n_cols), idx))
        out_shape.append(jax.ShapeDtypeStruct((n_rows, n_cols), BF16))
    tr_kj, tr_in = 0, ()
    if w_in_t is not None:
        n_main = w_in_t.shape[1] - gate_w
        sp = TR_SPLIT
        assert w_in_t.shape[2] == d and gate0 % LANES == 0 and n_main % (LANES * n_i) == 0 and d % (sp * LANES) == 0
        tr_kj = sp * n_main // (LANES * n_i)
        assert tr_kj <= n_ff
        job = lambda i, j: i * tr_kj + jnp.minimum(j, tr_kj - 1)
        in_specs += [pl.BlockSpec((None, LANES, d // sp), lambda i, j: (layer, job(i, j) // sp, job(i, j) % sp)),
                     pl.BlockSpec((None, LANES, d // sp), lambda i, j: (layer, job(i, j) // sp + 1, job(i, j) % sp))]
        out_specs.append(pl.BlockSpec((d // sp, LANES), lambda i, j: (job(i, j) % sp, job(i, j) // sp)))
        out_shape.append(jax.ShapeDtypeStruct((d, n_main), BF16))
        tr_in = (w_in_t, w_in_t)
    body = functools.partial(_ffn_body, n_ff=n_ff, final_norm=final_norm, n_tile=len(cast_tiles),
                             row_kj=tuple(row_kj), tr_kj=tr_kj, gate0=gate0, gate_w=gate_w)
    outs = pl.pallas_call(
        body,
        grid=(n_i, n_ff),
        in_specs=in_specs,
        out_specs=out_specs,
        out_shape=out_shape,
        scratch_shapes=[pltpu.VMEM((tm, d), BF16)],
        compiler_params=_params(("parallel", "arbitrary")),
        name="ffn_final" if final_norm else "ffn",
    )(x, g, w1, w3, w2, gf, *cast_tiles, *cast_rows, *tr_in)
    return outs[0], outs[1:]


def _inproj_body(*refs, n_cast):
    x_ref, g_ref, w_ref, wg_ref, bg_ref = refs[:5]
    cast_src = refs[5:5 + n_cast]
    qk_ref, v_ref, o_ref, u_ref, gates_ref = refs[5 + n_cast:10 + n_cast]
    cast_dst = refs[10 + n_cast:10 + 2 * n_cast]
    hn_ref = refs[-1]
    j = pl.program_id(1)
    nh = MLSTM_HEADS

    @pl.when(j == 0)
    def _():
        hn = _rms(x_ref[...], g_ref[...]).astype(BF16)
        hn_ref[...] = hn
        gt = lax.dot_general(wg_ref[...], hn, (((1,), (1,)), ((), ())), preferred_element_type=F32)
        gt = gt + bg_ref[...]
        for h in range(nh):
            gates_ref[h] = jnp.concatenate([gt[h:h + 1, :], gt[nh + h:nh + h + 1, :]], axis=0)

    res = jnp.dot(hn_ref[...], w_ref[...], preferred_element_type=F32)

    @pl.when(j == 0)
    def _():
        qk_ref[...] = res.astype(BF16)

    @pl.when(j == 1)
    def _():
        v_ref[...] = res.astype(BF16)

    @pl.when(j == 2)
    def _():
        o_ref[...] = res.astype(BF16)

    @pl.when(j == 3)
    def _():
        u_ref[...] = res

    for src, dst in zip(cast_src, cast_dst):
        dst[...] = src[...].astype(BF16)


def _inproj(x, g, w_main, wg_t, bg, *, tm=1024, layer=0, cast_rows=()):
    t, d = x.shape
    w = w_main.shape[1] // 4
    n_i, n_j = t // tm, 4
    blk = lambda: pl.BlockSpec((tm, w), lambda i, j: (i, 0))
    in_specs = [
        pl.BlockSpec((tm, d), lambda i, j: (i, 0)),
        pl.BlockSpec((1, d), lambda i, j: (0, 0)),
        pl.BlockSpec((d, w), lambda i, j: (0, j)),
        pl.BlockSpec((2 * MLSTM_HEADS, d), lambda i, j: (0, 0)),
        pl.BlockSpec((2 * MLSTM_HEADS, 1), lambda i, j: (0, 0)),
    ]
    out_specs = [blk(), blk(), blk(), blk(), pl.BlockSpec((MLSTM_HEADS, 2, tm), lambda i, j: (0, 0, i))]
    out_shape = [
        jax.ShapeDtypeStruct((t, w), BF16),
        jax.ShapeDtypeStruct((t, w), BF16),
        jax.ShapeDtypeStruct((t, w), BF16),
        jax.ShapeDtypeStruct((t, w), F32),
        jax.ShapeDtypeStruct((MLSTM_HEADS, 2, t), F32),
    ]
    for a in cast_rows:
        n_rows, n_cols = a.shape[1:]
        rr = n_rows // (n_i * n_j)
        assert n_rows % (n_i * n_j) == 0 and rr % 16 == 0
        in_specs.append(pl.BlockSpec((None, rr, n_cols), lambda i, j: (layer, i * n_j + j, 0)))
        out_specs.append(pl.BlockSpec((rr, n_cols), lambda i, j: (i * n_j + j, 0)))
        out_shape.append(jax.ShapeDtypeStruct((n_rows, n_cols), BF16))
    outs = pl.pallas_call(
        functools.partial(_inproj_body, n_cast=len(cast_rows)),
        grid=(n_i, n_j),
        in_specs=in_specs,
        out_specs=out_specs,
        out_shape=out_shape,
        scratch_shapes=[pltpu.VMEM((tm, d), BF16)],
        compiler_params=_params(("parallel", "arbitrary")),
        name="inproj",
    )(x, g, w_main, wg_t, bg, *cast_rows)
    return outs[:5], outs[5:]


def _split3(x):
    hi = x.astype(BF16).astype(F32)
    r1 = x - hi
    mid = r1.astype(BF16).astype(F32)
    lo = (r1 - mid).astype(BF16).astype(F32)
    return [hi, mid, lo]


def _log_sigmoid(x):
    return jnp.minimum(x, 0.0) - jnp.log(1.0 + jnp.exp(-jnp.abs(x)))


def _mlstm_body(q_ref, k_ref, v_ref, og_ref, g_ref, cwq_ref, cwk_ref, cbq_ref, cbk_ref, nw_ref,
                sel_ref, out_ref, xp_ref, qs_ref, ks_ref, cols_ref, *, seq, dk, dv):
    L = MLSTM_CHUNK
    nc = seq // L

    def conv_silu(x_ref, w_ref, b_ref, dst_ref, scale):
        xp_ref[0:SUBLANES, :] = jnp.zeros((SUBLANES, dk), F32)
        xp_ref[SUBLANES:, :] = x_ref[...].astype(F32)
        w = w_ref[...]
        acc = b_ref[...] + xp_ref[pl.ds(SUBLANES, seq), :] * w[3:4, :]
        for d in (1, 2, 3):
            acc = acc + xp_ref[pl.ds(SUBLANES - d, seq), :] * w[3 - d:4 - d, :]
        y = acc * jax.nn.sigmoid(acc)
        if scale != 1.0:
            y = y * scale
        dst_ref[...] = y.astype(BF16)

    conv_silu(q_ref, cwq_ref, cbq_ref, qs_ref, dk ** -0.5)
    conv_silu(k_ref, cwk_ref, cbk_ref, ks_ref, 1.0)

    rows = lax.broadcasted_iota(jnp.int32, (L, L), 0)
    cols = lax.broadcasted_iota(jnp.int32, (L, L), 1)
    causal = cols <= rows
    tri = jnp.where(rows <= cols, 1.0, 0.0).astype(BF16)

    ncp = -(-nc // SUBLANES) * SUBLANES
    pad = [jnp.zeros((ncp - nc, L), F32)] if ncp > nc else []
    g = g_ref[...]
    li_all = jnp.concatenate([g[0:1, c * L:(c + 1) * L] for c in range(nc)] + pad, axis=0)
    lf_all = jnp.concatenate([_log_sigmoid(g[1:2, c * L:(c + 1) * L]) for c in range(nc)] + pad, axis=0)
    parts = jnp.concatenate(_split3(lf_all), axis=0).astype(BF16)
    cs = jnp.dot(parts, tri, preferred_element_type=F32)
    b_all = cs[0:ncp] + cs[ncp:2 * ncp] + cs[2 * ncp:3 * ncp]
    d_all = li_all - b_all
    lane = lax.broadcasted_iota(jnp.int32, (ncp, L), 1)
    cm_all = d_all
    shift = 1
    while shift < L:
        cm_all = jnp.maximum(cm_all, jnp.where(lane >= shift, pltpu.roll(cm_all, shift, 1), -jnp.inf))
        shift *= 2
    at = jnp.concatenate(_split3(b_all) + _split3(d_all) + _split3(cm_all), axis=0).astype(BF16)
    cols_ref[...] = lax.dot_general(at, sel_ref[...], (((0,), (0,)), ((), ())), preferred_element_type=F32)

    ext = 2 * LANES
    rep = lambda a: jnp.concatenate([a] * (dv // LANES), axis=1)
    ones_l = jnp.ones((L, ext), BF16)
    mean_w = jnp.full((dv, LANES), 1.0 / dv, BF16)
    ct_ext = jnp.zeros((dk, dv + ext), F32)
    m_prev = jnp.zeros((1, 1), F32)
    nw = nw_ref[...]

    for c in range(nc):
        r0 = c * L
        q = qs_ref[pl.ds(r0, L), :]
        k = ks_ref[pl.ds(r0, L), :]
        v_ext = jnp.concatenate([v_ref[pl.ds(r0, L), :], ones_l], axis=1)
        d_row = d_all[c:c + 1, :]
        b_end = b_all[c:c + 1, L - 1:L]
        b_c = cols_ref[:, c * LANES:(c + 1) * LANES]
        d_c = cols_ref[:, (nc + c) * LANES:(nc + c + 1) * LANES]
        cm_c = cols_ref[:, (2 * nc + c) * LANES:(2 * nc + c + 1) * LANES]

        m_t = b_c + jnp.maximum(m_prev, cm_c)
        a_t = jnp.exp(b_c + m_prev - m_t)
        e_mt = jnp.exp(-m_t)
        arg = jnp.concatenate([b_c - m_t] * (L // LANES), axis=1) + d_row
        d_mat = jnp.where(causal, jnp.exp(arg), 0.0)

        qk = lax.dot_general(q, k, (((1,), (1,)), ((), ())), preferred_element_type=F32)
        s_b = (qk * d_mat).astype(BF16)
        sv = jnp.dot(s_b, v_ext, preferred_element_type=F32)
        qc = jnp.dot(q, ct_ext.astype(BF16), preferred_element_type=F32)
        num = sv[:, :dv] + rep(a_t) * qc[:, :dv]
        den = sv[:, dv:dv + LANES] + a_t * qc[:, dv:dv + LANES]
        inv = 1.0 / jnp.maximum(jnp.abs(den), e_mt)
        h = num * rep(inv)
        msq = jnp.dot((h * h).astype(BF16), mean_w, preferred_element_type=F32)
        hn = h * rep(lax.rsqrt(msq + EPS)) * nw
        og = og_ref[pl.ds(r0, L), :].astype(F32)
        out_ref[pl.ds(r0, L), :] = (hn * jax.nn.sigmoid(og)).astype(BF16)

        w_end = b_end + d_row
        m_loc = jnp.max(w_end, axis=-1, keepdims=True)
        e_c = jnp.exp(b_end + d_c - m_loc)
        ke = (k.astype(F32) * e_c).astype(BF16)
        c_loc = lax.dot_general(ke, v_ext, (((0,), (0,)), ((), ())), preferred_element_type=F32)
        m_new = jnp.maximum(b_end + m_prev, m_loc)
        ct_ext = jnp.exp(b_end + m_prev - m_new) * ct_ext + jnp.exp(m_loc - m_new) * c_loc
        m_prev = m_new


def _mlstm_selector(nc):
    ncp = -(-nc // SUBLANES) * SUBLANES
    sel = np.zeros((9 * ncp, 3 * nc * LANES), np.float32)
    for kind in range(3):
        for part in range(3):
            for c in range(nc):
                sel[(kind * 3 + part) * ncp + c, (kind * nc + c) * LANES:(kind * nc + c + 1) * LANES] = 1.0
    return jnp.asarray(sel, BF16)


def _mlstm(qk, v, og, gates, conv_w, conv_b, norm_w, *, batch, seq):
    t, w = qk.shape
    nh = MLSTM_HEADS
    dk = w // (2 * nh)
    dv = v.shape[1] // nh
    assert dk == LANES and dv % LANES == 0 and seq % MLSTM_CHUNK == 0 and MLSTM_CHUNK % LANES == 0
    nc = seq // MLSTM_CHUNK
    sel = _mlstm_selector(nc)
    body = functools.partial(_mlstm_body, seq=seq, dk=dk, dv=dv)
    return pl.pallas_call(
        body,
        grid=(batch, nh),
        in_specs=[
            pl.BlockSpec((seq, dk), lambda b, h: (b, h)),
            pl.BlockSpec((seq, dk), lambda b, h: (b, nh + h)),
            pl.BlockSpec((seq, dv), lambda b, h: (b, h)),
            pl.BlockSpec((seq, dv), lambda b, h: (b, h)),
            pl.BlockSpec((None, 2, seq), lambda b, h: (h, 0, b)),
            pl.BlockSpec((conv_w.shape[0], dk), lambda b, h: (0, h)),
            pl.BlockSpec((conv_w.shape[0], dk), lambda b, h: (0, nh + h)),
            pl.BlockSpec((1, dk), lambda b, h: (0, h)),
            pl.BlockSpec((1, dk), lambda b, h: (0, nh + h)),
            pl.BlockSpec((1, dv), lambda b, h: (0, h)),
            pl.BlockSpec(sel.shape, lambda b, h: (0, 0)),
        ],
        out_specs=pl.BlockSpec((seq, dv), lambda b, h: (b, h)),
        out_shape=jax.ShapeDtypeStruct((t, nh * dv), BF16),
        scratch_shapes=[
            pltpu.VMEM((seq + SUBLANES, dk), F32),
            pltpu.VMEM((seq, dk), BF16),
            pltpu.VMEM((seq, dk), BF16),
            pltpu.VMEM((MLSTM_CHUNK, 3 * nc * LANES), F32),
        ],
        compiler_params=_params(("parallel", "parallel")),
        name="mlstm",
    )(qk, qk, v, og, gates, conv_w, conv_w, conv_b, conv_b, norm_w, sel)


def _cmul(ar, ai, br, bi):
    return ar * br - ai * bi, ar * bi + ai * br


def _s5pack_body(lr_ref, li_ref, ldt_ref, br_ref, bi_ref, cr_ref, ci_ref, tile_ref,
                 bm_ref, cm_ref, dm_ref, lpr_ref, lpi_ref):
    R = S5_LIFT
    cb = S5_GROUPS_PER_BLOCK * S5_GROUP
    half = S5_GROUPS_PER_BLOCK * S5_STATE
    lr = lr_ref[...]
    li = li_ref[...]
    dt = jnp.exp(ldt_ref[...])
    mag = jnp.exp(lr * dt)
    ang = li * dt
    lbr = mag * jnp.cos(ang)
    lbi = mag * jnp.sin(ang)
    nr = lbr - 1.0
    den = lr * lr + li * li
    cr = (nr * lr + lbi * li) / den
    ci = (lbi * lr - nr * li) / den
    pr, pi_ = _cmul(cr, ci, br_ref[...], bi_ref[...])
    qr, qi = cr_ref[...], ci_ref[...]

    row = lax.broadcasted_iota(jnp.int32, (cb, half), 0)
    lane = lax.broadcasted_iota(jnp.int32, (cb, half), 1)
    same_group = (row // S5_GROUP) == (lane // S5_STATE)
    tile = tile_ref[...]

    def blockdiag(a):
        return jnp.where(same_group, jnp.dot(a.astype(BF16), tile, preferred_element_type=F32), 0.0)

    def pair(re, im):
        return jnp.concatenate([blockdiag(re), blockdiag(im)], axis=1).astype(BF16)

    c0 = pair(qr, -qi)
    lag_rows = []
    for m in range(R):
        rows_m = pair(pr, pi_)
        bm_ref[(R - 1 - m) * cb:(R - m) * cb, :] = rows_m
        lag_rows.append(rows_m)
        pr, pi_ = _cmul(pr, pi_, lbr, lbi)
        qr, qi = _cmul(qr, qi, lbr, lbi)
        cm_ref[m * cb:(m + 1) * cb, :] = pair(qr, -qi)
    dlag = [lax.dot_general(lag_rows[m], c0, (((1,), (1,)), ((), ())), preferred_element_type=F32).astype(BF16)
            for m in range(R)]
    zero = jnp.zeros((cb, cb), BF16)
    for rp in range(R):
        for r in range(R):
            dm_ref[rp * cb:(rp + 1) * cb, r * cb:(r + 1) * cb] = dlag[r - rp] if r >= rp else zero

    wr, wi = lbr, lbi
    for _ in range(R - 1):
        wr, wi = _cmul(wr, wi, lbr, lbi)
    first = same_group & ((row % S5_GROUP) == 0)

    def lane_vector(a):
        t = sum(jnp.dot(part.astype(BF16), tile, preferred_element_type=F32) for part in _split3(a))
        return jnp.sum(jnp.where(first, t, 0.0), axis=0, keepdims=True)

    lpr_ref[...] = lane_vector(wr)
    lpi_ref[...] = lane_vector(wi)


def _s5_weights(lam_re, lam_im, log_dt, b_re, b_im, c_re, c_im):
    g, p = lam_re.shape
    hch = b_re.shape[-1]
    gpb = S5_GROUPS_PER_BLOCK
    ngb = g // gpb
    R = S5_LIFT
    cb, half = gpb * hch, gpb * p
    rep = lambda a: jnp.repeat(a, hch, axis=0)
    to_rows = lambda a: jnp.transpose(a, (0, 2, 1)).reshape(g * hch, p)
    ldt = jnp.broadcast_to(log_dt[:, None], (g, p))
    tile = jnp.asarray(np.tile(np.eye(p, dtype=np.float32), (1, gpb)), BF16)
    blk = pl.BlockSpec((cb, p), lambda i: (i, 0))
    return pl.pallas_call(
        _s5pack_body,
        grid=(ngb,),
        in_specs=[blk] * 7 + [pl.BlockSpec((p, half), lambda i: (0, 0))],
        out_specs=[pl.BlockSpec((None, R * cb, 2 * half), lambda i: (i, 0, 0)),
                   pl.BlockSpec((None, R * cb, 2 * half), lambda i: (i, 0, 0)),
                   pl.BlockSpec((None, R * cb, R * cb), lambda i: (i, 0, 0)),
                   pl.BlockSpec((None, 1, half), lambda i: (i, 0, 0)),
                   pl.BlockSpec((None, 1, half), lambda i: (i, 0, 0))],
        out_shape=[jax.ShapeDtypeStruct((ngb, R * cb, 2 * half), BF16),
                   jax.ShapeDtypeStruct((ngb, R * cb, 2 * half), BF16),
                   jax.ShapeDtypeStruct((ngb, R * cb, R * cb), BF16),
                   jax.ShapeDtypeStruct((ngb, 1, half), F32),
                   jax.ShapeDtypeStruct((ngb, 1, half), F32)],
        name="s5pack",
    )(rep(lam_re), rep(lam_im), rep(ldt), to_rows(b_re), to_rows(b_im),
      c_re.reshape(g * hch, p), c_im.reshape(g * hch, p), tile)


def _s5_body(u_ref, bm_ref, cm_ref, dm_ref, lr_ref, li_ref, dsk_ref, wglu_ref, bglu_ref, o_ref,
             usb_ref, sb_ref, y_ref, st_ref, *, ts, ngb):
    R = S5_LIFT
    nk = ts // R
    rows = nk * SUBLANES
    half = S5_GROUPS_PER_BLOCK * S5_STATE
    cb = S5_GROUPS_PER_BLOCK * S5_GROUP

    @pl.when(pl.program_id(0) == 0)
    def _():
        st_ref[...] = jnp.zeros_like(st_ref)

    def srow(s):
        return (s % R) * rows + (s // R) * SUBLANES

    for s in range(ts):
        usb_ref[srow(s):srow(s) + SUBLANES, :] = u_ref[:, s, :]

    def lhs(gb):
        return jnp.concatenate([usb_ref[r * rows:(r + 1) * rows, gb * cb:(gb + 1) * cb] for r in range(R)],
                               axis=1).astype(BF16)

    def project_in(gb):
        sb_ref[gb % 2, SUBLANES:, :] = jnp.dot(lhs(gb), bm_ref[gb], preferred_element_type=F32)

    project_in(0)
    for gb in range(ngb):
        if gb + 1 < ngb:
            project_in(gb + 1)
        buf = sb_ref.at[gb % 2]
        lrb = jnp.broadcast_to(lr_ref[gb], (SUBLANES, half))
        lib = jnp.broadcast_to(li_ref[gb], (SUBLANES, half))
        xr = st_ref[gb, :, 0:half]
        xi = st_ref[gb, :, half:2 * half]
        buf[0:SUBLANES, 0:half] = xr
        buf[0:SUBLANES, half:2 * half] = xi
        for k in range(nk):
            rs = slice((k + 1) * SUBLANES, (k + 2) * SUBLANES)
            nr = lrb * xr - lib * xi + buf[rs, 0:half]
            ni = lrb * xi + lib * xr + buf[rs, half:2 * half]
            buf[rs, 0:half] = nr
            buf[rs, half:2 * half] = ni
            xr, xi = nr, ni
        st_ref[gb, :, 0:half] = xr
        st_ref[gb, :, half:2 * half] = xi

        yo = lax.dot_general(buf[0:rows, :].astype(BF16), cm_ref[gb], (((1,), (1,)), ((), ())),
                             preferred_element_type=F32)
        yd = jnp.dot(lhs(gb), dm_ref[gb], preferred_element_type=F32)
        for r in range(R):
            ug = usb_ref[r * rows:(r + 1) * rows, gb * cb:(gb + 1) * cb]
            yg = yo[:, r * cb:(r + 1) * cb] + yd[:, r * cb:(r + 1) * cb] + dsk_ref[:, gb * cb:(gb + 1) * cb] * ug
            y_ref[r * rows:(r + 1) * rows, gb * cb:(gb + 1) * cb] = jax.nn.gelu(yg)

    y = y_ref[...]
    z = jnp.dot(y.astype(BF16), wglu_ref[...], preferred_element_type=F32) + bglu_ref[...]
    usb_ref[...] = y * jax.nn.sigmoid(z)

    for s in range(ts):
        o_ref[:, s, :] = usb_ref[srow(s):srow(s) + SUBLANES, :]


def _s5(u, bm, cm, dm, lpr, lpi, dsk, wglu, bglu, *, ts=64):
    batch, seq, width = u.shape
    ngb = bm.shape[0]
    half = S5_GROUPS_PER_BLOCK * S5_STATE
    assert batch == SUBLANES and ts % S5_LIFT == 0 and seq % ts == 0
    body = functools.partial(_s5_body, ts=ts, ngb=ngb)
    full = lambda a: pl.BlockSpec(a.shape, lambda i, n=a.ndim: (0,) * n)
    return pl.pallas_call(
        body,
        grid=(seq // ts,),
        in_specs=[pl.BlockSpec((batch, ts, width), lambda i: (0, i, 0)),
                  full(bm), full(cm), full(dm), full(lpr), full(lpi), full(dsk), full(wglu), full(bglu)],
        out_specs=pl.BlockSpec((batch, ts, width), lambda i: (0, i, 0)),
        out_shape=jax.ShapeDtypeStruct((batch, seq, width), F32),
        scratch_shapes=[
            pltpu.VMEM((ts * batch, width), F32),
            pltpu.VMEM((2, (ts // S5_LIFT + 1) * batch, 2 * half), F32),
            pltpu.VMEM((ts * batch, width), F32),
            pltpu.VMEM((ngb, batch, 2 * half), F32),
        ],
        compiler_params=_params(("arbitrary",)),
        name="s5",
    )(u, bm, cm, dm, lpr, lpi, dsk, wglu, bglu)


def _outproj_body(x_ref, hm_ref, hs_ref, wa_ref, wb_ref, o_ref):
    acc = jnp.dot(hm_ref[...], wa_ref[...], preferred_element_type=F32)
    acc = acc + jnp.dot(hs_ref[...].astype(BF16), wb_ref[...], preferred_element_type=F32)
    o_ref[...] = x_ref[...] + acc


def _outproj(x, hm, hs, wo, *, tm=512):
    t, d = x.shape
    w = hm.shape[1]
    return pl.pallas_call(
        _outproj_body,
        grid=(t // tm,),
        in_specs=[
            pl.BlockSpec((tm, d), lambda i: (i, 0)),
            pl.BlockSpec((tm, w), lambda i: (i, 0)),
            pl.BlockSpec((tm, w), lambda i: (i, 0)),
            pl.BlockSpec((w, d), lambda i: (0, 0)),
            pl.BlockSpec((hs.shape[1], d), lambda i: (w // hs.shape[1], 0)),
        ],
        out_specs=pl.BlockSpec((tm, d), lambda i: (i, 0)),
        out_shape=jax.ShapeDtypeStruct((t, d), F32),
        compiler_params=_params(("parallel",)),
        name="outproj",
    )(x, hm, hs, wo, wo)


def kernel(x, ffn1_norm, ffn1_w1, ffn1_w3, ffn1_w2, mix_norm, w_in, conv_w, conv_b, b_i, b_f, mlstm_norm, lam_re, lam_im, log_dt, b_re, b_im, c_re, c_im, d_skip, w_glu, b_glu, w_out, ffn2_norm, ffn2_w1, ffn2_w3, ffn2_w2, final_norm):
    batch, seq, d = x.shape
    depth = ffn1_norm.shape[0]
    nh = MLSTM_HEADS
    qk_cols = conv_w.shape[-1]
    mw = mlstm_norm.shape[-1]
    gate0 = qk_cols + 2 * mw
    sw = w_glu.shape[-1]
    row = lambda a: a.reshape(1, -1).astype(F32)
    xt = x.reshape(batch * seq, d)
    gfin = row(final_norm)
    w_in_t = jnp.swapaxes(w_in, 1, 2)
    for l in range(depth):
        xt, (w1b, w3b, w2b, w_main) = _ffn(
            xt, row(ffn1_norm[l]), ffn1_w1[l].astype(BF16), ffn1_w3[l].astype(BF16), ffn1_w2[l].astype(BF16), gfin,
            final_norm=False, layer=l, cast_tiles=(ffn2_w1, ffn2_w3, ffn2_w2), w_in_t=w_in_t, gate0=gate0,
            gate_w=2 * nh)

        wg_t = w_in_t[l, gate0:gate0 + 2 * nh, :].astype(BF16)
        bg = jnp.concatenate([b_i[l], b_f[l]]).reshape(2 * nh, 1).astype(F32)
        (qk, v, og, u, gates), (wo, wglu) = _inproj(xt, row(mix_norm[l]), w_main, wg_t, bg, layer=l,
                                                    cast_rows=(w_out, w_glu))

        hm = _mlstm(qk, v, og, gates, conv_w[l].astype(F32), row(conv_b[l]), row(mlstm_norm[l]),
                    batch=batch, seq=seq)

        bm, cm, dm, lpr, lpi = _s5_weights(lam_re[l], lam_im[l], log_dt[l], b_re[l], b_im[l], c_re[l], c_im[l])
        hs = _s5(u.reshape(batch, seq, sw), bm, cm, dm, lpr, lpi, row(d_skip[l]), wglu, row(b_glu[l]))

        xt = _outproj(xt, hm, hs.reshape(batch * seq, sw), wo)

        xt, _ = _ffn(xt, row(ffn2_norm[l]), w1b, w3b, w2b, gfin, final_norm=(l == depth - 1))
    return xt.reshape(batch, seq, d)
```

```python
import functools

import jax
import jax.numpy as jnp
import numpy as np
from jax import lax
from jax.experimental import pallas as pl
from jax.experimental.pallas import tpu as pltpu

F32 = jnp.float32
BF16 = jnp.bfloat16

EPS = 1e-6
MLSTM_HEADS = 4
S5_GROUP = 16
S5_STATE = 64
S5_GROUPS_PER_BLOCK = 8
S5_LIFT = 2
LANES = 128
SUBLANES = 8
MLSTM_CHUNK = 256
TR_SPLIT = 4
VMEM_LIMIT_BYTES = 58 * 1024 * 1024


def _rms(x, g):
    return x * lax.rsqrt(jnp.mean(x * x, axis=-1, keepdims=True) + EPS) * g


def _params(sem):
    return pltpu.CompilerParams(dimension_semantics=sem, vmem_limit_bytes=VMEM_LIMIT_BYTES)


def _ffn_body(*refs, n_ff, final_norm, own_f32, n_tile, tr_kj, tile_lo, gate0, gate_w, n_carry):
    x_ref, g_ref, w1_ref, w3_ref, w2_ref, gf_ref = refs[:6]
    n_tr = 2 if tr_kj else 0
    n_in = 6 + n_tile + n_tr + n_carry
    tile_src = refs[6:6 + n_tile]
    o_ref = refs[n_in]
    n_own = 3 if own_f32 else 0
    own_dst = refs[n_in + 1:n_in + 1 + n_own]
    tile_dst = refs[n_in + 1 + n_own:n_in + 1 + n_own + n_tile]
    xn_ref = refs[-1]
    j = pl.program_id(1)

    @pl.when(j == 0)
    def _():
        x = x_ref[...]
        xn_ref[...] = _rms(x, g_ref[...]).astype(BF16)
        o_ref[...] = x

    w1, w3, w2 = w1_ref[...], w3_ref[...], w2_ref[...]
    if own_f32:
        w1, w3, w2 = w1.astype(BF16), w3.astype(BF16), w2.astype(BF16)
        own_dst[0][...] = w1
        own_dst[1][...] = w3
        own_dst[2][...] = w2
    xn = xn_ref[...]
    a = jnp.dot(xn, w1, preferred_element_type=F32)
    b = jnp.dot(xn, w3, preferred_element_type=F32)
    h = (a * jax.nn.sigmoid(a)) * (b * 0.5)
    o_ref[...] += jnp.dot(h.astype(BF16), w2, preferred_element_type=F32)

    if final_norm:
        @pl.when(j == n_ff - 1)
        def _():
            o_ref[...] = _rms(o_ref[...], gf_ref[...])

    for src, dst in zip(tile_src, tile_dst):
        dst[...] = src[...].astype(BF16)

    if tr_kj:
        ta_ref, tb_ref = refs[6 + n_tile], refs[7 + n_tile]
        main_ref = refs[-2]
        m = ((pl.program_id(0) + tile_lo) * tr_kj + j) // TR_SPLIT

        @pl.when((j < tr_kj) & (m < gate0 // LANES))
        def _():
            main_ref[...] = ta_ref[...].T.astype(BF16)

        @pl.when((j < tr_kj) & (m >= gate0 // LANES))
        def _():
            shifted = jnp.concatenate([ta_ref[gate_w:, :], tb_ref[:gate_w, :]], axis=0)
            main_ref[...] = shifted.T.astype(BF16)


def _ffn(x, g, w1, w3, w2, gf, *, final_norm, tm=1024, tf=512, tf_cast=512, layer=0, tile_lo=0, n_tiles=None,
         own_f32=False, carry=(), cast_tiles=(), w_in_t=None, gate0=None, gate_w=0):
    t, d = x.shape
    dff = w1.shape[-1] if w1.shape[-2] == d else w1.shape[-2]
    n_i, n_ff = t // tm, dff // tf
    n_tiles = n_i - tile_lo if n_tiles is None else n_tiles
    rb = d // n_i
    once = dict(pipeline_mode=pl.Buffered(1)) if n_tiles == 1 else {}
    tok = lambda i, j: (i + tile_lo, 0)
    in_specs = [pl.BlockSpec((tm, d), tok, **once), pl.BlockSpec((1, d), lambda i, j: (0, 0))]
    if own_f32:
        in_specs += [pl.BlockSpec((None, d, tf), lambda i, j: (layer, 0, j)),
                     pl.BlockSpec((None, d, tf), lambda i, j: (layer, 0, j)),
                     pl.BlockSpec((None, tf, d), lambda i, j: (layer, j, 0))]
    else:
        in_specs += [pl.BlockSpec((d, tf), lambda i, j: (0, j)),
                     pl.BlockSpec((d, tf), lambda i, j: (0, j)),
                     pl.BlockSpec((tf, d), lambda i, j: (j, 0))]
    in_specs.append(pl.BlockSpec((1, d), lambda i, j: (0, 0)))
    out_specs = [pl.BlockSpec((tm, d), tok, **once)]
    out_shape = [jax.ShapeDtypeStruct((t, d), F32)]
    if own_f32:
        out_specs += [pl.BlockSpec((d, tf), lambda i, j: (0, j)),
                      pl.BlockSpec((d, tf), lambda i, j: (0, j)),
                      pl.BlockSpec((tf, d), lambda i, j: (j, 0))]
        out_shape += [jax.ShapeDtypeStruct((d, dff), BF16), jax.ShapeDtypeStruct((d, dff), BF16),
                      jax.ShapeDtypeStruct((dff, d), BF16)]
    cj = lambda j: (j * tf) // tf_cast
    for w in cast_tiles:
        if w.shape[1:] == (d, dff):
            in_specs.append(pl.BlockSpec((None, rb, tf_cast), lambda i, j: (layer, i + tile_lo, cj(j))))
            out_specs.append(pl.BlockSpec((rb, tf_cast), lambda i, j: (i + tile_lo, cj(j))))
        else:
            assert w.shape[1:] == (dff, d) and rb % LANES == 0
            in_specs.append(pl.BlockSpec((None, tf_cast, rb), lambda i, j: (layer, cj(j), i + tile_lo)))
            out_specs.append(pl.BlockSpec((tf_cast, rb), lambda i, j: (cj(j), i + tile_lo)))
        out_shape.append(jax.ShapeDtypeStruct(w.shape[1:], BF16))
    tr_kj, tr_in = 0, ()
    if w_in_t is not None:
        n_main = w_in_t.shape[1] - gate_w
        sp = TR_SPLIT
        assert w_in_t.shape[2] == d and gate0 % LANES == 0 and n_main % (LANES * n_i) == 0 and d % (sp * LANES) == 0
        tr_kj = sp * n_main // (LANES * n_i)
        assert tr_kj <= n_ff
        job = lambda i, j: (i + tile_lo) * tr_kj + jnp.minimum(j, tr_kj - 1)
        in_specs += [pl.BlockSpec((None, LANES, d // sp), lambda i, j: (layer, job(i, j) // sp, job(i, j) % sp)),
                     pl.BlockSpec((None, LANES, d // sp), lambda i, j: (layer, job(i, j) // sp + 1, job(i, j) % sp))]
        out_specs.append(pl.BlockSpec((d // sp, LANES), lambda i, j: (job(i, j) % sp, job(i, j) // sp)))
        out_shape.append(jax.ShapeDtypeStruct((d, n_main), BF16))
        tr_in = (w_in_t, w_in_t)
    aliases = {}
    if carry:
        n_before = len(in_specs)
        keep = [0] + list(range(len(out_shape) - len(carry) + 1, len(out_shape)))
        assert len(keep) == len(carry) and not own_f32
        for k, (c, o) in enumerate(zip(carry, keep)):
            assert c.shape == out_shape[o].shape and c.dtype == out_shape[o].dtype
            in_specs.append(pl.BlockSpec(memory_space=pl.ANY))
            aliases[n_before + k] = o
    body = functools.partial(_ffn_body, n_ff=n_ff, final_norm=final_norm, own_f32=own_f32, n_tile=len(cast_tiles),
                             tr_kj=tr_kj, tile_lo=tile_lo, gate0=gate0, gate_w=gate_w, n_carry=len(carry))
    outs = pl.pallas_call(
        body,
        grid=(n_tiles, n_ff),
        in_specs=in_specs,
        out_specs=out_specs,
        out_shape=out_shape,
        input_output_aliases=aliases,
        scratch_shapes=[pltpu.VMEM((tm, d), BF16)],
        compiler_params=_params(("parallel", "arbitrary")),
        name=("ffn_final" if final_norm else "ffn") + ("_head" if own_f32 else ""),
    )(x, g, w1, w3, w2, gf, *cast_tiles, *tr_in, *carry)
    return outs[0], outs[1:]


def _inproj_body(*refs, n_cast):
    x_ref, g_ref, w_ref, wg_ref, bg_ref = refs[:5]
    cast_src = refs[5:5 + n_cast]
    qk_ref, v_ref, o_ref, u_ref, gates_ref = refs[5 + n_cast:10 + n_cast]
    cast_dst = refs[10 + n_cast:10 + 2 * n_cast]
    hn_ref = refs[-1]
    j = pl.program_id(1)
    nh = MLSTM_HEADS

    @pl.when(j == 0)
    def _():
        hn = _rms(x_ref[...], g_ref[...]).astype(BF16)
        hn_ref[...] = hn
        gt = lax.dot_general(wg_ref[...], hn, (((1,), (1,)), ((), ())), preferred_element_type=F32)
        gt = gt + bg_ref[...]
        for h in range(nh):
            gates_ref[h] = jnp.concatenate([gt[h:h + 1, :], gt[nh + h:nh + h + 1, :]], axis=0)

    res = jnp.dot(hn_ref[...], w_ref[...], preferred_element_type=F32)

    @pl.when(j == 0)
    def _():
        qk_ref[...] = res.astype(BF16)

    @pl.when(j == 1)
    def _():
        v_ref[...] = res.astype(BF16)

    @pl.when(j == 2)
    def _():
        o_ref[...] = res.astype(BF16)

    @pl.when(j == 3)
    def _():
        u_ref[...] = res

    for src, dst in zip(cast_src, cast_dst):
        dst[...] = src[...].astype(BF16)


def _inproj(x, g, w_main, wg_t, bg, *, tm=1024, layer=0, cast_rows=()):
    t, d = x.shape
    w = w_main.shape[1] // 4
    n_i, n_j = t // tm, 4
    blk = lambda: pl.BlockSpec((tm, w), lambda i, j: (i, 0))
    in_specs = [
        pl.BlockSpec((tm, d), lambda i, j: (i, 0)),
        pl.BlockSpec((1, d), lambda i, j: (0, 0)),
        pl.BlockSpec((d, w), lambda i, j: (0, j)),
        pl.BlockSpec((2 * MLSTM_HEADS, d), lambda i, j: (0, 0)),
        pl.BlockSpec((2 * MLSTM_HEADS, 1), lambda i, j: (0, 0)),
    ]
    out_specs = [blk(), blk(), blk(), blk(), pl.BlockSpec((MLSTM_HEADS, 2, tm), lambda i, j: (0, 0, i))]
    out_shape = [
        jax.ShapeDtypeStruct((t, w), BF16),
        jax.ShapeDtypeStruct((t, w), BF16),
        jax.ShapeDtypeStruct((t, w), BF16),
        jax.ShapeDtypeStruct((t, w), F32),
        jax.ShapeDtypeStruct((MLSTM_HEADS, 2, t), F32),
    ]
    for a in cast_rows:
        n_rows, n_cols = a.shape[1:]
        rr = n_rows // (n_i * n_j)
        assert n_rows % (n_i * n_j) == 0 and rr % 16 == 0
        in_specs.append(pl.BlockSpec((None, rr, n_cols), lambda i, j: (layer, i * n_j + j, 0)))
        out_specs.append(pl.BlockSpec((rr, n_cols), lambda i, j: (i * n_j + j, 0)))
        out_shape.append(jax.ShapeDtypeStruct((n_rows, n_cols), BF16))
    outs = pl.pallas_call(
        functools.partial(_inproj_body, n_cast=len(cast_rows)),
        grid=(n_i, n_j),
        in_specs=in_specs,
        out_specs=out_specs,
        out_shape=out_shape,
        scratch_shapes=[pltpu.VMEM((tm, d), BF16)],
        compiler_params=_params(("parallel", "arbitrary")),
        name="inproj",
    )(x, g, w_main, wg_t, bg, *cast_rows)
    return outs[:5], outs[5:]


def _split3(x):
    hi = x.astype(BF16).astype(F32)
    r1 = x - hi
    mid = r1.astype(BF16).astype(F32)
    lo = (r1 - mid).astype(BF16).astype(F32)
    return [hi, mid, lo]


def _log_sigmoid(x):
    return jnp.minimum(x, 0.0) - jnp.log(1.0 + jnp.exp(-jnp.abs(x)))


def _mlstm_body(q_ref, k_ref, v_ref, og_ref, g_ref, cwq_ref, cwk_ref, cbq_ref, cbk_ref, nw_ref,
                sel_ref, out_ref, xp_ref, qs_ref, ks_ref, cols_ref, *, seq, dk, dv):
    L = MLSTM_CHUNK
    nc = seq // L

    def conv_silu(x_ref, w_ref, b_ref, dst_ref, scale):
        xp_ref[0:SUBLANES, :] = jnp.zeros((SUBLANES, dk), F32)
        xp_ref[SUBLANES:, :] = x_ref[...].astype(F32)
        w = w_ref[...]
        acc = b_ref[...] + xp_ref[pl.ds(SUBLANES, seq), :] * w[3:4, :]
        for d in (1, 2, 3):
            acc = acc + xp_ref[pl.ds(SUBLANES - d, seq), :] * w[3 - d:4 - d, :]
        y = acc * jax.nn.sigmoid(acc)
        if scale != 1.0:
            y = y * scale
        dst_ref[...] = y.astype(BF16)

    conv_silu(q_ref, cwq_ref, cbq_ref, qs_ref, dk ** -0.5)
    conv_silu(k_ref, cwk_ref, cbk_ref, ks_ref, 1.0)

    rows = lax.broadcasted_iota(jnp.int32, (L, L), 0)
    cols = lax.broadcasted_iota(jnp.int32, (L, L), 1)
    causal = cols <= rows
    tri = jnp.where(rows <= cols, 1.0, 0.0).astype(BF16)

    ncp = -(-nc // SUBLANES) * SUBLANES
    pad = [jnp.zeros((ncp - nc, L), F32)] if ncp > nc else []
    g = g_ref[...]
    li_all = jnp.concatenate([g[0:1, c * L:(c + 1) * L] for c in range(nc)] + pad, axis=0)
    lf_all = jnp.concatenate([_log_sigmoid(g[1:2, c * L:(c + 1) * L]) for c in range(nc)] + pad, axis=0)
    parts = jnp.concatenate(_split3(lf_all), axis=0).astype(BF16)
    cs = jnp.dot(parts, tri, preferred_element_type=F32)
    b_all = cs[0:ncp] + cs[ncp:2 * ncp] + cs[2 * ncp:3 * ncp]
    d_all = li_all - b_all
    at = jnp.concatenate(_split3(b_all) + _split3(d_all), axis=0).astype(BF16)
    cols_ref[...] = lax.dot_general(at, sel_ref[...], (((0,), (0,)), ((), ())), preferred_element_type=F32)

    ext = 2 * LANES
    rep = lambda a: jnp.concatenate([a] * (dv // LANES), axis=1)
    ones_l = jnp.ones((L, ext), BF16)
    mean_w = jnp.full((dv, LANES), 1.0 / dv, BF16)
    ct_ext = jnp.zeros((dk, dv + ext), F32)
    m_prev = jnp.zeros((1, 1), F32)
    nw = nw_ref[...]

    for c in range(nc):
        r0 = c * L
        q = qs_ref[pl.ds(r0, L), :]
        k = ks_ref[pl.ds(r0, L), :]
        v_ext = jnp.concatenate([v_ref[pl.ds(r0, L), :], ones_l], axis=1)
        d_row = d_all[c:c + 1, :]
        b_end = b_all[c:c + 1, L - 1:L]
        b_c = cols_ref[:, c * LANES:(c + 1) * LANES]
        d_c = cols_ref[:, (nc + c) * LANES:(nc + c + 1) * LANES]

        cm_col = jnp.max(jnp.where(causal, d_row, -jnp.inf), axis=-1, keepdims=True)
        m_t = b_c + jnp.maximum(m_prev, cm_col)
        a_t = jnp.exp(b_c + m_prev - m_t)
        e_mt = jnp.exp(-m_t)
        arg = jnp.concatenate([b_c - m_t] * (L // LANES), axis=1) + d_row
        d_mat = jnp.where(causal, jnp.exp(arg), 0.0)

        qk = lax.dot_general(q, k, (((1,), (1,)), ((), ())), preferred_element_type=F32)
        s_b = (qk * d_mat).astype(BF16)
        sv = jnp.dot(s_b, v_ext, preferred_element_type=F32)
        qc = jnp.dot(q, ct_ext.astype(BF16), preferred_element_type=F32)
        num = sv[:, :dv] + rep(a_t) * qc[:, :dv]
        den = sv[:, dv:dv + LANES] + a_t * qc[:, dv:dv + LANES]
        inv = 1.0 / jnp.maximum(jnp.abs(den), e_mt)
        h = num * rep(inv)
        msq = jnp.dot((h * h).astype(BF16), mean_w, preferred_element_type=F32)
        hn = h * rep(lax.rsqrt(msq + EPS)) * nw
        og = og_ref[pl.ds(r0, L), :].astype(F32)
        out_ref[pl.ds(r0, L), :] = (hn * jax.nn.sigmoid(og)).astype(BF16)

        w_end = b_end + d_row
        m_loc = jnp.max(w_end, axis=-1, keepdims=True)
        e_c = jnp.exp(b_end + d_c - m_loc)
        ke = (k.astype(F32) * e_c).astype(BF16)
        c_loc = lax.dot_general(ke, v_ext, (((0,), (0,)), ((), ())), preferred_element_type=F32)
        m_new = jnp.maximum(b_end + m_prev, m_loc)
        ct_ext = jnp.exp(b_end + m_prev - m_new) * ct_ext + jnp.exp(m_loc - m_new) * c_loc
        m_prev = m_new


def _mlstm_selector(nc):
    ncp = -(-nc // SUBLANES) * SUBLANES
    sel = np.zeros((6 * ncp, 2 * nc * LANES), np.float32)
    for kind in range(2):
        for part in range(3):
            for c in range(nc):
                sel[(kind * 3 + part) * ncp + c, (kind * nc + c) * LANES:(kind * nc + c + 1) * LANES] = 1.0
    return jnp.asarray(sel, BF16)


def _mlstm(qk, v, og, gates, conv_w, conv_b, norm_w, *, batch, seq):
    t, w = qk.shape
    nh = MLSTM_HEADS
    dk = w // (2 * nh)
    dv = v.shape[1] // nh
    assert dk == LANES and dv % LANES == 0 and seq % MLSTM_CHUNK == 0 and MLSTM_CHUNK % LANES == 0
    nc = seq // MLSTM_CHUNK
    sel = _mlstm_selector(nc)
    body = functools.partial(_mlstm_body, seq=seq, dk=dk, dv=dv)
    return pl.pallas_call(
        body,
        grid=(batch, nh),
        in_specs=[
            pl.BlockSpec((seq, dk), lambda b, h: (b, h)),
            pl.BlockSpec((seq, dk), lambda b, h: (b, nh + h)),
            pl.BlockSpec((seq, dv), lambda b, h: (b, h)),
            pl.BlockSpec((seq, dv), lambda b, h: (b, h)),
            pl.BlockSpec((None, 2, seq), lambda b, h: (h, 0, b)),
            pl.BlockSpec((conv_w.shape[0], dk), lambda b, h: (0, h)),
            pl.BlockSpec((conv_w.shape[0], dk), lambda b, h: (0, nh + h)),
            pl.BlockSpec((1, dk), lambda b, h: (0, h)),
            pl.BlockSpec((1, dk), lambda b, h: (0, nh + h)),
            pl.BlockSpec((1, dv), lambda b, h: (0, h)),
            pl.BlockSpec(sel.shape, lambda b, h: (0, 0)),
        ],
        out_specs=pl.BlockSpec((seq, dv), lambda b, h: (b, h)),
        out_shape=jax.ShapeDtypeStruct((t, nh * dv), BF16),
        scratch_shapes=[
            pltpu.VMEM((seq + SUBLANES, dk), F32),
            pltpu.VMEM((seq, dk), BF16),
            pltpu.VMEM((seq, dk), BF16),
            pltpu.VMEM((MLSTM_CHUNK, 2 * nc * LANES), F32),
        ],
        compiler_params=_params(("parallel", "parallel")),
        name="mlstm",
    )(qk, qk, v, og, gates, conv_w, conv_w, conv_b, conv_b, norm_w, sel)


def _cmul(ar, ai, br, bi):
    return ar * br - ai * bi, ar * bi + ai * br


def _s5pack_body(lr_ref, li_ref, ldt_ref, br_ref, bi_ref, cr_ref, ci_ref, tile_ref,
                 bm_ref, cm_ref, dm_ref, lpr_ref, lpi_ref):
    R = S5_LIFT
    cb = S5_GROUPS_PER_BLOCK * S5_GROUP
    half = S5_GROUPS_PER_BLOCK * S5_STATE
    lr = lr_ref[...]
    li = li_ref[...]
    dt = jnp.exp(ldt_ref[...])
    mag = jnp.exp(lr * dt)
    ang = li * dt
    lbr = mag * jnp.cos(ang)
    lbi = mag * jnp.sin(ang)
    nr = lbr - 1.0
    den = lr * lr + li * li
    cr = (nr * lr + lbi * li) / den
    ci = (lbi * lr - nr * li) / den
    pr, pi_ = _cmul(cr, ci, br_ref[...], bi_ref[...])
    qr, qi = cr_ref[...], ci_ref[...]

    row = lax.broadcasted_iota(jnp.int32, (cb, half), 0)
    lane = lax.broadcasted_iota(jnp.int32, (cb, half), 1)
    same_group = (row // S5_GROUP) == (lane // S5_STATE)
    tile = tile_ref[...]

    def blockdiag(a):
        return jnp.where(same_group, jnp.dot(a.astype(BF16), tile, preferred_element_type=F32), 0.0)

    def pair(re, im):
        return jnp.concatenate([blockdiag(re), blockdiag(im)], axis=1).astype(BF16)

    c0 = pair(qr, -qi)
    lag_rows = []
    for m in range(R):
        rows_m = pair(pr, pi_)
        bm_ref[(R - 1 - m) * cb:(R - m) * cb, :] = rows_m
        lag_rows.append(rows_m)
        pr, pi_ = _cmul(pr, pi_, lbr, lbi)
        qr, qi = _cmul(qr, qi, lbr, lbi)
        cm_ref[m * cb:(m + 1) * cb, :] = pair(qr, -qi)
    dlag = [lax.dot_general(lag_rows[m], c0, (((1,), (1,)), ((), ())), preferred_element_type=F32).astype(BF16)
            for m in range(R)]
    zero = jnp.zeros((cb, cb), BF16)
    for rp in range(R):
        for r in range(R):
            dm_ref[rp * cb:(rp + 1) * cb, r * cb:(r + 1) * cb] = dlag[r - rp] if r >= rp else zero

    wr, wi = lbr, lbi
    for _ in range(R - 1):
        wr, wi = _cmul(wr, wi, lbr, lbi)
    first = same_group & ((row % S5_GROUP) == 0)

    def lane_vector(a):
        t = sum(jnp.dot(part.astype(BF16), tile, preferred_element_type=F32) for part in _split3(a))
        return jnp.sum(jnp.where(first, t, 0.0), axis=0, keepdims=True)

    lpr_ref[...] = lane_vector(wr)
    lpi_ref[...] = lane_vector(wi)


def _s5_weights(lam_re, lam_im, log_dt, b_re, b_im, c_re, c_im):
    g, p = lam_re.shape
    hch = b_re.shape[-1]
    gpb = S5_GROUPS_PER_BLOCK
    ngb = g // gpb
    R = S5_LIFT
    cb, half = gpb * hch, gpb * p
    rep = lambda a: jnp.repeat(a, hch, axis=0)
    to_rows = lambda a: jnp.transpose(a, (0, 2, 1)).reshape(g * hch, p)
    ldt = jnp.broadcast_to(log_dt[:, None], (g, p))
    tile = jnp.asarray(np.tile(np.eye(p, dtype=np.float32), (1, gpb)), BF16)
    blk = pl.BlockSpec((cb, p), lambda i: (i, 0))
    return pl.pallas_call(
        _s5pack_body,
        grid=(ngb,),
        in_specs=[blk] * 7 + [pl.BlockSpec((p, half), lambda i: (0, 0))],
        out_specs=[pl.BlockSpec((None, R * cb, 2 * half), lambda i: (i, 0, 0)),
                   pl.BlockSpec((None, R * cb, 2 * half), lambda i: (i, 0, 0)),
                   pl.BlockSpec((None, R * cb, R * cb), lambda i: (i, 0, 0)),
                   pl.BlockSpec((None, 1, half), lambda i: (i, 0, 0)),
                   pl.BlockSpec((None, 1, half), lambda i: (i, 0, 0))],
        out_shape=[jax.ShapeDtypeStruct((ngb, R * cb, 2 * half), BF16),
                   jax.ShapeDtypeStruct((ngb, R * cb, 2 * half), BF16),
                   jax.ShapeDtypeStruct((ngb, R * cb, R * cb), BF16),
                   jax.ShapeDtypeStruct((ngb, 1, half), F32),
                   jax.ShapeDtypeStruct((ngb, 1, half), F32)],
        name="s5pack",
    )(rep(lam_re), rep(lam_im), rep(ldt), to_rows(b_re), to_rows(b_im),
      c_re.reshape(g * hch, p), c_im.reshape(g * hch, p), tile)


def _s5_body(u_ref, bm_ref, cm_ref, dm_ref, lr_ref, li_ref, dsk_ref, wglu_ref, bglu_ref, o_ref,
             usb_ref, sb_ref, y_ref, st_ref, *, ts, ngb):
    R = S5_LIFT
    nk = ts // R
    rows = nk * SUBLANES
    half = S5_GROUPS_PER_BLOCK * S5_STATE
    cb = S5_GROUPS_PER_BLOCK * S5_GROUP

    @pl.when(pl.program_id(0) == 0)
    def _():
        st_ref[...] = jnp.zeros_like(st_ref)

    def srow(s):
        return (s % R) * rows + (s // R) * SUBLANES

    for s in range(ts):
        usb_ref[srow(s):srow(s) + SUBLANES, :] = u_ref[:, s, :]

    def lhs(gb):
        return jnp.concatenate([usb_ref[r * rows:(r + 1) * rows, gb * cb:(gb + 1) * cb] for r in range(R)],
                               axis=1).astype(BF16)

    def project_in(gb):
        sb_ref[gb % 2, SUBLANES:, :] = jnp.dot(lhs(gb), bm_ref[gb], preferred_element_type=F32)

    project_in(0)
    for gb in range(ngb):
        if gb + 1 < ngb:
            project_in(gb + 1)
        buf = sb_ref.at[gb % 2]
        lrb = jnp.broadcast_to(lr_ref[gb], (SUBLANES, half))
        lib = jnp.broadcast_to(li_ref[gb], (SUBLANES, half))
        xr = st_ref[gb, :, 0:half]
        xi = st_ref[gb, :, half:2 * half]
        buf[0:SUBLANES, 0:half] = xr
        buf[0:SUBLANES, half:2 * half] = xi
        for k in range(nk):
            rs = slice((k + 1) * SUBLANES, (k + 2) * SUBLANES)
            nr = lrb * xr - lib * xi + buf[rs, 0:half]
            ni = lrb * xi + lib * xr + buf[rs, half:2 * half]
            buf[rs, 0:half] = nr
            buf[rs, half:2 * half] = ni
            xr, xi = nr, ni
        st_ref[gb, :, 0:half] = xr
        st_ref[gb, :, half:2 * half] = xi

        yo = lax.dot_general(buf[0:rows, :].astype(BF16), cm_ref[gb], (((1,), (1,)), ((), ())),
                             preferred_element_type=F32)
        yd = jnp.dot(lhs(gb), dm_ref[gb], preferred_element_type=F32)
        for r in range(R):
            ug = usb_ref[r * rows:(r + 1) * rows, gb * cb:(gb + 1) * cb]
            yg = yo[:, r * cb:(r + 1) * cb] + yd[:, r * cb:(r + 1) * cb] + dsk_ref[:, gb * cb:(gb + 1) * cb] * ug
            y_ref[r * rows:(r + 1) * rows, gb * cb:(gb + 1) * cb] = jax.nn.gelu(yg)

    y = y_ref[...]
    z = jnp.dot(y.astype(BF16), wglu_ref[...], preferred_element_type=F32) + bglu_ref[...]
    usb_ref[...] = y * jax.nn.sigmoid(z)

    for s in range(ts):
        o_ref[:, s, :] = usb_ref[srow(s):srow(s) + SUBLANES, :]


def _s5(u, bm, cm, dm, lpr, lpi, dsk, wglu, bglu, *, ts=64):
    batch, seq, width = u.shape
    ngb = bm.shape[0]
    half = S5_GROUPS_PER_BLOCK * S5_STATE
    assert batch == SUBLANES and ts % S5_LIFT == 0 and seq % ts == 0
    body = functools.partial(_s5_body, ts=ts, ngb=ngb)
    full = lambda a: pl.BlockSpec(a.shape, lambda i, n=a.ndim: (0,) * n)
    return pl.pallas_call(
        body,
        grid=(seq // ts,),
        in_specs=[pl.BlockSpec((batch, ts, width), lambda i: (0, i, 0)),
                  full(bm), full(cm), full(dm), full(lpr), full(lpi), full(dsk), full(wglu), full(bglu)],
        out_specs=pl.BlockSpec((batch, ts, width), lambda i: (0, i, 0)),
        out_shape=jax.ShapeDtypeStruct((batch, seq, width), F32),
        scratch_shapes=[
            pltpu.VMEM((ts * batch, width), F32),
            pltpu.VMEM((2, (ts // S5_LIFT + 1) * batch, 2 * half), F32),
            pltpu.VMEM((ts * batch, width), F32),
            pltpu.VMEM((ngb, batch, 2 * half), F32),
        ],
        compiler_params=_params(("arbitrary",)),
        name="s5",
    )(u, bm, cm, dm, lpr, lpi, dsk, wglu, bglu)


def _outproj_body(x_ref, hm_ref, hs_ref, wa_ref, wb_ref, o_ref):
    acc = jnp.dot(hm_ref[...], wa_ref[...], preferred_element_type=F32)
    acc = acc + jnp.dot(hs_ref[...].astype(BF16), wb_ref[...], preferred_element_type=F32)
    o_ref[...] = x_ref[...] + acc


def _outproj(x, hm, hs, wo, *, tm=512):
    t, d = x.shape
    w = hm.shape[1]
    return pl.pallas_call(
        _outproj_body,
        grid=(t // tm,),
        in_specs=[
            pl.BlockSpec((tm, d), lambda i: (i, 0)),
            pl.BlockSpec((tm, w), lambda i: (i, 0)),
            pl.BlockSpec((tm, w), lambda i: (i, 0)),
            pl.BlockSpec((w, d), lambda i: (0, 0)),
            pl.BlockSpec((hs.shape[1], d), lambda i: (w // hs.shape[1], 0)),
        ],
        out_specs=pl.BlockSpec((tm, d), lambda i: (i, 0)),
        out_shape=jax.ShapeDtypeStruct((t, d), F32),
        compiler_params=_params(("parallel",)),
        name="outproj",
    )(x, hm, hs, wo, wo)


def kernel(x, ffn1_norm, ffn1_w1, ffn1_w3, ffn1_w2, mix_norm, w_in, conv_w, conv_b, b_i, b_f, mlstm_norm, lam_re, lam_im, log_dt, b_re, b_im, c_re, c_im, d_skip, w_glu, b_glu, w_out, ffn2_norm, ffn2_w1, ffn2_w3, ffn2_w2, final_norm):
    batch, seq, d = x.shape
    depth = ffn1_norm.shape[0]
    nh = MLSTM_HEADS
    qk_cols = conv_w.shape[-1]
    mw = mlstm_norm.shape[-1]
    gate0 = qk_cols + 2 * mw
    sw = w_glu.shape[-1]
    row = lambda a: a.reshape(1, -1).astype(F32)
    xt = x.reshape(batch * seq, d)
    gfin = row(final_norm)
    w_in_t = jnp.swapaxes(w_in, 1, 2)
    for l in range(depth):
        jobs = dict(layer=l, cast_tiles=(ffn2_w1, ffn2_w3, ffn2_w2), w_in_t=w_in_t, gate0=gate0, gate_w=2 * nh)
        x_head, head = _ffn(xt, row(ffn1_norm[l]), ffn1_w1, ffn1_w3, ffn1_w2, gfin, final_norm=False, tf=256,
                            n_tiles=1, own_f32=True, **jobs)
        xt, (w1b, w3b, w2b, w_main) = _ffn(xt, row(ffn1_norm[l]), head[0], head[1], head[2], gfin, final_norm=False,
                                           tile_lo=1, carry=(x_head,) + tuple(head[3:]), **jobs)

        wg_t = w_in_t[l, gate0:gate0 + 2 * nh, :].astype(BF16)
        bg = jnp.concatenate([b_i[l], b_f[l]]).reshape(2 * nh, 1).astype(F32)
        (qk, v, og, u, gates), (wo, wglu) = _inproj(xt, row(mix_norm[l]), w_main, wg_t, bg, layer=l,
                                                    cast_rows=(w_out, w_glu))

        hm = _mlstm(qk, v, og, gates, conv_w[l].astype(F32), row(conv_b[l]), row(mlstm_norm[l]),
                    batch=batch, seq=seq)

        bm, cm, dm, lpr, lpi = _s5_weights(lam_re[l], lam_im[l], log_dt[l], b_re[l], b_im[l], c_re[l], c_im[l])
        hs = _s5(u.reshape(batch, seq, sw), bm, cm, dm, lpr, lpi, row(d_skip[l]), wglu, row(b_glu[l]))

        xt = _outproj(xt, hm, hs.reshape(batch * seq, sw), wo)

        xt, _ = _ffn(xt, row(ffn2_norm[l]), w1b, w3b, w2b, gfin, final_norm=(l == depth - 1))
    return xt.reshape(batch, seq, d)
```

```python
import functools

import jax
import jax.numpy as jnp
import numpy as np
from jax import lax
from jax.experimental import pallas as pl
from jax.experimental.pallas import tpu as pltpu

F32 = jnp.float32
BF16 = jnp.bfloat16

EPS = 1e-6
MLSTM_HEADS = 4
S5_GROUP = 16
S5_STATE = 64
S5_GROUPS_PER_BLOCK = 8
S5_LIFT = 2
LANES = 128
SUBLANES = 8
MLSTM_CHUNK = 256
TR_SPLIT = 4
VMEM_LIMIT_BYTES = 58 * 1024 * 1024


def _rms(x, g):
    return x * lax.rsqrt(jnp.mean(x * x, axis=-1, keepdims=True) + EPS) * g


def _params(sem):
    return pltpu.CompilerParams(dimension_semantics=sem, vmem_limit_bytes=VMEM_LIMIT_BYTES)


def _ffn_body(*refs, n_ff, final_norm, own_f32, n_tile, tr_kj, tile_lo, gate0, gate_w, n_carry, fuse_prologue):
    x_ref, g_ref, w1_ref, w3_ref, w2_ref, gf_ref = refs[:6]
    n_tr = 2 if tr_kj else 0
    n_in = 6 + n_tile + n_tr + n_carry
    tile_src = refs[6:6 + n_tile]
    o_ref = refs[n_in]
    n_own = 3 if own_f32 else 0
    own_dst = refs[n_in + 1:n_in + 1 + n_own]
    tile_dst = refs[n_in + 1 + n_own:n_in + 1 + n_own + n_tile]
    xn_ref = refs[-1]
    j = pl.program_id(1)

    def swiglu_step(xn):
        w1, w3, w2 = w1_ref[...], w3_ref[...], w2_ref[...]
        if own_f32:
            w1, w3, w2 = w1.astype(BF16), w3.astype(BF16), w2.astype(BF16)
            own_dst[0][...] = w1
            own_dst[1][...] = w3
            own_dst[2][...] = w2
        a = jnp.dot(xn, w1, preferred_element_type=F32)
        b = jnp.dot(xn, w3, preferred_element_type=F32)
        h = (a * jax.nn.sigmoid(a)) * (b * 0.5)
        return jnp.dot(h.astype(BF16), w2, preferred_element_type=F32)

    if fuse_prologue:
        @pl.when(j == 0)
        def _():
            x = x_ref[...]
            xn = _rms(x, g_ref[...]).astype(BF16)
            xn_ref[...] = xn
            o_ref[...] = x + swiglu_step(xn)

        @pl.when(j > 0)
        def _():
            o_ref[...] += swiglu_step(xn_ref[...])
    else:
        @pl.when(j == 0)
        def _():
            x = x_ref[...]
            xn_ref[...] = _rms(x, g_ref[...]).astype(BF16)
            o_ref[...] = x

        o_ref[...] += swiglu_step(xn_ref[...])

    if final_norm:
        @pl.when(j == n_ff - 1)
        def _():
            o_ref[...] = _rms(o_ref[...], gf_ref[...])

    for src, dst in zip(tile_src, tile_dst):
        dst[...] = src[...].astype(BF16)

    if tr_kj:
        ta_ref, tb_ref = refs[6 + n_tile], refs[7 + n_tile]
        main_ref = refs[-2]
        m = ((pl.program_id(0) + tile_lo) * tr_kj + j) // TR_SPLIT

        @pl.when((j < tr_kj) & (m < gate0 // LANES))
        def _():
            main_ref[...] = ta_ref[...].T.astype(BF16)

        @pl.when((j < tr_kj) & (m >= gate0 // LANES))
        def _():
            shifted = jnp.concatenate([ta_ref[gate_w:, :], tb_ref[:gate_w, :]], axis=0)
            main_ref[...] = shifted.T.astype(BF16)


def _ffn(x, g, w1, w3, w2, gf, *, final_norm, tm=1024, tf=512, tf_cast=512, layer=0, tile_lo=0, n_tiles=None,
         own_f32=False, carry=(), cast_tiles=(), w_in_t=None, gate0=None, gate_w=0):
    t, d = x.shape
    dff = w1.shape[-1] if w1.shape[-2] == d else w1.shape[-2]
    n_i, n_ff = t // tm, dff // tf
    n_tiles = n_i - tile_lo if n_tiles is None else n_tiles
    rb = d // n_i
    once = dict(pipeline_mode=pl.Buffered(1)) if n_tiles == 1 else {}
    tok = lambda i, j: (i + tile_lo, 0)
    in_specs = [pl.BlockSpec((tm, d), tok, **once), pl.BlockSpec((1, d), lambda i, j: (0, 0))]
    if own_f32:
        in_specs += [pl.BlockSpec((None, d, tf), lambda i, j: (layer, 0, j)),
                     pl.BlockSpec((None, d, tf), lambda i, j: (layer, 0, j)),
                     pl.BlockSpec((None, tf, d), lambda i, j: (layer, j, 0))]
    else:
        in_specs += [pl.BlockSpec((d, tf), lambda i, j: (0, j)),
                     pl.BlockSpec((d, tf), lambda i, j: (0, j)),
                     pl.BlockSpec((tf, d), lambda i, j: (j, 0))]
    in_specs.append(pl.BlockSpec((1, d), lambda i, j: (0, 0)))
    out_specs = [pl.BlockSpec((tm, d), tok, **once)]
    out_shape = [jax.ShapeDtypeStruct((t, d), F32)]
    if own_f32:
        out_specs += [pl.BlockSpec((d, tf), lambda i, j: (0, j)),
                      pl.BlockSpec((d, tf), lambda i, j: (0, j)),
                      pl.BlockSpec((tf, d), lambda i, j: (j, 0))]
        out_shape += [jax.ShapeDtypeStruct((d, dff), BF16), jax.ShapeDtypeStruct((d, dff), BF16),
                      jax.ShapeDtypeStruct((dff, d), BF16)]
    cj = lambda j: (j * tf) // tf_cast
    for w in cast_tiles:
        if w.shape[1:] == (d, dff):
            in_specs.append(pl.BlockSpec((None, rb, tf_cast), lambda i, j: (layer, i + tile_lo, cj(j))))
            out_specs.append(pl.BlockSpec((rb, tf_cast), lambda i, j: (i + tile_lo, cj(j))))
        else:
            assert w.shape[1:] == (dff, d) and rb % LANES == 0
            in_specs.append(pl.BlockSpec((None, tf_cast, rb), lambda i, j: (layer, cj(j), i + tile_lo)))
            out_specs.append(pl.BlockSpec((tf_cast, rb), lambda i, j: (cj(j), i + tile_lo)))
        out_shape.append(jax.ShapeDtypeStruct(w.shape[1:], BF16))
    tr_kj, tr_in = 0, ()
    if w_in_t is not None:
        n_main = w_in_t.shape[1] - gate_w
        sp = TR_SPLIT
        assert w_in_t.shape[2] == d and gate0 % LANES == 0 and n_main % (LANES * n_i) == 0 and d % (sp * LANES) == 0
        tr_kj = sp * n_main // (LANES * n_i)
        assert tr_kj <= n_ff
        job = lambda i, j: (i + tile_lo) * tr_kj + jnp.minimum(j, tr_kj - 1)
        in_specs += [pl.BlockSpec((None, LANES, d // sp), lambda i, j: (layer, job(i, j) // sp, job(i, j) % sp)),
                     pl.BlockSpec((None, LANES, d // sp), lambda i, j: (layer, job(i, j) // sp + 1, job(i, j) % sp))]
        out_specs.append(pl.BlockSpec((d // sp, LANES), lambda i, j: (job(i, j) % sp, job(i, j) // sp)))
        out_shape.append(jax.ShapeDtypeStruct((d, n_main), BF16))
        tr_in = (w_in_t, w_in_t)
    aliases = {}
    if carry:
        n_before = len(in_specs)
        keep = [0] + list(range(len(out_shape) - len(carry) + 1, len(out_shape)))
        assert len(keep) == len(carry) and not own_f32
        for k, (c, o) in enumerate(zip(carry, keep)):
            assert c.shape == out_shape[o].shape and c.dtype == out_shape[o].dtype
            in_specs.append(pl.BlockSpec(memory_space=pl.ANY))
            aliases[n_before + k] = o
    body = functools.partial(_ffn_body, n_ff=n_ff, final_norm=final_norm, own_f32=own_f32, n_tile=len(cast_tiles),
                             tr_kj=tr_kj, tile_lo=tile_lo, gate0=gate0, gate_w=gate_w, n_carry=len(carry),
                             fuse_prologue=not (cast_tiles or tr_kj))
    outs = pl.pallas_call(
        body,
        grid=(n_tiles, n_ff),
        in_specs=in_specs,
        out_specs=out_specs,
        out_shape=out_shape,
        input_output_aliases=aliases,
        scratch_shapes=[pltpu.VMEM((tm, d), BF16)],
        compiler_params=_params(("parallel", "arbitrary")),
        name=("ffn_final" if final_norm else "ffn") + ("_head" if own_f32 else ""),
    )(x, g, w1, w3, w2, gf, *cast_tiles, *tr_in, *carry)
    return outs[0], outs[1:]


def _inproj_body(*refs, n_cast):
    x_ref, g_ref, w_ref, wg_ref, bg_ref = refs[:5]
    cast_src = refs[5:5 + n_cast]
    qkvo_ref, u_ref, gates_ref = refs[5 + n_cast:8 + n_cast]
    cast_dst = refs[8 + n_cast:8 + 2 * n_cast]
    hn_ref = refs[-1]
    j = pl.program_id(1)
    nh = MLSTM_HEADS

    @pl.when(j == 0)
    def _():
        hn = _rms(x_ref[...], g_ref[...]).astype(BF16)
        hn_ref[...] = hn
        gt = lax.dot_general(wg_ref[...], hn, (((1,), (1,)), ((), ())), preferred_element_type=F32)
        gt = gt + bg_ref[...]
        for h in range(nh):
            gates_ref[h] = jnp.concatenate([gt[h:h + 1, :], gt[nh + h:nh + h + 1, :]], axis=0)

    @pl.when(j < 3)
    def _():
        qkvo_ref[...] = jnp.dot(hn_ref[...], w_ref[...], preferred_element_type=F32).astype(BF16)

    @pl.when(j == 3)
    def _():
        u_ref[...] = jnp.dot(hn_ref[...], w_ref[...], preferred_element_type=F32)

    for src, dst in zip(cast_src, cast_dst):
        dst[...] = src[...].astype(BF16)


def _inproj(x, g, w_main, wg_t, bg, *, tm=1024, layer=0, cast_rows=()):
    t, d = x.shape
    w = w_main.shape[1] // 4
    n_i, n_j = t // tm, 4
    in_specs = [
        pl.BlockSpec((tm, d), lambda i, j: (i, 0)),
        pl.BlockSpec((1, d), lambda i, j: (0, 0)),
        pl.BlockSpec((d, w), lambda i, j: (0, j)),
        pl.BlockSpec((2 * MLSTM_HEADS, d), lambda i, j: (0, 0)),
        pl.BlockSpec((2 * MLSTM_HEADS, 1), lambda i, j: (0, 0)),
    ]
    out_specs = [pl.BlockSpec((tm, w), lambda i, j: (i, jnp.minimum(j, 2))),
                 pl.BlockSpec((tm, w), lambda i, j: (i, 0)),
                 pl.BlockSpec((MLSTM_HEADS, 2, tm), lambda i, j: (0, 0, i))]
    out_shape = [
        jax.ShapeDtypeStruct((t, 3 * w), BF16),
        jax.ShapeDtypeStruct((t, w), F32),
        jax.ShapeDtypeStruct((MLSTM_HEADS, 2, t), F32),
    ]
    for a in cast_rows:
        n_rows, n_cols = a.shape[1:]
        rr = n_rows // (n_i * n_j)
        assert n_rows % (n_i * n_j) == 0 and rr % 16 == 0
        in_specs.append(pl.BlockSpec((None, rr, n_cols), lambda i, j: (layer, i * n_j + j, 0)))
        out_specs.append(pl.BlockSpec((rr, n_cols), lambda i, j: (i * n_j + j, 0)))
        out_shape.append(jax.ShapeDtypeStruct((n_rows, n_cols), BF16))
    outs = pl.pallas_call(
        functools.partial(_inproj_body, n_cast=len(cast_rows)),
        grid=(n_i, n_j),
        in_specs=in_specs,
        out_specs=out_specs,
        out_shape=out_shape,
        scratch_shapes=[pltpu.VMEM((tm, d), BF16)],
        compiler_params=_params(("parallel", "arbitrary")),
        name="inproj",
    )(x, g, w_main, wg_t, bg, *cast_rows)
    return outs[:3], outs[3:]


def _split3(x):
    hi = x.astype(BF16).astype(F32)
    r1 = x - hi
    mid = r1.astype(BF16).astype(F32)
    lo = (r1 - mid).astype(BF16).astype(F32)
    return [hi, mid, lo]


def _log_sigmoid(x):
    return jnp.minimum(x, 0.0) - jnp.log(1.0 + jnp.exp(-jnp.abs(x)))


def _mlstm_body(q_ref, k_ref, v_ref, og_ref, g_ref, cwq_ref, cwk_ref, cbq_ref, cbk_ref, nw_ref,
                sel_ref, out_ref, xp_ref, qs_ref, ks_ref, cols_ref, *, seq, dk, dv):
    L = MLSTM_CHUNK
    nc = seq // L

    def conv_silu(x_ref, w_ref, b_ref, dst_ref, scale):
        xp_ref[0:SUBLANES, :] = jnp.zeros((SUBLANES, dk), F32)
        xp_ref[SUBLANES:, :] = x_ref[...].astype(F32)
        w = w_ref[...]
        acc = b_ref[...] + xp_ref[pl.ds(SUBLANES, seq), :] * w[3:4, :]
        for d in (1, 2, 3):
            acc = acc + xp_ref[pl.ds(SUBLANES - d, seq), :] * w[3 - d:4 - d, :]
        y = acc * jax.nn.sigmoid(acc)
        if scale != 1.0:
            y = y * scale
        dst_ref[...] = y.astype(BF16)

    conv_silu(q_ref, cwq_ref, cbq_ref, qs_ref, dk ** -0.5)
    conv_silu(k_ref, cwk_ref, cbk_ref, ks_ref, 1.0)

    rows = lax.broadcasted_iota(jnp.int32, (L, L), 0)
    cols = lax.broadcasted_iota(jnp.int32, (L, L), 1)
    causal = cols <= rows
    tri = jnp.where(rows <= cols, 1.0, 0.0).astype(BF16)

    ncp = -(-nc // SUBLANES) * SUBLANES
    pad = [jnp.zeros((ncp - nc, L), F32)] if ncp > nc else []
    g = g_ref[...]
    li_all = jnp.concatenate([g[0:1, c * L:(c + 1) * L] for c in range(nc)] + pad, axis=0)
    lf_all = jnp.concatenate([_log_sigmoid(g[1:2, c * L:(c + 1) * L]) for c in range(nc)] + pad, axis=0)
    parts = jnp.concatenate(_split3(lf_all), axis=0).astype(BF16)
    cs = jnp.dot(parts, tri, preferred_element_type=F32)
    b_all = cs[0:ncp] + cs[ncp:2 * ncp] + cs[2 * ncp:3 * ncp]
    d_all = li_all - b_all
    at = jnp.concatenate(_split3(b_all) + _split3(d_all), axis=0).astype(BF16)
    cols_ref[...] = lax.dot_general(at, sel_ref[...], (((0,), (0,)), ((), ())), preferred_element_type=F32)

    ext = 2 * LANES
    rep = lambda a: jnp.concatenate([a] * (dv // LANES), axis=1)
    ones_l = jnp.ones((L, ext), BF16)
    mean_w = jnp.full((dv, LANES), 1.0 / dv, BF16)
    ct_ext = jnp.zeros((dk, dv + ext), F32)
    m_prev = jnp.zeros((1, 1), F32)
    nw = nw_ref[...]

    for c in range(nc):
        r0 = c * L
        q = qs_ref[pl.ds(r0, L), :]
        k = ks_ref[pl.ds(r0, L), :]
        v_ext = jnp.concatenate([v_ref[pl.ds(r0, L), :], ones_l], axis=1)
        d_row = d_all[c:c + 1, :]
        b_end = b_all[c:c + 1, L - 1:L]
        b_c = cols_ref[:, c * LANES:(c + 1) * LANES]
        d_c = cols_ref[:, (nc + c) * LANES:(nc + c + 1) * LANES]

        cm_col = jnp.max(jnp.where(causal, d_row, -jnp.inf), axis=-1, keepdims=True)
        m_t = b_c + jnp.maximum(m_prev, cm_col)
        a_t = jnp.exp(b_c + m_prev - m_t)
        e_mt = jnp.exp(-m_t)
        arg = jnp.concatenate([b_c - m_t] * (L // LANES), axis=1) + d_row
        d_mat = jnp.where(causal, jnp.exp(arg), 0.0)

        qk = lax.dot_general(q, k, (((1,), (1,)), ((), ())), preferred_element_type=F32)
        s_b = (qk * d_mat).astype(BF16)
        sv = jnp.dot(s_b, v_ext, preferred_element_type=F32)
        qc = jnp.dot(q, ct_ext.astype(BF16), preferred_element_type=F32)
        num = sv[:, :dv] + rep(a_t) * qc[:, :dv]
        den = sv[:, dv:dv + LANES] + a_t * qc[:, dv:dv + LANES]
        inv = 1.0 / jnp.maximum(jnp.abs(den), e_mt)
        h = num * rep(inv)
        msq = jnp.dot((h * h).astype(BF16), mean_w, preferred_element_type=F32)
        hn = h * rep(lax.rsqrt(msq + EPS)) * nw
        og = og_ref[pl.ds(r0, L), :].astype(F32)
        out_ref[pl.ds(r0, L), :] = (hn * jax.nn.sigmoid(og)).astype(BF16)

        w_end = b_end + d_row
        m_loc = jnp.max(w_end, axis=-1, keepdims=True)
        e_c = jnp.exp(b_end + d_c - m_loc)
        ke = (k.astype(F32) * e_c).astype(BF16)
        c_loc = lax.dot_general(ke, v_ext, (((0,), (0,)), ((), ())), preferred_element_type=F32)
        m_new = jnp.maximum(b_end + m_prev, m_loc)
        ct_ext = jnp.exp(b_end + m_prev - m_new) * ct_ext + jnp.exp(m_loc - m_new) * c_loc
        m_prev = m_new


def _mlstm_selector(nc):
    ncp = -(-nc // SUBLANES) * SUBLANES
    sel = np.zeros((6 * ncp, 2 * nc * LANES), np.float32)
    for kind in range(2):
        for part in range(3):
            for c in range(nc):
                sel[(kind * 3 + part) * ncp + c, (kind * nc + c) * LANES:(kind * nc + c + 1) * LANES] = 1.0
    return jnp.asarray(sel, BF16)


def _mlstm(qkvo, gates, conv_w, conv_b, norm_w, *, batch, seq):
    t = qkvo.shape[0]
    w = qkvo.shape[1] // 3
    nh = MLSTM_HEADS
    dk = w // (2 * nh)
    dv = w // nh
    assert dk == LANES and dv % LANES == 0 and seq % MLSTM_CHUNK == 0 and MLSTM_CHUNK % LANES == 0
    nc = seq // MLSTM_CHUNK
    sel = _mlstm_selector(nc)
    body = functools.partial(_mlstm_body, seq=seq, dk=dk, dv=dv)
    return pl.pallas_call(
        body,
        grid=(batch, nh),
        in_specs=[
            pl.BlockSpec((seq, dk), lambda b, h: (b, h)),
            pl.BlockSpec((seq, dk), lambda b, h: (b, nh + h)),
            pl.BlockSpec((seq, dv), lambda b, h: (b, nh + h)),
            pl.BlockSpec((seq, dv), lambda b, h: (b, 2 * nh + h)),
            pl.BlockSpec((None, 2, seq), lambda b, h: (h, 0, b)),
            pl.BlockSpec((conv_w.shape[0], dk), lambda b, h: (0, h)),
            pl.BlockSpec((conv_w.shape[0], dk), lambda b, h: (0, nh + h)),
            pl.BlockSpec((1, dk), lambda b, h: (0, h)),
            pl.BlockSpec((1, dk), lambda b, h: (0, nh + h)),
            pl.BlockSpec((1, dv), lambda b, h: (0, h)),
            pl.BlockSpec(sel.shape, lambda b, h: (0, 0)),
        ],
        out_specs=pl.BlockSpec((seq, dv), lambda b, h: (b, h)),
        out_shape=jax.ShapeDtypeStruct((t, nh * dv), BF16),
        scratch_shapes=[
            pltpu.VMEM((seq + SUBLANES, dk), F32),
            pltpu.VMEM((seq, dk), BF16),
            pltpu.VMEM((seq, dk), BF16),
            pltpu.VMEM((MLSTM_CHUNK, 2 * nc * LANES), F32),
        ],
        compiler_params=_params(("parallel", "parallel")),
        name="mlstm",
    )(qkvo, qkvo, qkvo, qkvo, gates, conv_w, conv_w, conv_b, conv_b, norm_w, sel)


def _cmul(ar, ai, br, bi):
    return ar * br - ai * bi, ar * bi + ai * br


def _s5pack_body(lr_ref, li_ref, ldt_ref, br_ref, bi_ref, cr_ref, ci_ref, tile_ref,
                 bm_ref, cm_ref, dm_ref, lpr_ref, lpi_ref):
    R = S5_LIFT
    cb = S5_GROUPS_PER_BLOCK * S5_GROUP
    half = S5_GROUPS_PER_BLOCK * S5_STATE
    lr = lr_ref[...]
    li = li_ref[...]
    dt = jnp.exp(ldt_ref[...])
    mag = jnp.exp(lr * dt)
    ang = li * dt
    lbr = mag * jnp.cos(ang)
    lbi = mag * jnp.sin(ang)
    nr = lbr - 1.0
    den = lr * lr + li * li
    cr = (nr * lr + lbi * li) / den
    ci = (lbi * lr - nr * li) / den
    pr, pi_ = _cmul(cr, ci, br_ref[...], bi_ref[...])
    qr, qi = cr_ref[...], ci_ref[...]

    row = lax.broadcasted_iota(jnp.int32, (cb, half), 0)
    lane = lax.broadcasted_iota(jnp.int32, (cb, half), 1)
    same_group = (row // S5_GROUP) == (lane // S5_STATE)
    tile = tile_ref[...]

    def blockdiag(a):
        return jnp.where(same_group, jnp.dot(a.astype(BF16), tile, preferred_element_type=F32), 0.0)

    def pair(re, im):
        return jnp.concatenate([blockdiag(re), blockdiag(im)], axis=1).astype(BF16)

    c0 = pair(qr, -qi)
    lag_rows = []
    for m in range(R):
        rows_m = pair(pr, pi_)
        bm_ref[(R - 1 - m) * cb:(R - m) * cb, :] = rows_m
        lag_rows.append(rows_m)
        pr, pi_ = _cmul(pr, pi_, lbr, lbi)
        qr, qi = _cmul(qr, qi, lbr, lbi)
        cm_ref[m * cb:(m + 1) * cb, :] = pair(qr, -qi)
    dlag = [lax.dot_general(lag_rows[m], c0, (((1,), (1,)), ((), ())), preferred_element_type=F32).astype(BF16)
            for m in range(R)]
    zero = jnp.zeros((cb, cb), BF16)
    for rp in range(R):
        for r in range(R):
            dm_ref[rp * cb:(rp + 1) * cb, r * cb:(r + 1) * cb] = dlag[r - rp] if r >= rp else zero

    wr, wi = lbr, lbi
    for _ in range(R - 1):
        wr, wi = _cmul(wr, wi, lbr, lbi)
    first = same_group & ((row % S5_GROUP) == 0)

    def lane_vector(a):
        t = sum(jnp.dot(part.astype(BF16), tile, preferred_element_type=F32) for part in _split3(a))
        return jnp.sum(jnp.where(first, t, 0.0), axis=0, keepdims=True)

    lpr_ref[...] = lane_vector(wr)
    lpi_ref[...] = lane_vector(wi)


def _s5_weights(lam_re, lam_im, log_dt, b_re, b_im, c_re, c_im):
    g, p = lam_re.shape
    hch = b_re.shape[-1]
    gpb = S5_GROUPS_PER_BLOCK
    ngb = g // gpb
    R = S5_LIFT
    cb, half = gpb * hch, gpb * p
    rep = lambda a: jnp.repeat(a, hch, axis=0)
    to_rows = lambda a: jnp.transpose(a, (0, 2, 1)).reshape(g * hch, p)
    ldt = jnp.broadcast_to(log_dt[:, None], (g, p))
    tile = jnp.asarray(np.tile(np.eye(p, dtype=np.float32), (1, gpb)), BF16)
    blk = pl.BlockSpec((cb, p), lambda i: (i, 0))
    return pl.pallas_call(
        _s5pack_body,
        grid=(ngb,),
        in_specs=[blk] * 7 + [pl.BlockSpec((p, half), lambda i: (0, 0))],
        out_specs=[pl.BlockSpec((None, R * cb, 2 * half), lambda i: (i, 0, 0)),
                   pl.BlockSpec((None, R * cb, 2 * half), lambda i: (i, 0, 0)),
                   pl.BlockSpec((None, R * cb, R * cb), lambda i: (i, 0, 0)),
                   pl.BlockSpec((None, 1, half), lambda i: (i, 0, 0)),
                   pl.BlockSpec((None, 1, half), lambda i: (i, 0, 0))],
        out_shape=[jax.ShapeDtypeStruct((ngb, R * cb, 2 * half), BF16),
                   jax.ShapeDtypeStruct((ngb, R * cb, 2 * half), BF16),
                   jax.ShapeDtypeStruct((ngb, R * cb, R * cb), BF16),
                   jax.ShapeDtypeStruct((ngb, 1, half), F32),
                   jax.ShapeDtypeStruct((ngb, 1, half), F32)],
        name="s5pack",
    )(rep(lam_re), rep(lam_im), rep(ldt), to_rows(b_re), to_rows(b_im),
      c_re.reshape(g * hch, p), c_im.reshape(g * hch, p), tile)


def _s5_body(u_ref, bm_ref, cm_ref, dm_ref, lr_ref, li_ref, dsk_ref, wglu_ref, bglu_ref, o_ref,
             usb_ref, sb_ref, y_ref, st_ref, *, ts, ngb):
    R = S5_LIFT
    nk = ts // R
    rows = nk * SUBLANES
    half = S5_GROUPS_PER_BLOCK * S5_STATE
    cb = S5_GROUPS_PER_BLOCK * S5_GROUP

    @pl.when(pl.program_id(0) == 0)
    def _():
        st_ref[...] = jnp.zeros_like(st_ref)

    def srow(s):
        return (s % R) * rows + (s // R) * SUBLANES

    for s in range(ts):
        usb_ref[srow(s):srow(s) + SUBLANES, :] = u_ref[:, s, :]

    def lhs(gb):
        return jnp.concatenate([usb_ref[r * rows:(r + 1) * rows, gb * cb:(gb + 1) * cb] for r in range(R)],
                               axis=1).astype(BF16)

    def project_in(gb):
        sb_ref[gb % 2, SUBLANES:, :] = jnp.dot(lhs(gb), bm_ref[gb], preferred_element_type=F32)

    project_in(0)
    for gb in range(ngb):
        if gb + 1 < ngb:
            project_in(gb + 1)
        buf = sb_ref.at[gb % 2]
        lrb = jnp.broadcast_to(lr_ref[gb], (SUBLANES, half))
        lib = jnp.broadcast_to(li_ref[gb], (SUBLANES, half))
        xr = st_ref[gb, :, 0:half]
        xi = st_ref[gb, :, half:2 * half]
        buf[0:SUBLANES, 0:half] = xr
        buf[0:SUBLANES, half:2 * half] = xi
        for k in range(nk):
            rs = slice((k + 1) * SUBLANES, (k + 2) * SUBLANES)
            nr = lrb * xr - lib * xi + buf[rs, 0:half]
            ni = lrb * xi + lib * xr + buf[rs, half:2 * half]
            buf[rs, 0:half] = nr
            buf[rs, half:2 * half] = ni
            xr, xi = nr, ni
        st_ref[gb, :, 0:half] = xr
        st_ref[gb, :, half:2 * half] = xi

        yo = lax.dot_general(buf[0:rows, :].astype(BF16), cm_ref[gb], (((1,), (1,)), ((), ())),
                             preferred_element_type=F32)
        yd = jnp.dot(lhs(gb), dm_ref[gb], preferred_element_type=F32)
        for r in range(R):
            ug = usb_ref[r * rows:(r + 1) * rows, gb * cb:(gb + 1) * cb]
            yg = yo[:, r * cb:(r + 1) * cb] + yd[:, r * cb:(r + 1) * cb] + dsk_ref[:, gb * cb:(gb + 1) * cb] * ug
            y_ref[r * rows:(r + 1) * rows, gb * cb:(gb + 1) * cb] = jax.nn.gelu(yg)

    y = y_ref[...]
    z = jnp.dot(y.astype(BF16), wglu_ref[...], preferred_element_type=F32) + bglu_ref[...]
    usb_ref[...] = y * jax.nn.sigmoid(z)

    for s in range(ts):
        o_ref[:, s, :] = usb_ref[srow(s):srow(s) + SUBLANES, :]


def _s5(u, bm, cm, dm, lpr, lpi, dsk, wglu, bglu, *, ts=64):
    batch, seq, width = u.shape
    ngb = bm.shape[0]
    half = S5_GROUPS_PER_BLOCK * S5_STATE
    assert batch == SUBLANES and ts % S5_LIFT == 0 and seq % ts == 0
    body = functools.partial(_s5_body, ts=ts, ngb=ngb)
    full = lambda a: pl.BlockSpec(a.shape, lambda i, n=a.ndim: (0,) * n)
    return pl.pallas_call(
        body,
        grid=(seq // ts,),
        in_specs=[pl.BlockSpec((batch, ts, width), lambda i: (0, i, 0)),
                  full(bm), full(cm), full(dm), full(lpr), full(lpi), full(dsk), full(wglu), full(bglu)],
        out_specs=pl.BlockSpec((batch, ts, width), lambda i: (0, i, 0)),
        out_shape=jax.ShapeDtypeStruct((batch, seq, width), F32),
        scratch_shapes=[
            pltpu.VMEM((ts * batch, width), F32),
            pltpu.VMEM((2, (ts // S5_LIFT + 1) * batch, 2 * half), F32),
            pltpu.VMEM((ts * batch, width), F32),
            pltpu.VMEM((ngb, batch, 2 * half), F32),
        ],
        compiler_params=_params(("arbitrary",)),
        name="s5",
    )(u, bm, cm, dm, lpr, lpi, dsk, wglu, bglu)


def _outproj_body(x_ref, hm_ref, hs_ref, wa_ref, wb_ref, o_ref):
    acc = jnp.dot(hm_ref[...], wa_ref[...], preferred_element_type=F32)
    acc = acc + jnp.dot(hs_ref[...].astype(BF16), wb_ref[...], preferred_element_type=F32)
    o_ref[...] = x_ref[...] + acc


def _outproj(x, hm, hs, wo, *, tm=512):
    t, d = x.shape
    w = hm.shape[1]
    return pl.pallas_call(
        _outproj_body,
        grid=(t // tm,),
        in_specs=[
            pl.BlockSpec((tm, d), lambda i: (i, 0)),
            pl.BlockSpec((tm, w), lambda i: (i, 0)),
            pl.BlockSpec((tm, w), lambda i: (i, 0)),
            pl.BlockSpec((w, d), lambda i: (0, 0)),
            pl.BlockSpec((hs.shape[1], d), lambda i: (w // hs.shape[1], 0)),
        ],
        out_specs=pl.BlockSpec((tm, d), lambda i: (i, 0)),
        out_shape=jax.ShapeDtypeStruct((t, d), F32),
        compiler_params=_params(("parallel",)),
        name="outproj",
    )(x, hm, hs, wo, wo)


def kernel(x, ffn1_norm, ffn1_w1, ffn1_w3, ffn1_w2, mix_norm, w_in, conv_w, conv_b, b_i, b_f, mlstm_norm, lam_re, lam_im, log_dt, b_re, b_im, c_re, c_im, d_skip, w_glu, b_glu, w_out, ffn2_norm, ffn2_w1, ffn2_w3, ffn2_w2, final_norm):
    batch, seq, d = x.shape
    depth = ffn1_norm.shape[0]
    nh = MLSTM_HEADS
    qk_cols = conv_w.shape[-1]
    mw = mlstm_norm.shape[-1]
    gate0 = qk_cols + 2 * mw
    sw = w_glu.shape[-1]
    row = lambda a: a.reshape(1, -1).astype(F32)
    xt = x.reshape(batch * seq, d)
    gfin = row(final_norm)
    w_in_t = jnp.swapaxes(w_in, 1, 2)
    for l in range(depth):
        jobs = dict(layer=l, cast_tiles=(ffn2_w1, ffn2_w3, ffn2_w2), w_in_t=w_in_t, gate0=gate0, gate_w=2 * nh)
        x_head, head = _ffn(xt, row(ffn1_norm[l]), ffn1_w1, ffn1_w3, ffn1_w2, gfin, final_norm=False, tf=256,
                            n_tiles=1, own_f32=True, **jobs)
        xt, (w1b, w3b, w2b, w_main) = _ffn(xt, row(ffn1_norm[l]), head[0], head[1], head[2], gfin, final_norm=False,
                                           tile_lo=1, carry=(x_head,) + tuple(head[3:]), **jobs)

        wg_t = w_in_t[l, gate0:gate0 + 2 * nh, :].astype(BF16)
        bg = jnp.concatenate([b_i[l], b_f[l]]).reshape(2 * nh, 1).astype(F32)
        (qkvo, u, gates), (wo, wglu) = _inproj(xt, row(mix_norm[l]), w_main, wg_t, bg, layer=l,
                                               cast_rows=(w_out, w_glu))

        hm = _mlstm(qkvo, gates, conv_w[l].astype(F32), row(conv_b[l]), row(mlstm_norm[l]), batch=batch, seq=seq)

        bm, cm, dm, lpr, lpi = _s5_weights(lam_re[l], lam_im[l], log_dt[l], b_re[l], b_im[l], c_re[l], c_im[l])
        hs = _s5(u.reshape(batch, seq, sw), bm, cm, dm, lpr, lpi, row(d_skip[l]), wglu, row(b_glu[l]))

        xt = _outproj(xt, hm, hs.reshape(batch * seq, sw), wo)

        xt, _ = _ffn(xt, row(ffn2_norm[l]), w1b, w3b, w2b, gfin, final_norm=(l == depth - 1))
    return xt.reshape(batch, seq, d)
```

```python
import functools

import jax
import jax.numpy as jnp
import numpy as np
from jax import lax
from jax.experimental import pallas as pl
from jax.experimental.pallas import tpu as pltpu

F32 = jnp.float32
BF16 = jnp.bfloat16

EPS = 1e-6
MLSTM_HEADS = 4
S5_GROUP = 16
S5_STATE = 64
S5_GROUPS_PER_BLOCK = 8
S5_LIFT = 2
LANES = 128
SUBLANES = 8
MLSTM_CHUNK = 256
TR_SPLIT = 4
VMEM_LIMIT_BYTES = 58 * 1024 * 1024


def _rms(x, g):
    return x * lax.rsqrt(jnp.mean(x * x, axis=-1, keepdims=True) + EPS) * g


def _params(sem):
    return pltpu.CompilerParams(dimension_semantics=sem, vmem_limit_bytes=VMEM_LIMIT_BYTES)


def _ffn_body(*refs, n_ff, final_norm, own_f32, n_tile, tr_kj, tile_lo, gate0, gate_w, n_carry, fuse_prologue):
    x_ref, g_ref, w1_ref, w3_ref, w2_ref, gf_ref = refs[:6]
    n_tr = 2 if tr_kj else 0
    n_in = 6 + n_tile + n_tr + n_carry
    tile_src = refs[6:6 + n_tile]
    o_ref = refs[n_in]
    n_own = 3 if own_f32 else 0
    own_dst = refs[n_in + 1:n_in + 1 + n_own]
    tile_dst = refs[n_in + 1 + n_own:n_in + 1 + n_own + n_tile]
    xn_ref = refs[-1]
    j = pl.program_id(1)

    def swiglu_step(xn):
        w1, w3, w2 = w1_ref[...], w3_ref[...], w2_ref[...]
        if own_f32:
            w1, w3, w2 = w1.astype(BF16), w3.astype(BF16), w2.astype(BF16)
            own_dst[0][...] = w1
            own_dst[1][...] = w3
            own_dst[2][...] = w2
        a = jnp.dot(xn, w1, preferred_element_type=F32)
        b = jnp.dot(xn, w3, preferred_element_type=F32)
        h = (a * jax.nn.sigmoid(a)) * (b * 0.5)
        return jnp.dot(h.astype(BF16), w2, preferred_element_type=F32)

    if fuse_prologue:
        @pl.when(j == 0)
        def _():
            x = x_ref[...]
            xn = _rms(x, g_ref[...]).astype(BF16)
            xn_ref[...] = xn
            o_ref[...] = x + swiglu_step(xn)

        @pl.when(j > 0)
        def _():
            o_ref[...] += swiglu_step(xn_ref[...])
    else:
        @pl.when(j == 0)
        def _():
            x = x_ref[...]
            xn_ref[...] = _rms(x, g_ref[...]).astype(BF16)
            o_ref[...] = x

        o_ref[...] += swiglu_step(xn_ref[...])

    if final_norm:
        @pl.when(j == n_ff - 1)
        def _():
            o_ref[...] = _rms(o_ref[...], gf_ref[...])

    for src, dst in zip(tile_src, tile_dst):
        dst[...] = src[...].astype(BF16)

    if tr_kj:
        ta_ref, tb_ref = refs[6 + n_tile], refs[7 + n_tile]
        main_ref = refs[-2]
        m = ((pl.program_id(0) + tile_lo) * tr_kj + j) // TR_SPLIT

        @pl.when((j < tr_kj) & (m < gate0 // LANES))
        def _():
            main_ref[...] = ta_ref[...].T.astype(BF16)

        @pl.when((j < tr_kj) & (m >= gate0 // LANES))
        def _():
            shifted = jnp.concatenate([ta_ref[gate_w:, :], tb_ref[:gate_w, :]], axis=0)
            main_ref[...] = shifted.T.astype(BF16)


def _ffn(x, g, w1, w3, w2, gf, *, final_norm, tm=1024, tf=512, tf_cast=512, layer=0, tile_lo=0, n_tiles=None,
         own_f32=False, carry=(), cast_tiles=(), w_in_t=None, gate0=None, gate_w=0):
    t, d = x.shape
    dff = w1.shape[-1] if w1.shape[-2] == d else w1.shape[-2]
    n_i, n_ff = t // tm, dff // tf
    n_tiles = n_i - tile_lo if n_tiles is None else n_tiles
    rb = d // n_i
    once = dict(pipeline_mode=pl.Buffered(1)) if n_tiles == 1 else {}
    tok = lambda i, j: (i + tile_lo, 0)
    in_specs = [pl.BlockSpec((tm, d), tok, **once), pl.BlockSpec((1, d), lambda i, j: (0, 0))]
    if own_f32:
        in_specs += [pl.BlockSpec((None, d, tf), lambda i, j: (layer, 0, j)),
                     pl.BlockSpec((None, d, tf), lambda i, j: (layer, 0, j)),
                     pl.BlockSpec((None, tf, d), lambda i, j: (layer, j, 0))]
    else:
        in_specs += [pl.BlockSpec((d, tf), lambda i, j: (0, j)),
                     pl.BlockSpec((d, tf), lambda i, j: (0, j)),
                     pl.BlockSpec((tf, d), lambda i, j: (j, 0))]
    in_specs.append(pl.BlockSpec((1, d), lambda i, j: (0, 0)))
    out_specs = [pl.BlockSpec((tm, d), tok, **once)]
    out_shape = [jax.ShapeDtypeStruct((t, d), F32)]
    if own_f32:
        out_specs += [pl.BlockSpec((d, tf), lambda i, j: (0, j)),
                      pl.BlockSpec((d, tf), lambda i, j: (0, j)),
                      pl.BlockSpec((tf, d), lambda i, j: (j, 0))]
        out_shape += [jax.ShapeDtypeStruct((d, dff), BF16), jax.ShapeDtypeStruct((d, dff), BF16),
                      jax.ShapeDtypeStruct((dff, d), BF16)]
    cj = lambda j: (j * tf) // tf_cast
    for w in cast_tiles:
        if w.shape[1:] == (d, dff):
            in_specs.append(pl.BlockSpec((None, rb, tf_cast), lambda i, j: (layer, i + tile_lo, cj(j))))
            out_specs.append(pl.BlockSpec((rb, tf_cast), lambda i, j: (i + tile_lo, cj(j))))
        else:
            assert w.shape[1:] == (dff, d) and rb % LANES == 0
            in_specs.append(pl.BlockSpec((None, tf_cast, rb), lambda i, j: (layer, cj(j), i + tile_lo)))
            out_specs.append(pl.BlockSpec((tf_cast, rb), lambda i, j: (cj(j), i + tile_lo)))
        out_shape.append(jax.ShapeDtypeStruct(w.shape[1:], BF16))
    tr_kj, tr_in = 0, ()
    if w_in_t is not None:
        n_main = w_in_t.shape[1] - gate_w
        sp = TR_SPLIT
        assert w_in_t.shape[2] == d and gate0 % LANES == 0 and n_main % (LANES * n_i) == 0 and d % (sp * LANES) == 0
        tr_kj = sp * n_main // (LANES * n_i)
        assert tr_kj <= n_ff
        job = lambda i, j: (i + tile_lo) * tr_kj + jnp.minimum(j, tr_kj - 1)
        in_specs += [pl.BlockSpec((None, LANES, d // sp), lambda i, j: (layer, job(i, j) // sp, job(i, j) % sp)),
                     pl.BlockSpec((None, LANES, d // sp), lambda i, j: (layer, job(i, j) // sp + 1, job(i, j) % sp))]
        out_specs.append(pl.BlockSpec((d // sp, LANES), lambda i, j: (job(i, j) % sp, job(i, j) // sp)))
        out_shape.append(jax.ShapeDtypeStruct((d, n_main), BF16))
        tr_in = (w_in_t, w_in_t)
    aliases = {}
    if carry:
        n_before = len(in_specs)
        keep = [0] + list(range(len(out_shape) - len(carry) + 1, len(out_shape)))
        assert len(keep) == len(carry) and not own_f32
        for k, (c, o) in enumerate(zip(carry, keep)):
            assert c.shape == out_shape[o].shape and c.dtype == out_shape[o].dtype
            in_specs.append(pl.BlockSpec(memory_space=pl.ANY))
            aliases[n_before + k] = o
    body = functools.partial(_ffn_body, n_ff=n_ff, final_norm=final_norm, own_f32=own_f32, n_tile=len(cast_tiles),
                             tr_kj=tr_kj, tile_lo=tile_lo, gate0=gate0, gate_w=gate_w, n_carry=len(carry),
                             fuse_prologue=not (cast_tiles or tr_kj))
    outs = pl.pallas_call(
        body,
        grid=(n_tiles, n_ff),
        in_specs=in_specs,
        out_specs=out_specs,
        out_shape=out_shape,
        input_output_aliases=aliases,
        scratch_shapes=[pltpu.VMEM((tm, d), BF16)],
        compiler_params=_params(("parallel", "arbitrary")),
        name=("ffn_final" if final_norm else "ffn") + ("_head" if own_f32 else ""),
    )(x, g, w1, w3, w2, gf, *cast_tiles, *tr_in, *carry)
    return outs[0], outs[1:]


def _inproj_body(*refs, n_cast):
    x_ref, g_ref, w_ref, wg_ref, bg_ref = refs[:5]
    cast_src = refs[5:5 + n_cast]
    q_ref, k_ref, v_ref, o_ref, u_ref, gates_ref = refs[5 + n_cast:11 + n_cast]
    cast_dst = refs[11 + n_cast:11 + 2 * n_cast]
    hn_ref = refs[-1]
    j = pl.program_id(1)
    nh = MLSTM_HEADS

    @pl.when(j == 0)
    def _():
        hn = _rms(x_ref[...], g_ref[...]).astype(BF16)
        hn_ref[...] = hn
        gt = lax.dot_general(wg_ref[...], hn, (((1,), (1,)), ((), ())), preferred_element_type=F32)
        gt = gt + bg_ref[...]
        for h in range(nh):
            gates_ref[h] = jnp.concatenate([gt[h:h + 1, :], gt[nh + h:nh + h + 1, :]], axis=0)

    def project():
        return jnp.dot(hn_ref[...], w_ref[...], preferred_element_type=F32)

    def heads(dst_ref, res, col0):
        wd = dst_ref.shape[-1]
        for h in range(nh):
            dst_ref[h] = res[:, col0 + h * wd:col0 + (h + 1) * wd].astype(BF16)

    @pl.when(j == 0)
    def _():
        res = project()
        heads(q_ref, res, 0)
        heads(k_ref, res, res.shape[1] // 2)

    @pl.when(j == 1)
    def _():
        heads(v_ref, project(), 0)

    @pl.when(j == 2)
    def _():
        heads(o_ref, project(), 0)

    @pl.when(j == 3)
    def _():
        u_ref[...] = project()

    for src, dst in zip(cast_src, cast_dst):
        dst[...] = src[...].astype(BF16)


def _inproj(x, g, w_main, wg_t, bg, *, tm=1024, layer=0, cast_rows=()):
    t, d = x.shape
    w = w_main.shape[1] // 4
    n_i, n_j = t // tm, 4
    in_specs = [
        pl.BlockSpec((tm, d), lambda i, j: (i, 0)),
        pl.BlockSpec((1, d), lambda i, j: (0, 0)),
        pl.BlockSpec((d, w), lambda i, j: (0, j)),
        pl.BlockSpec((2 * MLSTM_HEADS, d), lambda i, j: (0, 0)),
        pl.BlockSpec((2 * MLSTM_HEADS, 1), lambda i, j: (0, 0)),
    ]
    nh = MLSTM_HEADS
    dk, dv = w // (2 * nh), w // nh
    hm = lambda wd: pl.BlockSpec((nh, tm, wd), lambda i, j: (0, i, 0))
    out_specs = [hm(dk), hm(dk), hm(dv), hm(dv),
                 pl.BlockSpec((tm, w), lambda i, j: (i, 0)),
                 pl.BlockSpec((nh, 2, tm), lambda i, j: (0, 0, i))]
    out_shape = [
        jax.ShapeDtypeStruct((nh, t, dk), BF16),
        jax.ShapeDtypeStruct((nh, t, dk), BF16),
        jax.ShapeDtypeStruct((nh, t, dv), BF16),
        jax.ShapeDtypeStruct((nh, t, dv), BF16),
        jax.ShapeDtypeStruct((t, w), F32),
        jax.ShapeDtypeStruct((nh, 2, t), F32),
    ]
    for a in cast_rows:
        n_rows, n_cols = a.shape[1:]
        rr = n_rows // (n_i * n_j)
        assert n_rows % (n_i * n_j) == 0 and rr % 16 == 0
        in_specs.append(pl.BlockSpec((None, rr, n_cols), lambda i, j: (layer, i * n_j + j, 0)))
        out_specs.append(pl.BlockSpec((rr, n_cols), lambda i, j: (i * n_j + j, 0)))
        out_shape.append(jax.ShapeDtypeStruct((n_rows, n_cols), BF16))
    outs = pl.pallas_call(
        functools.partial(_inproj_body, n_cast=len(cast_rows)),
        grid=(n_i, n_j),
        in_specs=in_specs,
        out_specs=out_specs,
        out_shape=out_shape,
        scratch_shapes=[pltpu.VMEM((tm, d), BF16)],
        compiler_params=_params(("parallel", "arbitrary")),
        name="inproj",
    )(x, g, w_main, wg_t, bg, *cast_rows)
    return outs[:6], outs[6:]


def _split3(x):
    hi = x.astype(BF16).astype(F32)
    r1 = x - hi
    mid = r1.astype(BF16).astype(F32)
    lo = (r1 - mid).astype(BF16).astype(F32)
    return [hi, mid, lo]


def _log_sigmoid(x):
    return jnp.minimum(x, 0.0) - jnp.log(1.0 + jnp.exp(-jnp.abs(x)))


def _mlstm_body(q_ref, k_ref, v_ref, og_ref, g_ref, cwq_ref, cwk_ref, cbq_ref, cbk_ref, nw_ref,
                sel_ref, out_ref, xp_ref, qs_ref, ks_ref, cols_ref, *, seq, dk, dv):
    L = MLSTM_CHUNK
    nc = seq // L

    def conv_silu(x_ref, w_ref, b_ref, dst_ref, scale):
        xp_ref[0:SUBLANES, :] = jnp.zeros((SUBLANES, dk), F32)
        xp_ref[SUBLANES:, :] = x_ref[...].astype(F32)
        w = w_ref[...]
        acc = b_ref[...] + xp_ref[pl.ds(SUBLANES, seq), :] * w[3:4, :]
        for d in (1, 2, 3):
            acc = acc + xp_ref[pl.ds(SUBLANES - d, seq), :] * w[3 - d:4 - d, :]
        y = acc * jax.nn.sigmoid(acc)
        if scale != 1.0:
            y = y * scale
        dst_ref[...] = y.astype(BF16)

    conv_silu(q_ref, cwq_ref, cbq_ref, qs_ref, dk ** -0.5)
    conv_silu(k_ref, cwk_ref, cbk_ref, ks_ref, 1.0)

    rows = lax.broadcasted_iota(jnp.int32, (L, L), 0)
    cols = lax.broadcasted_iota(jnp.int32, (L, L), 1)
    causal = cols <= rows
    tri = jnp.where(rows <= cols, 1.0, 0.0).astype(BF16)

    ncp = -(-nc // SUBLANES) * SUBLANES
    pad = [jnp.zeros((ncp - nc, L), F32)] if ncp > nc else []
    g = g_ref[...]
    li_all = jnp.concatenate([g[0:1, c * L:(c + 1) * L] for c in range(nc)] + pad, axis=0)
    lf_all = jnp.concatenate([_log_sigmoid(g[1:2, c * L:(c + 1) * L]) for c in range(nc)] + pad, axis=0)
    parts = jnp.concatenate(_split3(lf_all), axis=0).astype(BF16)
    cs = jnp.dot(parts, tri, preferred_element_type=F32)
    b_all = cs[0:ncp] + cs[ncp:2 * ncp] + cs[2 * ncp:3 * ncp]
    d_all = li_all - b_all
    at = jnp.concatenate(_split3(b_all) + _split3(d_all), axis=0).astype(BF16)
    cols_ref[...] = lax.dot_general(at, sel_ref[...], (((0,), (0,)), ((), ())), preferred_element_type=F32)

    ext = 2 * LANES
    rep = lambda a: jnp.concatenate([a] * (dv // LANES), axis=1)
    ones_l = jnp.ones((L, ext), BF16)
    mean_w = jnp.full((dv, LANES), 1.0 / dv, BF16)
    ct_ext = jnp.zeros((dk, dv + ext), F32)
    m_prev = jnp.zeros((1, 1), F32)
    nw = nw_ref[...]

    for c in range(nc):
        r0 = c * L
        q = qs_ref[pl.ds(r0, L), :]
        k = ks_ref[pl.ds(r0, L), :]
        v_ext = jnp.concatenate([v_ref[pl.ds(r0, L), :], ones_l], axis=1)
        d_row = d_all[c:c + 1, :]
        b_end = b_all[c:c + 1, L - 1:L]
        b_c = cols_ref[:, c * LANES:(c + 1) * LANES]
        d_c = cols_ref[:, (nc + c) * LANES:(nc + c + 1) * LANES]

        cm_col = jnp.max(jnp.where(causal, d_row, -jnp.inf), axis=-1, keepdims=True)
        m_t = b_c + jnp.maximum(m_prev, cm_col)
        a_t = jnp.exp(b_c + m_prev - m_t)
        e_mt = jnp.exp(-m_t)
        arg = jnp.concatenate([b_c - m_t] * (L // LANES), axis=1) + d_row
        d_mat = jnp.where(causal, jnp.exp(arg), 0.0)

        qk = lax.dot_general(q, k, (((1,), (1,)), ((), ())), preferred_element_type=F32)
        s_b = (qk * d_mat).astype(BF16)
        sv = jnp.dot(s_b, v_ext, preferred_element_type=F32)
        qc = jnp.dot(q, ct_ext.astype(BF16), preferred_element_type=F32)
        num = sv[:, :dv] + rep(a_t) * qc[:, :dv]
        den = sv[:, dv:dv + LANES] + a_t * qc[:, dv:dv + LANES]
        inv = 1.0 / jnp.maximum(jnp.abs(den), e_mt)
        h = num * rep(inv)
        msq = jnp.dot((h * h).astype(BF16), mean_w, preferred_element_type=F32)
        hn = h * rep(lax.rsqrt(msq + EPS)) * nw
        og = og_ref[pl.ds(r0, L), :].astype(F32)
        out_ref[pl.ds(r0, L), :] = (hn * jax.nn.sigmoid(og)).astype(BF16)

        w_end = b_end + d_row
        m_loc = jnp.max(w_end, axis=-1, keepdims=True)
        e_c = jnp.exp(b_end + d_c - m_loc)
        ke = (k.astype(F32) * e_c).astype(BF16)
        c_loc = lax.dot_general(ke, v_ext, (((0,), (0,)), ((), ())), preferred_element_type=F32)
        m_new = jnp.maximum(b_end + m_prev, m_loc)
        ct_ext = jnp.exp(b_end + m_prev - m_new) * ct_ext + jnp.exp(m_loc - m_new) * c_loc
        m_prev = m_new


def _mlstm_selector(nc):
    ncp = -(-nc // SUBLANES) * SUBLANES
    sel = np.zeros((6 * ncp, 2 * nc * LANES), np.float32)
    for kind in range(2):
        for part in range(3):
            for c in range(nc):
                sel[(kind * 3 + part) * ncp + c, (kind * nc + c) * LANES:(kind * nc + c + 1) * LANES] = 1.0
    return jnp.asarray(sel, BF16)


def _mlstm(q, k, v, og, gates, conv_w, conv_b, norm_w, *, batch, seq):
    nh, t, dk = q.shape
    dv = v.shape[2]
    assert dk == LANES and dv % LANES == 0 and seq % MLSTM_CHUNK == 0 and MLSTM_CHUNK % LANES == 0
    nc = seq // MLSTM_CHUNK
    sel = _mlstm_selector(nc)
    body = functools.partial(_mlstm_body, seq=seq, dk=dk, dv=dv)
    return pl.pallas_call(
        body,
        grid=(batch, nh),
        in_specs=[
            pl.BlockSpec((None, seq, dk), lambda b, h: (h, b, 0)),
            pl.BlockSpec((None, seq, dk), lambda b, h: (h, b, 0)),
            pl.BlockSpec((None, seq, dv), lambda b, h: (h, b, 0)),
            pl.BlockSpec((None, seq, dv), lambda b, h: (h, b, 0)),
            pl.BlockSpec((None, 2, seq), lambda b, h: (h, 0, b)),
            pl.BlockSpec((conv_w.shape[0], dk), lambda b, h: (0, h)),
            pl.BlockSpec((conv_w.shape[0], dk), lambda b, h: (0, nh + h)),
            pl.BlockSpec((1, dk), lambda b, h: (0, h)),
            pl.BlockSpec((1, dk), lambda b, h: (0, nh + h)),
            pl.BlockSpec((1, dv), lambda b, h: (0, h)),
            pl.BlockSpec(sel.shape, lambda b, h: (0, 0)),
        ],
        out_specs=pl.BlockSpec((None, seq, dv), lambda b, h: (h, b, 0)),
        out_shape=jax.ShapeDtypeStruct((nh, t, dv), BF16),
        scratch_shapes=[
            pltpu.VMEM((seq + SUBLANES, dk), F32),
            pltpu.VMEM((seq, dk), BF16),
            pltpu.VMEM((seq, dk), BF16),
            pltpu.VMEM((MLSTM_CHUNK, 2 * nc * LANES), F32),
        ],
        compiler_params=_params(("parallel", "parallel")),
        name="mlstm",
    )(q, k, v, og, gates, conv_w, conv_w, conv_b, conv_b, norm_w, sel)


def _cmul(ar, ai, br, bi):
    return ar * br - ai * bi, ar * bi + ai * br


def _s5pack_body(lr_ref, li_ref, ldt_ref, br_ref, bi_ref, cr_ref, ci_ref, tile_ref,
                 bm_ref, cm_ref, dm_ref, lpr_ref, lpi_ref):
    R = S5_LIFT
    cb = S5_GROUPS_PER_BLOCK * S5_GROUP
    half = S5_GROUPS_PER_BLOCK * S5_STATE
    lr = lr_ref[...]
    li = li_ref[...]
    dt = jnp.exp(ldt_ref[...])
    mag = jnp.exp(lr * dt)
    ang = li * dt
    lbr = mag * jnp.cos(ang)
    lbi = mag * jnp.sin(ang)
    nr = lbr - 1.0
    den = lr * lr + li * li
    cr = (nr * lr + lbi * li) / den
    ci = (lbi * lr - nr * li) / den
    pr, pi_ = _cmul(cr, ci, br_ref[...], bi_ref[...])
    qr, qi = cr_ref[...], ci_ref[...]

    row = lax.broadcasted_iota(jnp.int32, (cb, half), 0)
    lane = lax.broadcasted_iota(jnp.int32, (cb, half), 1)
    same_group = (row // S5_GROUP) == (lane // S5_STATE)
    tile = tile_ref[...]

    def blockdiag(a):
        return jnp.where(same_group, jnp.dot(a.astype(BF16), tile, preferred_element_type=F32), 0.0)

    def pair(re, im):
        return jnp.concatenate([blockdiag(re), blockdiag(im)], axis=1).astype(BF16)

    c0 = pair(qr, -qi)
    lag_rows = []
    for m in range(R):
        rows_m = pair(pr, pi_)
        bm_ref[(R - 1 - m) * cb:(R - m) * cb, :] = rows_m
        lag_rows.append(rows_m)
        pr, pi_ = _cmul(pr, pi_, lbr, lbi)
        qr, qi = _cmul(qr, qi, lbr, lbi)
        cm_ref[m * cb:(m + 1) * cb, :] = pair(qr, -qi)
    dlag = [lax.dot_general(lag_rows[m], c0, (((1,), (1,)), ((), ())), preferred_element_type=F32).astype(BF16)
            for m in range(R)]
    zero = jnp.zeros((cb, cb), BF16)
    for rp in range(R):
        for r in range(R):
            dm_ref[rp * cb:(rp + 1) * cb, r * cb:(r + 1) * cb] = dlag[r - rp] if r >= rp else zero

    wr, wi = lbr, lbi
    for _ in range(R - 1):
        wr, wi = _cmul(wr, wi, lbr, lbi)
    first = same_group & ((row % S5_GROUP) == 0)

    def lane_vector(a):
        t = sum(jnp.dot(part.astype(BF16), tile, preferred_element_type=F32) for part in _split3(a))
        return jnp.sum(jnp.where(first, t, 0.0), axis=0, keepdims=True)

    lpr_ref[...] = lane_vector(wr)
    lpi_ref[...] = lane_vector(wi)


def _s5_weights(lam_re, lam_im, log_dt, b_re, b_im, c_re, c_im):
    g, p = lam_re.shape
    hch = b_re.shape[-1]
    gpb = S5_GROUPS_PER_BLOCK
    ngb = g // gpb
    R = S5_LIFT
    cb, half = gpb * hch, gpb * p
    rep = lambda a: jnp.repeat(a, hch, axis=0)
    to_rows = lambda a: jnp.transpose(a, (0, 2, 1)).reshape(g * hch, p)
    ldt = jnp.broadcast_to(log_dt[:, None], (g, p))
    tile = jnp.asarray(np.tile(np.eye(p, dtype=np.float32), (1, gpb)), BF16)
    blk = pl.BlockSpec((cb, p), lambda i: (i, 0))
    return pl.pallas_call(
        _s5pack_body,
        grid=(ngb,),
        in_specs=[blk] * 7 + [pl.BlockSpec((p, half), lambda i: (0, 0))],
        out_specs=[pl.BlockSpec((None, R * cb, 2 * half), lambda i: (i, 0, 0)),
                   pl.BlockSpec((None, R * cb, 2 * half), lambda i: (i, 0, 0)),
                   pl.BlockSpec((None, R * cb, R * cb), lambda i: (i, 0, 0)),
                   pl.BlockSpec((None, 1, half), lambda i: (i, 0, 0)),
                   pl.BlockSpec((None, 1, half), lambda i: (i, 0, 0))],
        out_shape=[jax.ShapeDtypeStruct((ngb, R * cb, 2 * half), BF16),
                   jax.ShapeDtypeStruct((ngb, R * cb, 2 * half), BF16),
                   jax.ShapeDtypeStruct((ngb, R * cb, R * cb), BF16),
                   jax.ShapeDtypeStruct((ngb, 1, half), F32),
                   jax.ShapeDtypeStruct((ngb, 1, half), F32)],
        name="s5pack",
    )(rep(lam_re), rep(lam_im), rep(ldt), to_rows(b_re), to_rows(b_im),
      c_re.reshape(g * hch, p), c_im.reshape(g * hch, p), tile)


def _s5_body(u_ref, bm_ref, cm_ref, dm_ref, lr_ref, li_ref, dsk_ref, wglu_ref, bglu_ref, o_ref,
             usb_ref, sb_ref, y_ref, st_ref, *, ts, ngb):
    R = S5_LIFT
    nk = ts // R
    rows = nk * SUBLANES
    half = S5_GROUPS_PER_BLOCK * S5_STATE
    cb = S5_GROUPS_PER_BLOCK * S5_GROUP

    @pl.when(pl.program_id(0) == 0)
    def _():
        st_ref[...] = jnp.zeros_like(st_ref)

    def srow(s):
        return (s % R) * rows + (s // R) * SUBLANES

    for s in range(ts):
        usb_ref[srow(s):srow(s) + SUBLANES, :] = u_ref[:, s, :]

    def lhs(gb):
        return jnp.concatenate([usb_ref[r * rows:(r + 1) * rows, gb * cb:(gb + 1) * cb] for r in range(R)],
                               axis=1).astype(BF16)

    def project_in(gb):
        sb_ref[gb % 2, SUBLANES:, :] = jnp.dot(lhs(gb), bm_ref[gb], preferred_element_type=F32)

    project_in(0)
    for gb in range(ngb):
        if gb + 1 < ngb:
            project_in(gb + 1)
        buf = sb_ref.at[gb % 2]
        lrb = jnp.broadcast_to(lr_ref[gb], (SUBLANES, half))
        lib = jnp.broadcast_to(li_ref[gb], (SUBLANES, half))
        xr = st_ref[gb, :, 0:half]
        xi = st_ref[gb, :, half:2 * half]
        buf[0:SUBLANES, 0:half] = xr
        buf[0:SUBLANES, half:2 * half] = xi
        for k in range(nk):
            rs = slice((k + 1) * SUBLANES, (k + 2) * SUBLANES)
            nr = lrb * xr - lib * xi + buf[rs, 0:half]
            ni = lrb * xi + lib * xr + buf[rs, half:2 * half]
            buf[rs, 0:half] = nr
            buf[rs, half:2 * half] = ni
            xr, xi = nr, ni
        st_ref[gb, :, 0:half] = xr
        st_ref[gb, :, half:2 * half] = xi

        yo = lax.dot_general(buf[0:rows, :].astype(BF16), cm_ref[gb], (((1,), (1,)), ((), ())),
                             preferred_element_type=F32)
        yd = jnp.dot(lhs(gb), dm_ref[gb], preferred_element_type=F32)
        for r in range(R):
            ug = usb_ref[r * rows:(r + 1) * rows, gb * cb:(gb + 1) * cb]
            yg = yo[:, r * cb:(r + 1) * cb] + yd[:, r * cb:(r + 1) * cb] + dsk_ref[:, gb * cb:(gb + 1) * cb] * ug
            y_ref[r * rows:(r + 1) * rows, gb * cb:(gb + 1) * cb] = jax.nn.gelu(yg)

    y = y_ref[...]
    z = jnp.dot(y.astype(BF16), wglu_ref[...], preferred_element_type=F32) + bglu_ref[...]
    usb_ref[...] = y * jax.nn.sigmoid(z)

    for s in range(ts):
        o_ref[:, s, :] = usb_ref[srow(s):srow(s) + SUBLANES, :]


def _s5(u, bm, cm, dm, lpr, lpi, dsk, wglu, bglu, *, ts=64):
    batch, seq, width = u.shape
    ngb = bm.shape[0]
    half = S5_GROUPS_PER_BLOCK * S5_STATE
    assert batch == SUBLANES and ts % S5_LIFT == 0 and seq % ts == 0
    body = functools.partial(_s5_body, ts=ts, ngb=ngb)
    full = lambda a: pl.BlockSpec(a.shape, lambda i, n=a.ndim: (0,) * n)
    return pl.pallas_call(
        body,
        grid=(seq // ts,),
        in_specs=[pl.BlockSpec((batch, ts, width), lambda i: (0, i, 0)),
                  full(bm), full(cm), full(dm), full(lpr), full(lpi), full(dsk), full(wglu), full(bglu)],
        out_specs=pl.BlockSpec((batch, ts, width), lambda i: (0, i, 0)),
        out_shape=jax.ShapeDtypeStruct((batch, seq, width), F32),
        scratch_shapes=[
            pltpu.VMEM((ts * batch, width), F32),
            pltpu.VMEM((2, (ts // S5_LIFT + 1) * batch, 2 * half), F32),
            pltpu.VMEM((ts * batch, width), F32),
            pltpu.VMEM((ngb, batch, 2 * half), F32),
        ],
        compiler_params=_params(("arbitrary",)),
        name="s5",
    )(u, bm, cm, dm, lpr, lpi, dsk, wglu, bglu)


def _outproj_body(x_ref, hm_ref, hs_ref, wa_ref, wb_ref, o_ref):
    hm = jnp.concatenate([hm_ref[h] for h in range(hm_ref.shape[0])], axis=1)
    acc = jnp.dot(hm, wa_ref[...], preferred_element_type=F32)
    acc = acc + jnp.dot(hs_ref[...].astype(BF16), wb_ref[...], preferred_element_type=F32)
    o_ref[...] = x_ref[...] + acc


def _outproj(x, hm, hs, wo, *, tm=512):
    t, d = x.shape
    w = hm.shape[0] * hm.shape[2]
    return pl.pallas_call(
        _outproj_body,
        grid=(t // tm,),
        in_specs=[
            pl.BlockSpec((tm, d), lambda i: (i, 0)),
            pl.BlockSpec((hm.shape[0], tm, hm.shape[2]), lambda i: (0, i, 0)),
            pl.BlockSpec((tm, hs.shape[1]), lambda i: (i, 0)),
            pl.BlockSpec((w, d), lambda i: (0, 0)),
            pl.BlockSpec((hs.shape[1], d), lambda i: (w // hs.shape[1], 0)),
        ],
        out_specs=pl.BlockSpec((tm, d), lambda i: (i, 0)),
        out_shape=jax.ShapeDtypeStruct((t, d), F32),
        compiler_params=_params(("parallel",)),
        name="outproj",
    )(x, hm, hs, wo, wo)


def kernel(x, ffn1_norm, ffn1_w1, ffn1_w3, ffn1_w2, mix_norm, w_in, conv_w, conv_b, b_i, b_f, mlstm_norm, lam_re, lam_im, log_dt, b_re, b_im, c_re, c_im, d_skip, w_glu, b_glu, w_out, ffn2_norm, ffn2_w1, ffn2_w3, ffn2_w2, final_norm):
    batch, seq, d = x.shape
    depth = ffn1_norm.shape[0]
    nh = MLSTM_HEADS
    qk_cols = conv_w.shape[-1]
    mw = mlstm_norm.shape[-1]
    gate0 = qk_cols + 2 * mw
    sw = w_glu.shape[-1]
    row = lambda a: a.reshape(1, -1).astype(F32)
    xt = x.reshape(batch * seq, d)
    gfin = row(final_norm)
    w_in_t = jnp.swapaxes(w_in, 1, 2)
    for l in range(depth):
        jobs = dict(layer=l, cast_tiles=(ffn2_w1, ffn2_w3, ffn2_w2), w_in_t=w_in_t, gate0=gate0, gate_w=2 * nh)
        x_head, head = _ffn(xt, row(ffn1_norm[l]), ffn1_w1, ffn1_w3, ffn1_w2, gfin, final_norm=False, tf=256,
                            n_tiles=1, own_f32=True, **jobs)
        xt, (w1b, w3b, w2b, w_main) = _ffn(xt, row(ffn1_norm[l]), head[0], head[1], head[2], gfin, final_norm=False,
                                           tile_lo=1, carry=(x_head,) + tuple(head[3:]), **jobs)

        wg_t = w_in_t[l, gate0:gate0 + 2 * nh, :].astype(BF16)
        bg = jnp.concatenate([b_i[l], b_f[l]]).reshape(2 * nh, 1).astype(F32)
        (q, k, v, og, u, gates), (wo, wglu) = _inproj(xt, row(mix_norm[l]), w_main, wg_t, bg, layer=l,
                                                      cast_rows=(w_out, w_glu))

        hm = _mlstm(q, k, v, og, gates, conv_w[l].astype(F32), row(conv_b[l]), row(mlstm_norm[l]),
                    batch=batch, seq=seq)

        bm, cm, dm, lpr, lpi = _s5_weights(lam_re[l], lam_im[l], log_dt[l], b_re[l], b_im[l], c_re[l], c_im[l])
        hs = _s5(u.reshape(batch, seq, sw), bm, cm, dm, lpr, lpi, row(d_skip[l]), wglu, row(b_glu[l]))

        xt = _outproj(xt, hm, hs.reshape(batch * seq, sw), wo)

        xt, _ = _ffn(xt, row(ffn2_norm[l]), w1b, w3b, w2b, gfin, final_norm=(l == depth - 1))
    return xt.reshape(batch, seq, d)
```

```python
import functools

import jax
import jax.numpy as jnp
import numpy as np
from jax import lax
from jax.experimental import pallas as pl
from jax.experimental.pallas import tpu as pltpu

F32 = jnp.float32
BF16 = jnp.bfloat16

EPS = 1e-6
MLSTM_HEADS = 4
S5_GROUP = 16
S5_STATE = 64
S5_GROUPS_PER_BLOCK = 8
S5_LIFT = 2
LANES = 128
SUBLANES = 8
MLSTM_CHUNK = 256
TR_SPLIT = 4
VMEM_LIMIT_BYTES = 58 * 1024 * 1024


def _rms(x, g):
    return x * lax.rsqrt(jnp.mean(x * x, axis=-1, keepdims=True) + EPS) * g


def _params(sem):
    return pltpu.CompilerParams(dimension_semantics=sem, vmem_limit_bytes=VMEM_LIMIT_BYTES)


def _ffn_body(*refs, n_ff, final_norm, own_f32, n_tile, tr_kj, tile_lo, gate0, gate_w, n_carry, fuse_prologue):
    x_ref, g_ref, w1_ref, w3_ref, w2_ref, gf_ref = refs[:6]
    n_tr = 2 if tr_kj else 0
    n_in = 6 + n_tile + n_tr + n_carry
    tile_src = refs[6:6 + n_tile]
    o_ref = refs[n_in]
    n_own = 3 if own_f32 else 0
    own_dst = refs[n_in + 1:n_in + 1 + n_own]
    tile_dst = refs[n_in + 1 + n_own:n_in + 1 + n_own + n_tile]
    xn_ref = refs[-1]
    j = pl.program_id(1)

    def swiglu_step(xn):
        w1, w3, w2 = w1_ref[...], w3_ref[...], w2_ref[...]
        if own_f32:
            w1, w3, w2 = w1.astype(BF16), w3.astype(BF16), w2.astype(BF16)
            own_dst[0][...] = w1
            own_dst[1][...] = w3
            own_dst[2][...] = w2
        a = jnp.dot(xn, w1, preferred_element_type=F32)
        b = jnp.dot(xn, w3, preferred_element_type=F32)
        h = (a * jax.nn.sigmoid(a)) * (b * 0.5)
        return jnp.dot(h.astype(BF16), w2, preferred_element_type=F32)

    if fuse_prologue:
        @pl.when(j == 0)
        def _():
            x = x_ref[...]
            xn = _rms(x, g_ref[...]).astype(BF16)
            xn_ref[...] = xn
            o_ref[...] = x + swiglu_step(xn)

        @pl.when(j > 0)
        def _():
            o_ref[...] += swiglu_step(xn_ref[...])
    else:
        @pl.when(j == 0)
        def _():
            x = x_ref[...]
            xn_ref[...] = _rms(x, g_ref[...]).astype(BF16)
            o_ref[...] = x

        o_ref[...] += swiglu_step(xn_ref[...])

    if final_norm:
        @pl.when(j == n_ff - 1)
        def _():
            o_ref[...] = _rms(o_ref[...], gf_ref[...])

    for src, dst in zip(tile_src, tile_dst):
        dst[...] = src[...].astype(BF16)

    if tr_kj:
        ta_ref, tb_ref = refs[6 + n_tile], refs[7 + n_tile]
        main_ref = refs[-2]
        m = ((pl.program_id(0) + tile_lo) * tr_kj + j) // TR_SPLIT

        @pl.when((j < tr_kj) & (m < gate0 // LANES))
        def _():
            main_ref[...] = ta_ref[...].T.astype(BF16)

        @pl.when((j < tr_kj) & (m >= gate0 // LANES))
        def _():
            shifted = jnp.concatenate([ta_ref[gate_w:, :], tb_ref[:gate_w, :]], axis=0)
            main_ref[...] = shifted.T.astype(BF16)


def _ffn(x, g, w1, w3, w2, gf, *, final_norm, tm=1024, tf=512, tf_cast=512, layer=0, tile_lo=0, n_tiles=None,
         own_f32=False, carry=(), cast_tiles=(), w_in_t=None, gate0=None, gate_w=0):
    t, d = x.shape
    dff = w1.shape[-1] if w1.shape[-2] == d else w1.shape[-2]
    n_i, n_ff = t // tm, dff // tf
    n_tiles = n_i - tile_lo if n_tiles is None else n_tiles
    rb = d // n_i
    once = dict(pipeline_mode=pl.Buffered(1)) if n_tiles == 1 else {}
    tok = lambda i, j: (i + tile_lo, 0)
    in_specs = [pl.BlockSpec((tm, d), tok, **once), pl.BlockSpec((1, d), lambda i, j: (0, 0))]
    if own_f32:
        in_specs += [pl.BlockSpec((None, d, tf), lambda i, j: (layer, 0, j)),
                     pl.BlockSpec((None, d, tf), lambda i, j: (layer, 0, j)),
                     pl.BlockSpec((None, tf, d), lambda i, j: (layer, j, 0))]
    else:
        in_specs += [pl.BlockSpec((d, tf), lambda i, j: (0, j)),
                     pl.BlockSpec((d, tf), lambda i, j: (0, j)),
                     pl.BlockSpec((tf, d), lambda i, j: (j, 0))]
    in_specs.append(pl.BlockSpec((1, d), lambda i, j: (0, 0)))
    out_specs = [pl.BlockSpec((tm, d), tok, **once)]
    out_shape = [jax.ShapeDtypeStruct((t, d), F32)]
    if own_f32:
        out_specs += [pl.BlockSpec((d, tf), lambda i, j: (0, j)),
                      pl.BlockSpec((d, tf), lambda i, j: (0, j)),
                      pl.BlockSpec((tf, d), lambda i, j: (j, 0))]
        out_shape += [jax.ShapeDtypeStruct((d, dff), BF16), jax.ShapeDtypeStruct((d, dff), BF16),
                      jax.ShapeDtypeStruct((dff, d), BF16)]
    cj = lambda j: (j * tf) // tf_cast
    for w in cast_tiles:
        if w.shape[1:] == (d, dff):
            in_specs.append(pl.BlockSpec((None, rb, tf_cast), lambda i, j: (layer, i + tile_lo, cj(j))))
            out_specs.append(pl.BlockSpec((rb, tf_cast), lambda i, j: (i + tile_lo, cj(j))))
        else:
            assert w.shape[1:] == (dff, d) and rb % LANES == 0
            in_specs.append(pl.BlockSpec((None, tf_cast, rb), lambda i, j: (layer, cj(j), i + tile_lo)))
            out_specs.append(pl.BlockSpec((tf_cast, rb), lambda i, j: (cj(j), i + tile_lo)))
        out_shape.append(jax.ShapeDtypeStruct(w.shape[1:], BF16))
    tr_kj, tr_in = 0, ()
    if w_in_t is not None:
        n_main = w_in_t.shape[1] - gate_w
        sp = TR_SPLIT
        assert w_in_t.shape[2] == d and gate0 % LANES == 0 and n_main % (LANES * n_i) == 0 and d % (sp * LANES) == 0
        tr_kj = sp * n_main // (LANES * n_i)
        assert tr_kj <= n_ff
        job = lambda i, j: (i + tile_lo) * tr_kj + jnp.minimum(j, tr_kj - 1)
        in_specs += [pl.BlockSpec((None, LANES, d // sp), lambda i, j: (layer, job(i, j) // sp, job(i, j) % sp)),
                     pl.BlockSpec((None, LANES, d // sp), lambda i, j: (layer, job(i, j) // sp + 1, job(i, j) % sp))]
        out_specs.append(pl.BlockSpec((d // sp, LANES), lambda i, j: (job(i, j) % sp, job(i, j) // sp)))
        out_shape.append(jax.ShapeDtypeStruct((d, n_main), BF16))
        tr_in = (w_in_t, w_in_t)
    aliases = {}
    if carry:
        n_before = len(in_specs)
        keep = [0] + list(range(len(out_shape) - len(carry) + 1, len(out_shape)))
        assert len(keep) == len(carry) and not own_f32
        for k, (c, o) in enumerate(zip(carry, keep)):
            assert c.shape == out_shape[o].shape and c.dtype == out_shape[o].dtype
            in_specs.append(pl.BlockSpec(memory_space=pl.ANY))
            aliases[n_before + k] = o
    body = functools.partial(_ffn_body, n_ff=n_ff, final_norm=final_norm, own_f32=own_f32, n_tile=len(cast_tiles),
                             tr_kj=tr_kj, tile_lo=tile_lo, gate0=gate0, gate_w=gate_w, n_carry=len(carry),
                             fuse_prologue=not (own_f32 and tr_kj))
    outs = pl.pallas_call(
        body,
        grid=(n_tiles, n_ff),
        in_specs=in_specs,
        out_specs=out_specs,
        out_shape=out_shape,
        input_output_aliases=aliases,
        scratch_shapes=[pltpu.VMEM((tm, d), BF16)],
        compiler_params=_params(("parallel", "arbitrary")),
        name=("ffn_final" if final_norm else "ffn") + ("_head" if own_f32 else ""),
    )(x, g, w1, w3, w2, gf, *cast_tiles, *tr_in, *carry)
    return outs[0], outs[1:]


def _inproj_body(*refs, n_cast):
    x_ref, g_ref, w_ref, wg_ref, bg_ref = refs[:5]
    cast_src = refs[5:5 + n_cast]
    q_ref, k_ref, v_ref, o_ref, u_ref, gates_ref = refs[5 + n_cast:11 + n_cast]
    cast_dst = refs[11 + n_cast:11 + 2 * n_cast]
    hn_ref = refs[-1]
    j = pl.program_id(1)
    nh = MLSTM_HEADS

    @pl.when(j == 0)
    def _():
        hn = _rms(x_ref[...], g_ref[...]).astype(BF16)
        hn_ref[...] = hn
        gt = lax.dot_general(wg_ref[...], hn, (((1,), (1,)), ((), ())), preferred_element_type=F32)
        gt = gt + bg_ref[...]
        for h in range(nh):
            gates_ref[h] = jnp.concatenate([gt[h:h + 1, :], gt[nh + h:nh + h + 1, :]], axis=0)

    def project():
        return jnp.dot(hn_ref[...], w_ref[...], preferred_element_type=F32)

    def heads(dst_ref, res, col0):
        wd = dst_ref.shape[-1]
        for h in range(nh):
            dst_ref[h] = res[:, col0 + h * wd:col0 + (h + 1) * wd].astype(BF16)

    @pl.when(j == 0)
    def _():
        res = project()
        heads(q_ref, res, 0)
        heads(k_ref, res, res.shape[1] // 2)

    @pl.when(j == 1)
    def _():
        heads(v_ref, project(), 0)

    @pl.when(j == 2)
    def _():
        heads(o_ref, project(), 0)

    @pl.when(j == 3)
    def _():
        u_ref[...] = project()

    for src, dst in zip(cast_src, cast_dst):
        dst[...] = src[...].astype(BF16)


def _inproj(x, g, w_main, wg_t, bg, *, tm=1024, layer=0, cast_rows=()):
    t, d = x.shape
    w = w_main.shape[1] // 4
    n_i, n_j = t // tm, 4
    in_specs = [
        pl.BlockSpec((tm, d), lambda i, j: (i, 0)),
        pl.BlockSpec((1, d), lambda i, j: (0, 0)),
        pl.BlockSpec((d, w), lambda i, j: (0, j)),
        pl.BlockSpec((2 * MLSTM_HEADS, d), lambda i, j: (0, 0)),
        pl.BlockSpec((2 * MLSTM_HEADS, 1), lambda i, j: (0, 0)),
    ]
    nh = MLSTM_HEADS
    dk, dv = w // (2 * nh), w // nh
    hm = lambda wd: pl.BlockSpec((nh, tm, wd), lambda i, j: (0, i, 0))
    out_specs = [hm(dk), hm(dk), hm(dv), hm(dv),
                 pl.BlockSpec((tm, w), lambda i, j: (i, 0)),
                 pl.BlockSpec((nh, 2, tm), lambda i, j: (0, 0, i))]
    out_shape = [
        jax.ShapeDtypeStruct((nh, t, dk), BF16),
        jax.ShapeDtypeStruct((nh, t, dk), BF16),
        jax.ShapeDtypeStruct((nh, t, dv), BF16),
        jax.ShapeDtypeStruct((nh, t, dv), BF16),
        jax.ShapeDtypeStruct((t, w), F32),
        jax.ShapeDtypeStruct((nh, 2, t), F32),
    ]
    for a in cast_rows:
        n_rows, n_cols = a.shape[1:]
        rr = n_rows // (n_i * n_j)
        assert n_rows % (n_i * n_j) == 0 and rr % 16 == 0
        in_specs.append(pl.BlockSpec((None, rr, n_cols), lambda i, j: (layer, i * n_j + j, 0)))
        out_specs.append(pl.BlockSpec((rr, n_cols), lambda i, j: (i * n_j + j, 0)))
        out_shape.append(jax.ShapeDtypeStruct((n_rows, n_cols), BF16))
    outs = pl.pallas_call(
        functools.partial(_inproj_body, n_cast=len(cast_rows)),
        grid=(n_i, n_j),
        in_specs=in_specs,
        out_specs=out_specs,
        out_shape=out_shape,
        scratch_shapes=[pltpu.VMEM((tm, d), BF16)],
        compiler_params=_params(("parallel", "arbitrary")),
        name="inproj",
    )(x, g, w_main, wg_t, bg, *cast_rows)
    return outs[:6], outs[6:]


def _split3(x):
    hi = x.astype(BF16).astype(F32)
    r1 = x - hi
    mid = r1.astype(BF16).astype(F32)
    lo = (r1 - mid).astype(BF16).astype(F32)
    return [hi, mid, lo]


def _log_sigmoid(x):
    return jnp.minimum(x, 0.0) - jnp.log(1.0 + jnp.exp(-jnp.abs(x)))


def _mlstm_body(q_ref, k_ref, v_ref, og_ref, g_ref, cwq_ref, cwk_ref, cbq_ref, cbk_ref, nw_ref,
                sel_ref, out_ref, xp_ref, qs_ref, ks_ref, cols_ref, *, seq, dk, dv):
    L = MLSTM_CHUNK
    nc = seq // L

    def conv_silu(x_ref, w_ref, b_ref, dst_ref, scale):
        xp_ref[0:SUBLANES, :] = jnp.zeros((SUBLANES, dk), F32)
        xp_ref[SUBLANES:, :] = x_ref[...].astype(F32)
        w = w_ref[...]
        acc = b_ref[...] + xp_ref[pl.ds(SUBLANES, seq), :] * w[3:4, :]
        for d in (1, 2, 3):
            acc = acc + xp_ref[pl.ds(SUBLANES - d, seq), :] * w[3 - d:4 - d, :]
        y = acc * jax.nn.sigmoid(acc)
        if scale != 1.0:
            y = y * scale
        dst_ref[...] = y.astype(BF16)

    conv_silu(q_ref, cwq_ref, cbq_ref, qs_ref, dk ** -0.5)
    conv_silu(k_ref, cwk_ref, cbk_ref, ks_ref, 1.0)

    rows = lax.broadcasted_iota(jnp.int32, (L, L), 0)
    cols = lax.broadcasted_iota(jnp.int32, (L, L), 1)
    causal = cols <= rows
    tri = jnp.where(rows <= cols, 1.0, 0.0).astype(BF16)

    ncp = -(-nc // SUBLANES) * SUBLANES
    pad = [jnp.zeros((ncp - nc, L), F32)] if ncp > nc else []
    g = g_ref[...]
    li_all = jnp.concatenate([g[0:1, c * L:(c + 1) * L] for c in range(nc)] + pad, axis=0)
    lf_all = jnp.concatenate([_log_sigmoid(g[1:2, c * L:(c + 1) * L]) for c in range(nc)] + pad, axis=0)
    parts = jnp.concatenate(_split3(lf_all), axis=0).astype(BF16)
    cs = jnp.dot(parts, tri, preferred_element_type=F32)
    b_all = cs[0:ncp] + cs[ncp:2 * ncp] + cs[2 * ncp:3 * ncp]
    d_all = li_all - b_all
    at = jnp.concatenate(_split3(b_all) + _split3(d_all), axis=0).astype(BF16)
    cols_ref[...] = lax.dot_general(at, sel_ref[...], (((0,), (0,)), ((), ())), preferred_element_type=F32)

    ext = 2 * LANES
    rep = lambda a: jnp.concatenate([a] * (dv // LANES), axis=1)
    ones_l = jnp.ones((L, ext), BF16)
    mean_w = jnp.full((dv, LANES), 1.0 / dv, BF16)
    ct_ext = jnp.zeros((dk, dv + ext), F32)
    m_prev = jnp.zeros((1, 1), F32)
    nw = nw_ref[...]

    for c in range(nc):
        r0 = c * L
        q = qs_ref[pl.ds(r0, L), :]
        k = ks_ref[pl.ds(r0, L), :]
        v_ext = jnp.concatenate([v_ref[pl.ds(r0, L), :], ones_l], axis=1)
        d_row = d_all[c:c + 1, :]
        b_end = b_all[c:c + 1, L - 1:L]
        b_c = cols_ref[:, c * LANES:(c + 1) * LANES]
        d_c = cols_ref[:, (nc + c) * LANES:(nc + c + 1) * LANES]

        cm_col = jnp.max(jnp.where(causal, d_row, -jnp.inf), axis=-1, keepdims=True)
        m_t = b_c + jnp.maximum(m_prev, cm_col)
        a_t = jnp.exp(b_c + m_prev - m_t)
        e_mt = jnp.exp(-m_t)
        arg = jnp.concatenate([b_c - m_t] * (L // LANES), axis=1) + d_row
        d_mat = jnp.where(causal, jnp.exp(arg), 0.0)

        qk = lax.dot_general(q, k, (((1,), (1,)), ((), ())), preferred_element_type=F32)
        s_b = (qk * d_mat).astype(BF16)
        sv = jnp.dot(s_b, v_ext, preferred_element_type=F32)
        qc = jnp.dot(q, ct_ext.astype(BF16), preferred_element_type=F32)
        num = sv[:, :dv] + rep(a_t) * qc[:, :dv]
        den = sv[:, dv:dv + LANES] + a_t * qc[:, dv:dv + LANES]
        inv = 1.0 / jnp.maximum(jnp.abs(den), e_mt)
        h = num * rep(inv)
        msq = jnp.dot((h * h).astype(BF16), mean_w, preferred_element_type=F32)
        hn = h * rep(lax.rsqrt(msq + EPS)) * nw
        og = og_ref[pl.ds(r0, L), :].astype(F32)
        out_ref[pl.ds(r0, L), :] = (hn * jax.nn.sigmoid(og)).astype(BF16)

        w_end = b_end + d_row
        m_loc = jnp.max(w_end, axis=-1, keepdims=True)
        e_c = jnp.exp(b_end + d_c - m_loc)
        ke = (k.astype(F32) * e_c).astype(BF16)
        c_loc = lax.dot_general(ke, v_ext, (((0,), (0,)), ((), ())), preferred_element_type=F32)
        m_new = jnp.maximum(b_end + m_prev, m_loc)
        ct_ext = jnp.exp(b_end + m_prev - m_new) * ct_ext + jnp.exp(m_loc - m_new) * c_loc
        m_prev = m_new


def _mlstm_selector(nc):
    ncp = -(-nc // SUBLANES) * SUBLANES
    sel = np.zeros((6 * ncp, 2 * nc * LANES), np.float32)
    for kind in range(2):
        for part in range(3):
            for c in range(nc):
                sel[(kind * 3 + part) * ncp + c, (kind * nc + c) * LANES:(kind * nc + c + 1) * LANES] = 1.0
    return jnp.asarray(sel, BF16)


def _mlstm(q, k, v, og, gates, conv_w, conv_b, norm_w, *, batch, seq):
    nh, t, dk = q.shape
    dv = v.shape[2]
    assert dk == LANES and dv % LANES == 0 and seq % MLSTM_CHUNK == 0 and MLSTM_CHUNK % LANES == 0
    nc = seq // MLSTM_CHUNK
    sel = _mlstm_selector(nc)
    body = functools.partial(_mlstm_body, seq=seq, dk=dk, dv=dv)
    return pl.pallas_call(
        body,
        grid=(batch, nh),
        in_specs=[
            pl.BlockSpec((None, seq, dk), lambda b, h: (h, b, 0)),
            pl.BlockSpec((None, seq, dk), lambda b, h: (h, b, 0)),
            pl.BlockSpec((None, seq, dv), lambda b, h: (h, b, 0)),
            pl.BlockSpec((None, seq, dv), lambda b, h: (h, b, 0)),
            pl.BlockSpec((None, 2, seq), lambda b, h: (h, 0, b)),
            pl.BlockSpec((conv_w.shape[0], dk), lambda b, h: (0, h)),
            pl.BlockSpec((conv_w.shape[0], dk), lambda b, h: (0, nh + h)),
            pl.BlockSpec((1, dk), lambda b, h: (0, h)),
            pl.BlockSpec((1, dk), lambda b, h: (0, nh + h)),
            pl.BlockSpec((1, dv), lambda b, h: (0, h)),
            pl.BlockSpec(sel.shape, lambda b, h: (0, 0)),
        ],
        out_specs=pl.BlockSpec((None, seq, dv), lambda b, h: (h, b, 0)),
        out_shape=jax.ShapeDtypeStruct((nh, t, dv), BF16),
        scratch_shapes=[
            pltpu.VMEM((seq + SUBLANES, dk), F32),
            pltpu.VMEM((seq, dk), BF16),
            pltpu.VMEM((seq, dk), BF16),
            pltpu.VMEM((MLSTM_CHUNK, 2 * nc * LANES), F32),
        ],
        compiler_params=_params(("parallel", "parallel")),
        name="mlstm",
    )(q, k, v, og, gates, conv_w, conv_w, conv_b, conv_b, norm_w, sel)


def _cmul(ar, ai, br, bi):
    return ar * br - ai * bi, ar * bi + ai * br


def _s5pack_body(lr_ref, li_ref, ldt_ref, br_ref, bi_ref, cr_ref, ci_ref, tile_ref,
                 bm_ref, cm_ref, dm_ref, lpr_ref, lpi_ref):
    R = S5_LIFT
    cb = S5_GROUPS_PER_BLOCK * S5_GROUP
    half = S5_GROUPS_PER_BLOCK * S5_STATE
    lr = lr_ref[...]
    li = li_ref[...]
    dt = jnp.exp(ldt_ref[...])
    mag = jnp.exp(lr * dt)
    ang = li * dt
    lbr = mag * jnp.cos(ang)
    lbi = mag * jnp.sin(ang)
    nr = lbr - 1.0
    den = lr * lr + li * li
    cr = (nr * lr + lbi * li) / den
    ci = (lbi * lr - nr * li) / den
    pr, pi_ = _cmul(cr, ci, br_ref[...], bi_ref[...])
    qr, qi = cr_ref[...], ci_ref[...]

    row = lax.broadcasted_iota(jnp.int32, (cb, half), 0)
    lane = lax.broadcasted_iota(jnp.int32, (cb, half), 1)
    same_group = (row // S5_GROUP) == (lane // S5_STATE)
    tile = tile_ref[...]

    def blockdiag(a):
        return jnp.where(same_group, jnp.dot(a.astype(BF16), tile, preferred_element_type=F32), 0.0)

    def pair(re, im):
        return jnp.concatenate([blockdiag(re), blockdiag(im)], axis=1).astype(BF16)

    c0 = pair(qr, -qi)
    lag_rows = []
    for m in range(R):
        rows_m = pair(pr, pi_)
        bm_ref[(R - 1 - m) * cb:(R - m) * cb, :] = rows_m
        lag_rows.append(rows_m)
        pr, pi_ = _cmul(pr, pi_, lbr, lbi)
        qr, qi = _cmul(qr, qi, lbr, lbi)
        cm_ref[m * cb:(m + 1) * cb, :] = pair(qr, -qi)
    dlag = [lax.dot_general(lag_rows[m], c0, (((1,), (1,)), ((), ())), preferred_element_type=F32).astype(BF16)
            for m in range(R)]
    zero = jnp.zeros((cb, cb), BF16)
    for rp in range(R):
        for r in range(R):
            dm_ref[rp * cb:(rp + 1) * cb, r * cb:(r + 1) * cb] = dlag[r - rp] if r >= rp else zero

    wr, wi = lbr, lbi
    for _ in range(R - 1):
        wr, wi = _cmul(wr, wi, lbr, lbi)
    first = same_group & ((row % S5_GROUP) == 0)

    def lane_vector(a):
        t = sum(jnp.dot(part.astype(BF16), tile, preferred_element_type=F32) for part in _split3(a))
        return jnp.sum(jnp.where(first, t, 0.0), axis=0, keepdims=True)

    lpr_ref[...] = lane_vector(wr)
    lpi_ref[...] = lane_vector(wi)


def _s5_weights(lam_re, lam_im, log_dt, b_re, b_im, c_re, c_im):
    g, p = lam_re.shape
    hch = b_re.shape[-1]
    gpb = S5_GROUPS_PER_BLOCK
    ngb = g // gpb
    R = S5_LIFT
    cb, half = gpb * hch, gpb * p
    rep = lambda a: jnp.repeat(a, hch, axis=0)
    to_rows = lambda a: jnp.transpose(a, (0, 2, 1)).reshape(g * hch, p)
    ldt = jnp.broadcast_to(log_dt[:, None], (g, p))
    tile = jnp.asarray(np.tile(np.eye(p, dtype=np.float32), (1, gpb)), BF16)
    blk = pl.BlockSpec((cb, p), lambda i: (i, 0))
    return pl.pallas_call(
        _s5pack_body,
        grid=(ngb,),
        in_specs=[blk] * 7 + [pl.BlockSpec((p, half), lambda i: (0, 0))],
        out_specs=[pl.BlockSpec((None, R * cb, 2 * half), lambda i: (i, 0, 0)),
                   pl.BlockSpec((None, R * cb, 2 * half), lambda i: (i, 0, 0)),
                   pl.BlockSpec((None, R * cb, R * cb), lambda i: (i, 0, 0)),
                   pl.BlockSpec((None, 1, half), lambda i: (i, 0, 0)),
                   pl.BlockSpec((None, 1, half), lambda i: (i, 0, 0))],
        out_shape=[jax.ShapeDtypeStruct((ngb, R * cb, 2 * half), BF16),
                   jax.ShapeDtypeStruct((ngb, R * cb, 2 * half), BF16),
                   jax.ShapeDtypeStruct((ngb, R * cb, R * cb), BF16),
                   jax.ShapeDtypeStruct((ngb, 1, half), F32),
                   jax.ShapeDtypeStruct((ngb, 1, half), F32)],
        name="s5pack",
    )(rep(lam_re), rep(lam_im), rep(ldt), to_rows(b_re), to_rows(b_im),
      c_re.reshape(g * hch, p), c_im.reshape(g * hch, p), tile)


def _s5_body(u_ref, bm_ref, cm_ref, dm_ref, lr_ref, li_ref, dsk_ref, wglu_ref, bglu_ref, o_ref,
             usb_ref, sb_ref, y_ref, st_ref, *, ts, ngb):
    R = S5_LIFT
    nk = ts // R
    rows = nk * SUBLANES
    half = S5_GROUPS_PER_BLOCK * S5_STATE
    cb = S5_GROUPS_PER_BLOCK * S5_GROUP

    @pl.when(pl.program_id(0) == 0)
    def _():
        st_ref[...] = jnp.zeros_like(st_ref)

    def srow(s):
        return (s % R) * rows + (s // R) * SUBLANES

    for s in range(ts):
        usb_ref[srow(s):srow(s) + SUBLANES, :] = u_ref[:, s, :]

    def lhs(gb):
        return jnp.concatenate([usb_ref[r * rows:(r + 1) * rows, gb * cb:(gb + 1) * cb] for r in range(R)],
                               axis=1).astype(BF16)

    def project_in(gb):
        sb_ref[gb % 2, SUBLANES:, :] = jnp.dot(lhs(gb), bm_ref[gb], preferred_element_type=F32)

    project_in(0)
    for gb in range(ngb):
        if gb + 1 < ngb:
            project_in(gb + 1)
        buf = sb_ref.at[gb % 2]
        lrb = jnp.broadcast_to(lr_ref[gb], (SUBLANES, half))
        lib = jnp.broadcast_to(li_ref[gb], (SUBLANES, half))
        xr = st_ref[gb, :, 0:half]
        xi = st_ref[gb, :, half:2 * half]
        buf[0:SUBLANES, 0:half] = xr
        buf[0:SUBLANES, half:2 * half] = xi
        for k in range(nk):
            rs = slice((k + 1) * SUBLANES, (k + 2) * SUBLANES)
            nr = lrb * xr - lib * xi + buf[rs, 0:half]
            ni = lrb * xi + lib * xr + buf[rs, half:2 * half]
            buf[rs, 0:half] = nr
            buf[rs, half:2 * half] = ni
            xr, xi = nr, ni
        st_ref[gb, :, 0:half] = xr
        st_ref[gb, :, half:2 * half] = xi

        yo = lax.dot_general(buf[0:rows, :].astype(BF16), cm_ref[gb], (((1,), (1,)), ((), ())),
                             preferred_element_type=F32)
        yd = jnp.dot(lhs(gb), dm_ref[gb], preferred_element_type=F32)
        for r in range(R):
            ug = usb_ref[r * rows:(r + 1) * rows, gb * cb:(gb + 1) * cb]
            yg = yo[:, r * cb:(r + 1) * cb] + yd[:, r * cb:(r + 1) * cb] + dsk_ref[:, gb * cb:(gb + 1) * cb] * ug
            y_ref[r * rows:(r + 1) * rows, gb * cb:(gb + 1) * cb] = jax.nn.gelu(yg)

    y = y_ref[...]
    z = jnp.dot(y.astype(BF16), wglu_ref[...], preferred_element_type=F32) + bglu_ref[...]
    usb_ref[...] = y * jax.nn.sigmoid(z)

    for s in range(ts):
        o_ref[:, s, :] = usb_ref[srow(s):srow(s) + SUBLANES, :]


def _s5(u, bm, cm, dm, lpr, lpi, dsk, wglu, bglu, *, ts=64):
    batch, seq, width = u.shape
    ngb = bm.shape[0]
    half = S5_GROUPS_PER_BLOCK * S5_STATE
    assert batch == SUBLANES and ts % S5_LIFT == 0 and seq % ts == 0
    body = functools.partial(_s5_body, ts=ts, ngb=ngb)
    full = lambda a: pl.BlockSpec(a.shape, lambda i, n=a.ndim: (0,) * n)
    return pl.pallas_call(
        body,
        grid=(seq // ts,),
        in_specs=[pl.BlockSpec((batch, ts, width), lambda i: (0, i, 0)),
                  full(bm), full(cm), full(dm), full(lpr), full(lpi), full(dsk), full(wglu), full(bglu)],
        out_specs=pl.BlockSpec((batch, ts, width), lambda i: (0, i, 0)),
        out_shape=jax.ShapeDtypeStruct((batch, seq, width), F32),
        scratch_shapes=[
            pltpu.VMEM((ts * batch, width), F32),
            pltpu.VMEM((2, (ts // S5_LIFT + 1) * batch, 2 * half), F32),
            pltpu.VMEM((ts * batch, width), F32),
            pltpu.VMEM((ngb, batch, 2 * half), F32),
        ],
        compiler_params=_params(("arbitrary",)),
        name="s5",
    )(u, bm, cm, dm, lpr, lpi, dsk, wglu, bglu)


def _outproj_body(x_ref, hm_ref, hs_ref, wa_ref, wb_ref, o_ref):
    hm = jnp.concatenate([hm_ref[h] for h in range(hm_ref.shape[0])], axis=1)
    acc = jnp.dot(hm, wa_ref[...], preferred_element_type=F32)
    acc = acc + jnp.dot(hs_ref[...].astype(BF16), wb_ref[...], preferred_element_type=F32)
    o_ref[...] = x_ref[...] + acc


def _outproj(x, hm, hs, wo, *, tm=512):
    t, d = x.shape
    w = hm.shape[0] * hm.shape[2]
    return pl.pallas_call(
        _outproj_body,
        grid=(t // tm,),
        in_specs=[
            pl.BlockSpec((tm, d), lambda i: (i, 0)),
            pl.BlockSpec((hm.shape[0], tm, hm.shape[2]), lambda i: (0, i, 0)),
            pl.BlockSpec((tm, hs.shape[1]), lambda i: (i, 0)),
            pl.BlockSpec((w, d), lambda i: (0, 0)),
            pl.BlockSpec((hs.shape[1], d), lambda i: (w // hs.shape[1], 0)),
        ],
        out_specs=pl.BlockSpec((tm, d), lambda i: (i, 0)),
        out_shape=jax.ShapeDtypeStruct((t, d), F32),
        compiler_params=_params(("parallel",)),
        name="outproj",
    )(x, hm, hs, wo, wo)


def kernel(x, ffn1_norm, ffn1_w1, ffn1_w3, ffn1_w2, mix_norm, w_in, conv_w, conv_b, b_i, b_f, mlstm_norm, lam_re, lam_im, log_dt, b_re, b_im, c_re, c_im, d_skip, w_glu, b_glu, w_out, ffn2_norm, ffn2_w1, ffn2_w3, ffn2_w2, final_norm):
    batch, seq, d = x.shape
    depth = ffn1_norm.shape[0]
    nh = MLSTM_HEADS
    qk_cols = conv_w.shape[-1]
    mw = mlstm_norm.shape[-1]
    gate0 = qk_cols + 2 * mw
    sw = w_glu.shape[-1]
    row = lambda a: a.reshape(1, -1).astype(F32)
    xt = x.reshape(batch * seq, d)
    gfin = row(final_norm)
    w_in_t = jnp.swapaxes(w_in, 1, 2)
    for l in range(depth):
        jobs = dict(layer=l, w_in_t=w_in_t, gate0=gate0, gate_w=2 * nh)
        x_head, head = _ffn(xt, row(ffn1_norm[l]), ffn1_w1, ffn1_w3, ffn1_w2, gfin, final_norm=False, tf=256,
                            n_tiles=1, own_f32=True, **jobs)
        xt, (w_main,) = _ffn(xt, row(ffn1_norm[l]), head[0], head[1], head[2], gfin, final_norm=False,
                             tile_lo=1, carry=(x_head, head[3]), **jobs)

        wg_t = w_in_t[l, gate0:gate0 + 2 * nh, :].astype(BF16)
        bg = jnp.concatenate([b_i[l], b_f[l]]).reshape(2 * nh, 1).astype(F32)
        (q, k, v, og, u, gates), (wo, wglu) = _inproj(xt, row(mix_norm[l]), w_main, wg_t, bg, layer=l,
                                                      cast_rows=(w_out, w_glu))

        hm = _mlstm(q, k, v, og, gates, conv_w[l].astype(F32), row(conv_b[l]), row(mlstm_norm[l]),
                    batch=batch, seq=seq)

        bm, cm, dm, lpr, lpi = _s5_weights(lam_re[l], lam_im[l], log_dt[l], b_re[l], b_im[l], c_re[l], c_im[l])
        hs = _s5(u.reshape(batch, seq, sw), bm, cm, dm, lpr, lpi, row(d_skip[l]), wglu, row(b_glu[l]))

        xt = _outproj(xt, hm, hs.reshape(batch * seq, sw), wo)

        last = l == depth - 1
        y_head, head = _ffn(xt, row(ffn2_norm[l]), ffn2_w1, ffn2_w3, ffn2_w2, gfin, final_norm=last, tf=256,
                            n_tiles=1, own_f32=True, layer=l)
        xt, _ = _ffn(xt, row(ffn2_norm[l]), head[0], head[1], head[2], gfin, final_norm=last, tile_lo=1,
                     carry=(y_head,))
    return xt.reshape(batch, seq, d)
```

```python
import functools

import jax
import jax.numpy as jnp
import numpy as np
from jax import lax
from jax.experimental import pallas as pl
from jax.experimental.pallas import tpu as pltpu

F32 = jnp.float32
BF16 = jnp.bfloat16

EPS = 1e-6
MLSTM_HEADS = 4
S5_GROUP = 16
S5_STATE = 64
S5_GROUPS_PER_BLOCK = 8
S5_LIFT = 2
LANES = 128
SUBLANES = 8
MLSTM_CHUNK = 256
TR_SPLIT = 4
VMEM_LIMIT_BYTES = 58 * 1024 * 1024


def _rms(x, g):
    return x * lax.rsqrt(jnp.mean(x * x, axis=-1, keepdims=True) + EPS) * g


def _params(sem):
    return pltpu.CompilerParams(dimension_semantics=sem, vmem_limit_bytes=VMEM_LIMIT_BYTES)


def _ffn_body(*refs, n_ff, final_norm, own_f32, n_tile, tr_kj, tile_lo, gate0, gate_w, n_carry, fuse_prologue):
    x_ref, g_ref, w1_ref, w3_ref, w2_ref, gf_ref = refs[:6]
    n_tr = 2 if tr_kj else 0
    n_in = 6 + n_tile + n_tr + n_carry
    tile_src = refs[6:6 + n_tile]
    o_ref = refs[n_in]
    n_own = 3 if own_f32 else 0
    own_dst = refs[n_in + 1:n_in + 1 + n_own]
    tile_dst = refs[n_in + 1 + n_own:n_in + 1 + n_own + n_tile]
    xn_ref = refs[-1]
    j = pl.program_id(1)

    def swiglu_step(xn):
        w1, w3, w2 = w1_ref[...], w3_ref[...], w2_ref[...]
        if own_f32:
            w1, w3, w2 = w1.astype(BF16), w3.astype(BF16), w2.astype(BF16)
            own_dst[0][...] = w1
            own_dst[1][...] = w3
            own_dst[2][...] = w2
        a = jnp.dot(xn, w1, preferred_element_type=F32)
        b = jnp.dot(xn, w3, preferred_element_type=F32)
        h = (a * jax.nn.sigmoid(a)) * (b * 0.5)
        return jnp.dot(h.astype(BF16), w2, preferred_element_type=F32)

    if fuse_prologue:
        @pl.when(j == 0)
        def _():
            x = x_ref[...]
            xn = _rms(x, g_ref[...]).astype(BF16)
            xn_ref[...] = xn
            o_ref[...] = x + swiglu_step(xn)

        @pl.when(j > 0)
        def _():
            o_ref[...] += swiglu_step(xn_ref[...])
    else:
        @pl.when(j == 0)
        def _():
            x = x_ref[...]
            xn_ref[...] = _rms(x, g_ref[...]).astype(BF16)
            o_ref[...] = x

        o_ref[...] += swiglu_step(xn_ref[...])

    if final_norm:
        @pl.when(j == n_ff - 1)
        def _():
            o_ref[...] = _rms(o_ref[...], gf_ref[...])

    for src, dst in zip(tile_src, tile_dst):
        dst[...] = src[...].astype(BF16)

    if tr_kj:
        ta_ref, tb_ref = refs[6 + n_tile], refs[7 + n_tile]
        main_ref = refs[-2]
        m = ((pl.program_id(0) + tile_lo) * tr_kj + j) // TR_SPLIT

        @pl.when((j < tr_kj) & (m < gate0 // LANES))
        def _():
            main_ref[...] = ta_ref[...].T.astype(BF16)

        @pl.when((j < tr_kj) & (m >= gate0 // LANES))
        def _():
            shifted = jnp.concatenate([ta_ref[gate_w:, :], tb_ref[:gate_w, :]], axis=0)
            main_ref[...] = shifted.T.astype(BF16)


def _ffn(x, g, w1, w3, w2, gf, *, final_norm, tm=1024, tf=512, tf_cast=512, layer=0, tile_lo=0, n_tiles=None,
         own_f32=False, carry=(), cast_tiles=(), w_in_t=None, gate0=None, gate_w=0):
    t, d = x.shape
    dff = w1.shape[-1] if w1.shape[-2] == d else w1.shape[-2]
    n_i, n_ff = t // tm, dff // tf
    n_tiles = n_i - tile_lo if n_tiles is None else n_tiles
    rb = d // n_i
    once = dict(pipeline_mode=pl.Buffered(1)) if n_tiles == 1 else {}
    tok = lambda i, j: (i + tile_lo, 0)
    in_specs = [pl.BlockSpec((tm, d), tok, **once), pl.BlockSpec((1, d), lambda i, j: (0, 0))]
    if own_f32:
        in_specs += [pl.BlockSpec((None, d, tf), lambda i, j: (layer, 0, j)),
                     pl.BlockSpec((None, d, tf), lambda i, j: (layer, 0, j)),
                     pl.BlockSpec((None, tf, d), lambda i, j: (layer, j, 0))]
    else:
        in_specs += [pl.BlockSpec((d, tf), lambda i, j: (0, j)),
                     pl.BlockSpec((d, tf), lambda i, j: (0, j)),
                     pl.BlockSpec((tf, d), lambda i, j: (j, 0))]
    in_specs.append(pl.BlockSpec((1, d), lambda i, j: (0, 0)))
    out_specs = [pl.BlockSpec((tm, d), tok, **once)]
    out_shape = [jax.ShapeDtypeStruct((t, d), F32)]
    if own_f32:
        out_specs += [pl.BlockSpec((d, tf), lambda i, j: (0, j)),
                      pl.BlockSpec((d, tf), lambda i, j: (0, j)),
                      pl.BlockSpec((tf, d), lambda i, j: (j, 0))]
        out_shape += [jax.ShapeDtypeStruct((d, dff), BF16), jax.ShapeDtypeStruct((d, dff), BF16),
                      jax.ShapeDtypeStruct((dff, d), BF16)]
    cj = lambda j: (j * tf) // tf_cast
    for w in cast_tiles:
        if w.shape[1:] == (d, dff):
            in_specs.append(pl.BlockSpec((None, rb, tf_cast), lambda i, j: (layer, i + tile_lo, cj(j))))
            out_specs.append(pl.BlockSpec((rb, tf_cast), lambda i, j: (i + tile_lo, cj(j))))
        else:
            assert w.shape[1:] == (dff, d) and rb % LANES == 0
            in_specs.append(pl.BlockSpec((None, tf_cast, rb), lambda i, j: (layer, cj(j), i + tile_lo)))
            out_specs.append(pl.BlockSpec((tf_cast, rb), lambda i, j: (cj(j), i + tile_lo)))
        out_shape.append(jax.ShapeDtypeStruct(w.shape[1:], BF16))
    tr_kj, tr_in = 0, ()
    if w_in_t is not None:
        n_main = w_in_t.shape[1] - gate_w
        sp = TR_SPLIT
        assert w_in_t.shape[2] == d and gate0 % LANES == 0 and n_main % (LANES * n_i) == 0 and d % (sp * LANES) == 0
        tr_kj = sp * n_main // (LANES * n_i)
        assert tr_kj <= n_ff
        job = lambda i, j: (i + tile_lo) * tr_kj + jnp.minimum(j, tr_kj - 1)
        in_specs += [pl.BlockSpec((None, LANES, d // sp), lambda i, j: (layer, job(i, j) // sp, job(i, j) % sp)),
                     pl.BlockSpec((None, LANES, d // sp), lambda i, j: (layer, job(i, j) // sp + 1, job(i, j) % sp))]
        out_specs.append(pl.BlockSpec((d // sp, LANES), lambda i, j: (job(i, j) % sp, job(i, j) // sp)))
        out_shape.append(jax.ShapeDtypeStruct((d, n_main), BF16))
        tr_in = (w_in_t, w_in_t)
    aliases = {}
    if carry:
        n_before = len(in_specs)
        keep = [0] + list(range(len(out_shape) - len(carry) + 1, len(out_shape)))
        assert len(keep) == len(carry) and not own_f32
        for k, (c, o) in enumerate(zip(carry, keep)):
            assert c.shape == out_shape[o].shape and c.dtype == out_shape[o].dtype
            in_specs.append(pl.BlockSpec(memory_space=pl.ANY))
            aliases[n_before + k] = o
    body = functools.partial(_ffn_body, n_ff=n_ff, final_norm=final_norm, own_f32=own_f32, n_tile=len(cast_tiles),
                             tr_kj=tr_kj, tile_lo=tile_lo, gate0=gate0, gate_w=gate_w, n_carry=len(carry),
                             fuse_prologue=not (cast_tiles or tr_kj))
    outs = pl.pallas_call(
        body,
        grid=(n_tiles, n_ff),
        in_specs=in_specs,
        out_specs=out_specs,
        out_shape=out_shape,
        input_output_aliases=aliases,
        scratch_shapes=[pltpu.VMEM((tm, d), BF16)],
        compiler_params=_params(("parallel", "arbitrary")),
        name=("ffn_final" if final_norm else "ffn") + ("_head" if own_f32 else ""),
    )(x, g, w1, w3, w2, gf, *cast_tiles, *tr_in, *carry)
    return outs[0], outs[1:]


def _inproj_body(*refs, n_cast):
    x_ref, g_ref, w_ref, wg_ref, bg_ref = refs[:5]
    cast_src = refs[5:5 + n_cast]
    qkvo_ref, u_ref, gates_ref = refs[5 + n_cast:8 + n_cast]
    cast_dst = refs[8 + n_cast:8 + 2 * n_cast]
    hn_ref = refs[-1]
    j = pl.program_id(1)
    nh = MLSTM_HEADS

    @pl.when(j == 0)
    def _():
        hn = _rms(x_ref[...], g_ref[...]).astype(BF16)
        hn_ref[...] = hn
        gt = lax.dot_general(wg_ref[...], hn, (((1,), (1,)), ((), ())), preferred_element_type=F32)
        gt = gt + bg_ref[...]
        for h in range(nh):
            gates_ref[h] = jnp.concatenate([gt[h:h + 1, :], gt[nh + h:nh + h + 1, :]], axis=0)

    @pl.when(j < 3)
    def _():
        qkvo_ref[...] = jnp.dot(hn_ref[...], w_ref[...], preferred_element_type=F32).astype(BF16)

    @pl.when(j == 3)
    def _():
        u_ref[...] = jnp.dot(hn_ref[...], w_ref[...], preferred_element_type=F32)

    for src, dst in zip(cast_src, cast_dst):
        dst[...] = src[...].astype(BF16)


def _inproj(x, g, w_main, wg_t, bg, *, tm=1024, layer=0, cast_rows=()):
    t, d = x.shape
    w = w_main.shape[1] // 4
    n_i, n_j = t // tm, 4
    in_specs = [
        pl.BlockSpec((tm, d), lambda i, j: (i, 0)),
        pl.BlockSpec((1, d), lambda i, j: (0, 0)),
        pl.BlockSpec((d, w), lambda i, j: (0, j)),
        pl.BlockSpec((2 * MLSTM_HEADS, d), lambda i, j: (0, 0)),
        pl.BlockSpec((2 * MLSTM_HEADS, 1), lambda i, j: (0, 0)),
    ]
    out_specs = [pl.BlockSpec((tm, w), lambda i, j: (i, jnp.minimum(j, 2))),
                 pl.BlockSpec((tm, w), lambda i, j: (i, 0)),
                 pl.BlockSpec((MLSTM_HEADS, 2, tm), lambda i, j: (0, 0, i))]
    out_shape = [
        jax.ShapeDtypeStruct((t, 3 * w), BF16),
        jax.ShapeDtypeStruct((t, w), F32),
        jax.ShapeDtypeStruct((MLSTM_HEADS, 2, t), F32),
    ]
    for a in cast_rows:
        n_rows, n_cols = a.shape[1:]
        rr = n_rows // (n_i * n_j)
        assert n_rows % (n_i * n_j) == 0 and rr % 16 == 0
        in_specs.append(pl.BlockSpec((None, rr, n_cols), lambda i, j: (layer, i * n_j + j, 0)))
        out_specs.append(pl.BlockSpec((rr, n_cols), lambda i, j: (i * n_j + j, 0)))
        out_shape.append(jax.ShapeDtypeStruct((n_rows, n_cols), BF16))
    outs = pl.pallas_call(
        functools.partial(_inproj_body, n_cast=len(cast_rows)),
        grid=(n_i, n_j),
        in_specs=in_specs,
        out_specs=out_specs,
        out_shape=out_shape,
        scratch_shapes=[pltpu.VMEM((tm, d), BF16)],
        compiler_params=_params(("parallel", "arbitrary")),
        name="inproj",
    )(x, g, w_main, wg_t, bg, *cast_rows)
    return outs[:3], outs[3:]


def _split3(x):
    hi = x.astype(BF16).astype(F32)
    r1 = x - hi
    mid = r1.astype(BF16).astype(F32)
    lo = (r1 - mid).astype(BF16).astype(F32)
    return [hi, mid, lo]


def _log_sigmoid(x):
    return jnp.minimum(x, 0.0) - jnp.log(1.0 + jnp.exp(-jnp.abs(x)))


def _mlstm_body(q_ref, k_ref, v_ref, og_ref, g_ref, cwq_ref, cwk_ref, cbq_ref, cbk_ref, nw_ref,
                sel_ref, out_ref, xp_ref, qs_ref, ks_ref, cols_ref, *, seq, dk, dv):
    L = MLSTM_CHUNK
    nc = seq // L

    def conv_silu(x_ref, w_ref, b_ref, dst_ref, scale):
        xp_ref[0:SUBLANES, :] = jnp.zeros((SUBLANES, dk), F32)
        xp_ref[SUBLANES:, :] = x_ref[...].astype(F32)
        w = w_ref[...]
        acc = b_ref[...] + xp_ref[pl.ds(SUBLANES, seq), :] * w[3:4, :]
        for d in (1, 2, 3):
            acc = acc + xp_ref[pl.ds(SUBLANES - d, seq), :] * w[3 - d:4 - d, :]
        y = acc * jax.nn.sigmoid(acc)
        if scale != 1.0:
            y = y * scale
        dst_ref[...] = y.astype(BF16)

    conv_silu(q_ref, cwq_ref, cbq_ref, qs_ref, dk ** -0.5)
    conv_silu(k_ref, cwk_ref, cbk_ref, ks_ref, 1.0)

    rows = lax.broadcasted_iota(jnp.int32, (L, L), 0)
    cols = lax.broadcasted_iota(jnp.int32, (L, L), 1)
    causal = cols <= rows
    tri = jnp.where(rows <= cols, 1.0, 0.0).astype(BF16)

    ncp = -(-nc // SUBLANES) * SUBLANES
    pad = [jnp.zeros((ncp - nc, L), F32)] if ncp > nc else []
    g = g_ref[...]
    li_all = jnp.concatenate([g[0:1, c * L:(c + 1) * L] for c in range(nc)] + pad, axis=0)
    lf_all = jnp.concatenate([_log_sigmoid(g[1:2, c * L:(c + 1) * L]) for c in range(nc)] + pad, axis=0)
    parts = jnp.concatenate(_split3(lf_all), axis=0).astype(BF16)
    cs = jnp.dot(parts, tri, preferred_element_type=F32)
    b_all = cs[0:ncp] + cs[ncp:2 * ncp] + cs[2 * ncp:3 * ncp]
    d_all = li_all - b_all
    at = jnp.concatenate(_split3(b_all) + _split3(d_all), axis=0).astype(BF16)
    cols_ref[...] = lax.dot_general(at, sel_ref[...], (((0,), (0,)), ((), ())), preferred_element_type=F32)

    ext = 2 * LANES
    rep = lambda a: jnp.concatenate([a] * (dv // LANES), axis=1)
    ones_l = jnp.ones((L, ext), BF16)
    mean_w = jnp.full((dv, LANES), 1.0 / dv, BF16)
    ct_ext = jnp.zeros((dk, dv + ext), F32)
    m_prev = jnp.zeros((1, 1), F32)
    nw = nw_ref[...]

    for c in range(nc):
        r0 = c * L
        q = qs_ref[pl.ds(r0, L), :]
        k = ks_ref[pl.ds(r0, L), :]
        v_ext = jnp.concatenate([v_ref[pl.ds(r0, L), :], ones_l], axis=1)
        d_row = d_all[c:c + 1, :]
        b_end = b_all[c:c + 1, L - 1:L]
        b_c = cols_ref[:, c * LANES:(c + 1) * LANES]
        d_c = cols_ref[:, (nc + c) * LANES:(nc + c + 1) * LANES]

        cm_col = jnp.max(jnp.where(causal, d_row, -jnp.inf), axis=-1, keepdims=True)
        m_t = b_c + jnp.maximum(m_prev, cm_col)
        a_t = jnp.exp(b_c + m_prev - m_t)
        e_mt = jnp.exp(-m_t)
        arg = jnp.concatenate([b_c - m_t] * (L // LANES), axis=1) + d_row
        d_mat = jnp.where(causal, jnp.exp(arg), 0.0)

        qk = lax.dot_general(q, k, (((1,), (1,)), ((), ())), preferred_element_type=F32)
        s_b = (qk * d_mat).astype(BF16)
        sv = jnp.dot(s_b, v_ext, preferred_element_type=F32)
        qc = jnp.dot(q, ct_ext.astype(BF16), preferred_element_type=F32)
        num = sv[:, :dv] + rep(a_t) * qc[:, :dv]
        den = sv[:, dv:dv + LANES] + a_t * qc[:, dv:dv + LANES]
        inv = 1.0 / jnp.maximum(jnp.abs(den), e_mt)
        h = num * rep(inv)
        msq = jnp.dot((h * h).astype(BF16), mean_w, preferred_element_type=F32)
        hn = h * rep(lax.rsqrt(msq + EPS)) * nw
        og = og_ref[pl.ds(r0, L), :].astype(F32)
        out_ref[pl.ds(r0, L), :] = (hn * jax.nn.sigmoid(og)).astype(BF16)

        w_end = b_end + d_row
        m_loc = jnp.max(w_end, axis=-1, keepdims=True)
        e_c = jnp.exp(b_end + d_c - m_loc)
        ke = (k.astype(F32) * e_c).astype(BF16)
        c_loc = lax.dot_general(ke, v_ext, (((0,), (0,)), ((), ())), preferred_element_type=F32)
        m_new = jnp.maximum(b_end + m_prev, m_loc)
        ct_ext = jnp.exp(b_end + m_prev - m_new) * ct_ext + jnp.exp(m_loc - m_new) * c_loc
        m_prev = m_new


def _mlstm_selector(nc):
    ncp = -(-nc // SUBLANES) * SUBLANES
    sel = np.zeros((6 * ncp, 2 * nc * LANES), np.float32)
    for kind in range(2):
        for part in range(3):
            for c in range(nc):
                sel[(kind * 3 + part) * ncp + c, (kind * nc + c) * LANES:(kind * nc + c + 1) * LANES] = 1.0
    return jnp.asarray(sel, BF16)


def _mlstm(qkvo, gates, conv_w, conv_b, norm_w, *, batch, seq):
    t = qkvo.shape[0]
    w = qkvo.shape[1] // 3
    nh = MLSTM_HEADS
    dk = w // (2 * nh)
    dv = w // nh
    assert dk == LANES and dv % LANES == 0 and seq % MLSTM_CHUNK == 0 and MLSTM_CHUNK % LANES == 0
    nc = seq // MLSTM_CHUNK
    sel = _mlstm_selector(nc)
    body = functools.partial(_mlstm_body, seq=seq, dk=dk, dv=dv)
    return pl.pallas_call(
        body,
        grid=(batch, nh),
        in_specs=[
            pl.BlockSpec((seq, dk), lambda b, h: (b, h)),
            pl.BlockSpec((seq, dk), lambda b, h: (b, nh + h)),
            pl.BlockSpec((seq, dv), lambda b, h: (b, nh + h)),
            pl.BlockSpec((seq, dv), lambda b, h: (b, 2 * nh + h)),
            pl.BlockSpec((None, 2, seq), lambda b, h: (h, 0, b)),
            pl.BlockSpec((conv_w.shape[0], dk), lambda b, h: (0, h)),
            pl.BlockSpec((conv_w.shape[0], dk), lambda b, h: (0, nh + h)),
            pl.BlockSpec((1, dk), lambda b, h: (0, h)),
            pl.BlockSpec((1, dk), lambda b, h: (0, nh + h)),
            pl.BlockSpec((1, dv), lambda b, h: (0, h)),
            pl.BlockSpec(sel.shape, lambda b, h: (0, 0)),
        ],
        out_specs=pl.BlockSpec((seq, dv), lambda b, h: (b, h)),
        out_shape=jax.ShapeDtypeStruct((t, nh * dv), BF16),
        scratch_shapes=[
            pltpu.VMEM((seq + SUBLANES, dk), F32),
            pltpu.VMEM((seq, dk), BF16),
            pltpu.VMEM((seq, dk), BF16),
            pltpu.VMEM((MLSTM_CHUNK, 2 * nc * LANES), F32),
        ],
        compiler_params=_params(("parallel", "parallel")),
        name="mlstm",
    )(qkvo, qkvo, qkvo, qkvo, gates, conv_w, conv_w, conv_b, conv_b, norm_w, sel)


def _cmul(ar, ai, br, bi):
    return ar * br - ai * bi, ar * bi + ai * br


def _s5pack_body(lr_ref, li_ref, ldt_ref, br_ref, bi_ref, cr_ref, ci_ref, tile_ref,
                 bm_ref, cm_ref, dm_ref, lpr_ref, lpi_ref):
    R = S5_LIFT
    cb = S5_GROUPS_PER_BLOCK * S5_GROUP
    half = S5_GROUPS_PER_BLOCK * S5_STATE
    lr = lr_ref[...]
    li = li_ref[...]
    dt = jnp.exp(ldt_ref[...])
    mag = jnp.exp(lr * dt)
    ang = li * dt
    lbr = mag * jnp.cos(ang)
    lbi = mag * jnp.sin(ang)
    nr = lbr - 1.0
    den = lr * lr + li * li
    cr = (nr * lr + lbi * li) / den
    ci = (lbi * lr - nr * li) / den
    pr, pi_ = _cmul(cr, ci, br_ref[...], bi_ref[...])
    qr, qi = cr_ref[...], ci_ref[...]

    row = lax.broadcasted_iota(jnp.int32, (cb, half), 0)
    lane = lax.broadcasted_iota(jnp.int32, (cb, half), 1)
    same_group = (row // S5_GROUP) == (lane // S5_STATE)
    tile = tile_ref[...]

    def blockdiag(a):
        return jnp.where(same_group, jnp.dot(a.astype(BF16), tile, preferred_element_type=F32), 0.0)

    def pair(re, im):
        return jnp.concatenate([blockdiag(re), blockdiag(im)], axis=1).astype(BF16)

    c0 = pair(qr, -qi)
    lag_rows = []
    for m in range(R):
        rows_m = pair(pr, pi_)
        bm_ref[(R - 1 - m) * cb:(R - m) * cb, :] = rows_m
        lag_rows.append(rows_m)
        pr, pi_ = _cmul(pr, pi_, lbr, lbi)
        qr, qi = _cmul(qr, qi, lbr, lbi)
        cm_ref[m * cb:(m + 1) * cb, :] = pair(qr, -qi)
    dlag = [lax.dot_general(lag_rows[m], c0, (((1,), (1,)), ((), ())), preferred_element_type=F32).astype(BF16)
            for m in range(R)]
    zero = jnp.zeros((cb, cb), BF16)
    for rp in range(R):
        for r in range(R):
            dm_ref[rp * cb:(rp + 1) * cb, r * cb:(r + 1) * cb] = dlag[r - rp] if r >= rp else zero

    wr, wi = lbr, lbi
    for _ in range(R - 1):
        wr, wi = _cmul(wr, wi, lbr, lbi)
    first = same_group & ((row % S5_GROUP) == 0)

    def lane_vector(a):
        t = sum(jnp.dot(part.astype(BF16), tile, preferred_element_type=F32) for part in _split3(a))
        return jnp.sum(jnp.where(first, t, 0.0), axis=0, keepdims=True)

    lpr_ref[...] = lane_vector(wr)
    lpi_ref[...] = lane_vector(wi)


def _s5_weights(lam_re, lam_im, log_dt, b_re, b_im, c_re, c_im):
    g, p = lam_re.shape
    hch = b_re.shape[-1]
    gpb = S5_GROUPS_PER_BLOCK
    ngb = g // gpb
    R = S5_LIFT
    cb, half = gpb * hch, gpb * p
    rep = lambda a: jnp.repeat(a, hch, axis=0)
    to_rows = lambda a: jnp.transpose(a, (0, 2, 1)).reshape(g * hch, p)
    ldt = jnp.broadcast_to(log_dt[:, None], (g, p))
    tile = jnp.asarray(np.tile(np.eye(p, dtype=np.float32), (1, gpb)), BF16)
    blk = pl.BlockSpec((cb, p), lambda i: (i, 0))
    return pl.pallas_call(
        _s5pack_body,
        grid=(ngb,),
        in_specs=[blk] * 7 + [pl.BlockSpec((p, half), lambda i: (0, 0))],
        out_specs=[pl.BlockSpec((None, R * cb, 2 * half), lambda i: (i, 0, 0)),
                   pl.BlockSpec((None, R * cb, 2 * half), lambda i: (i, 0, 0)),
                   pl.BlockSpec((None, R * cb, R * cb), lambda i: (i, 0, 0)),
                   pl.BlockSpec((None, 1, half), lambda i: (i, 0, 0)),
                   pl.BlockSpec((None, 1, half), lambda i: (i, 0, 0))],
        out_shape=[jax.ShapeDtypeStruct((ngb, R * cb, 2 * half), BF16),
                   jax.ShapeDtypeStruct((ngb, R * cb, 2 * half), BF16),
                   jax.ShapeDtypeStruct((ngb, R * cb, R * cb), BF16),
                   jax.ShapeDtypeStruct((ngb, 1, half), F32),
                   jax.ShapeDtypeStruct((ngb, 1, half), F32)],
        name="s5pack",
    )(rep(lam_re), rep(lam_im), rep(ldt), to_rows(b_re), to_rows(b_im),
      c_re.reshape(g * hch, p), c_im.reshape(g * hch, p), tile)


def _s5_body(u_ref, bm_ref, cm_ref, dm_ref, lr_ref, li_ref, dsk_ref, wglu_ref, bglu_ref, o_ref,
             usb_ref, sb_ref, y_ref, st_ref, *, ts, ngb):
    R = S5_LIFT
    nk = ts // R
    rows = nk * SUBLANES
    half = S5_GROUPS_PER_BLOCK * S5_STATE
    cb = S5_GROUPS_PER_BLOCK * S5_GROUP

    @pl.when(pl.program_id(0) == 0)
    def _():
        st_ref[...] = jnp.zeros_like(st_ref)

    def srow(s):
        return (s % R) * rows + (s // R) * SUBLANES

    for s0 in range(0, ts, SUBLANES):
        blk = jnp.swapaxes(u_ref[:, s0:s0 + SUBLANES, :], 0, 1)
        for s in range(s0, s0 + SUBLANES):
            usb_ref[srow(s):srow(s) + SUBLANES, :] = blk[s - s0]

    def lhs(gb):
        return jnp.concatenate([usb_ref[r * rows:(r + 1) * rows, gb * cb:(gb + 1) * cb] for r in range(R)],
                               axis=1).astype(BF16)

    def project_in(gb):
        sb_ref[gb % 2, SUBLANES:, :] = jnp.dot(lhs(gb), bm_ref[gb], preferred_element_type=F32)

    project_in(0)
    for gb in range(ngb):
        if gb + 1 < ngb:
            project_in(gb + 1)
        buf = sb_ref.at[gb % 2]
        lrb = jnp.broadcast_to(lr_ref[gb], (SUBLANES, half))
        lib = jnp.broadcast_to(li_ref[gb], (SUBLANES, half))
        xr = st_ref[gb, :, 0:half]
        xi = st_ref[gb, :, half:2 * half]
        buf[0:SUBLANES, 0:half] = xr
        buf[0:SUBLANES, half:2 * half] = xi
        for k in range(nk):
            rs = slice((k + 1) * SUBLANES, (k + 2) * SUBLANES)
            nr = lrb * xr - lib * xi + buf[rs, 0:half]
            ni = lrb * xi + lib * xr + buf[rs, half:2 * half]
            buf[rs, 0:half] = nr
            buf[rs, half:2 * half] = ni
            xr, xi = nr, ni
        st_ref[gb, :, 0:half] = xr
        st_ref[gb, :, half:2 * half] = xi

        yo = lax.dot_general(buf[0:rows, :].astype(BF16), cm_ref[gb], (((1,), (1,)), ((), ())),
                             preferred_element_type=F32)
        yd = jnp.dot(lhs(gb), dm_ref[gb], preferred_element_type=F32)
        for r in range(R):
            ug = usb_ref[r * rows:(r + 1) * rows, gb * cb:(gb + 1) * cb]
            yg = yo[:, r * cb:(r + 1) * cb] + yd[:, r * cb:(r + 1) * cb] + dsk_ref[:, gb * cb:(gb + 1) * cb] * ug
            y_ref[r * rows:(r + 1) * rows, gb * cb:(gb + 1) * cb] = jax.nn.gelu(yg)

    y = y_ref[...]
    z = jnp.dot(y.astype(BF16), wglu_ref[...], preferred_element_type=F32) + bglu_ref[...]
    usb_ref[...] = y * jax.nn.sigmoid(z)

    for s0 in range(0, ts, SUBLANES):
        blk = jnp.stack([usb_ref[srow(s):srow(s) + SUBLANES, :] for s in range(s0, s0 + SUBLANES)], axis=0)
        o_ref[:, s0:s0 + SUBLANES, :] = jnp.swapaxes(blk, 0, 1)


def _s5(u, bm, cm, dm, lpr, lpi, dsk, wglu, bglu, *, ts=64):
    batch, seq, width = u.shape
    ngb = bm.shape[0]
    half = S5_GROUPS_PER_BLOCK * S5_STATE
    assert batch == SUBLANES and ts % S5_LIFT == 0 and ts % SUBLANES == 0 and seq % ts == 0
    body = functools.partial(_s5_body, ts=ts, ngb=ngb)
    full = lambda a: pl.BlockSpec(a.shape, lambda i, n=a.ndim: (0,) * n)
    return pl.pallas_call(
        body,
        grid=(seq // ts,),
        in_specs=[pl.BlockSpec((batch, ts, width), lambda i: (0, i, 0)),
                  full(bm), full(cm), full(dm), full(lpr), full(lpi), full(dsk), full(wglu), full(bglu)],
        out_specs=pl.BlockSpec((batch, ts, width), lambda i: (0, i, 0)),
        out_shape=jax.ShapeDtypeStruct((batch, seq, width), F32),
        scratch_shapes=[
            pltpu.VMEM((ts * batch, width), F32),
            pltpu.VMEM((2, (ts // S5_LIFT + 1) * batch, 2 * half), F32),
            pltpu.VMEM((ts * batch, width), F32),
            pltpu.VMEM((ngb, batch, 2 * half), F32),
        ],
        compiler_params=_params(("arbitrary",)),
        name="s5",
    )(u, bm, cm, dm, lpr, lpi, dsk, wglu, bglu)


def _outproj_body(x_ref, hm_ref, hs_ref, wa_ref, wb_ref, o_ref):
    acc = jnp.dot(hm_ref[...], wa_ref[...], preferred_element_type=F32)
    acc = acc + jnp.dot(hs_ref[...].astype(BF16), wb_ref[...], preferred_element_type=F32)
    o_ref[...] = x_ref[...] + acc


def _outproj(x, hm, hs, wo, *, tm=512):
    t, d = x.shape
    w = hm.shape[1]
    return pl.pallas_call(
        _outproj_body,
        grid=(t // tm,),
        in_specs=[
            pl.BlockSpec((tm, d), lambda i: (i, 0)),
            pl.BlockSpec((tm, w), lambda i: (i, 0)),
            pl.BlockSpec((tm, w), lambda i: (i, 0)),
            pl.BlockSpec((w, d), lambda i: (0, 0)),
            pl.BlockSpec((hs.shape[1], d), lambda i: (w // hs.shape[1], 0)),
        ],
        out_specs=pl.BlockSpec((tm, d), lambda i: (i, 0)),
        out_shape=jax.ShapeDtypeStruct((t, d), F32),
        compiler_params=_params(("parallel",)),
        name="outproj",
    )(x, hm, hs, wo, wo)


def kernel(x, ffn1_norm, ffn1_w1, ffn1_w3, ffn1_w2, mix_norm, w_in, conv_w, conv_b, b_i, b_f, mlstm_norm, lam_re, lam_im, log_dt, b_re, b_im, c_re, c_im, d_skip, w_glu, b_glu, w_out, ffn2_norm, ffn2_w1, ffn2_w3, ffn2_w2, final_norm):
    batch, seq, d = x.shape
    depth = ffn1_norm.shape[0]
    nh = MLSTM_HEADS
    qk_cols = conv_w.shape[-1]
    mw = mlstm_norm.shape[-1]
    gate0 = qk_cols + 2 * mw
    sw = w_glu.shape[-1]
    row = lambda a: a.reshape(1, -1).astype(F32)
    xt = x.reshape(batch * seq, d)
    gfin = row(final_norm)
    w_in_t = jnp.swapaxes(w_in, 1, 2)
    for l in range(depth):
        jobs = dict(layer=l, cast_tiles=(ffn2_w1, ffn2_w3, ffn2_w2), w_in_t=w_in_t, gate0=gate0, gate_w=2 * nh)
        x_head, head = _ffn(xt, row(ffn1_norm[l]), ffn1_w1, ffn1_w3, ffn1_w2, gfin, final_norm=False, tf=256,
                            n_tiles=1, own_f32=True, **jobs)
        xt, (w1b, w3b, w2b, w_main) = _ffn(xt, row(ffn1_norm[l]), head[0], head[1], head[2], gfin, final_norm=False,
                                           tile_lo=1, carry=(x_head,) + tuple(head[3:]), **jobs)

        wg_t = w_in_t[l, gate0:gate0 + 2 * nh, :].astype(BF16)
        bg = jnp.concatenate([b_i[l], b_f[l]]).reshape(2 * nh, 1).astype(F32)
        (qkvo, u, gates), (wo, wglu) = _inproj(xt, row(mix_norm[l]), w_main, wg_t, bg, layer=l,
                                               cast_rows=(w_out, w_glu))

        hm = _mlstm(qkvo, gates, conv_w[l].astype(F32), row(conv_b[l]), row(mlstm_norm[l]), batch=batch, seq=seq)

        bm, cm, dm, lpr, lpi = _s5_weights(lam_re[l], lam_im[l], log_dt[l], b_re[l], b_im[l], c_re[l], c_im[l])
        hs = _s5(u.reshape(batch, seq, sw), bm, cm, dm, lpr, lpi, row(d_skip[l]), wglu, row(b_glu[l]))

        xt = _outproj(xt, hm, hs.reshape(batch * seq, sw), wo)

        xt, _ = _ffn(xt, row(ffn2_norm[l]), w1b, w3b, w2b, gfin, final_norm=(l == depth - 1))
    return xt.reshape(batch, seq, d)
```

```python
import functools

import jax
import jax.numpy as jnp
import numpy as np
from jax import lax
from jax.experimental import pallas as pl
from jax.experimental.pallas import tpu as pltpu

F32 = jnp.float32
BF16 = jnp.bfloat16

EPS = 1e-6
MLSTM_HEADS = 4
S5_GROUP = 16
S5_STATE = 64
S5_GROUPS_PER_BLOCK = 8
S5_LIFT = 2
LANES = 128
SUBLANES = 8
MLSTM_CHUNK = 256
TR_SPLIT = 4
VMEM_LIMIT_BYTES = 58 * 1024 * 1024


def _rms(x, g):
    return x * lax.rsqrt(jnp.mean(x * x, axis=-1, keepdims=True) + EPS) * g


def _params(sem):
    return pltpu.CompilerParams(dimension_semantics=sem, vmem_limit_bytes=VMEM_LIMIT_BYTES)


def _ffn_body(*refs, n_ff, final_norm, own_f32, n_tile, tr_kj, tile_lo, gate0, gate_w, n_carry, fuse_prologue):
    x_ref, g_ref, w1_ref, w3_ref, w2_ref, gf_ref = refs[:6]
    n_tr = 2 if tr_kj else 0
    n_in = 6 + n_tile + n_tr + n_carry
    tile_src = refs[6:6 + n_tile]
    o_ref = refs[n_in]
    n_own = 3 if own_f32 else 0
    own_dst = refs[n_in + 1:n_in + 1 + n_own]
    tile_dst = refs[n_in + 1 + n_own:n_in + 1 + n_own + n_tile]
    xn_ref = refs[-1]
    j = pl.program_id(1)

    def swiglu_step(xn):
        w1, w3, w2 = w1_ref[...], w3_ref[...], w2_ref[...]
        if own_f32:
            w1, w3, w2 = w1.astype(BF16), w3.astype(BF16), w2.astype(BF16)
            own_dst[0][...] = w1
            own_dst[1][...] = w3
            own_dst[2][...] = w2
        a = jnp.dot(xn, w1, preferred_element_type=F32)
        b = jnp.dot(xn, w3, preferred_element_type=F32)
        h = (a * jax.nn.sigmoid(a)) * (b * 0.5)
        return jnp.dot(h.astype(BF16), w2, preferred_element_type=F32)

    if fuse_prologue:
        @pl.when(j == 0)
        def _():
            x = x_ref[...]
            xn = _rms(x, g_ref[...]).astype(BF16)
            xn_ref[...] = xn
            o_ref[...] = x + swiglu_step(xn)

        @pl.when(j > 0)
        def _():
            o_ref[...] += swiglu_step(xn_ref[...])
    else:
        @pl.when(j == 0)
        def _():
            x = x_ref[...]
            xn_ref[...] = _rms(x, g_ref[...]).astype(BF16)
            o_ref[...] = x

        o_ref[...] += swiglu_step(xn_ref[...])

    if final_norm:
        @pl.when(j == n_ff - 1)
        def _():
            o_ref[...] = _rms(o_ref[...], gf_ref[...])

    for src, dst in zip(tile_src, tile_dst):
        dst[...] = src[...].astype(BF16)

    if tr_kj:
        ta_ref, tb_ref = refs[6 + n_tile], refs[7 + n_tile]
        main_ref = refs[-2]
        m = ((pl.program_id(0) + tile_lo) * tr_kj + j) // TR_SPLIT

        @pl.when((j < tr_kj) & (m < gate0 // LANES))
        def _():
            main_ref[...] = ta_ref[...].T.astype(BF16)

        @pl.when((j < tr_kj) & (m >= gate0 // LANES))
        def _():
            shifted = jnp.concatenate([ta_ref[gate_w:, :], tb_ref[:gate_w, :]], axis=0)
            main_ref[...] = shifted.T.astype(BF16)


def _ffn(x, g, w1, w3, w2, gf, *, final_norm, tm=1024, tf=512, tf_cast=512, layer=0, tile_lo=0, n_tiles=None,
         own_f32=False, carry=(), cast_tiles=(), w_in_t=None, gate0=None, gate_w=0):
    t, d = x.shape
    dff = w1.shape[-1] if w1.shape[-2] == d else w1.shape[-2]
    n_i, n_ff = t // tm, dff // tf
    n_tiles = n_i - tile_lo if n_tiles is None else n_tiles
    rb = d // n_i
    once = dict(pipeline_mode=pl.Buffered(1)) if n_tiles == 1 else {}
    tok = lambda i, j: (i + tile_lo, 0)
    in_specs = [pl.BlockSpec((tm, d), tok, **once), pl.BlockSpec((1, d), lambda i, j: (0, 0))]
    if own_f32:
        in_specs += [pl.BlockSpec((None, d, tf), lambda i, j: (layer, 0, j)),
                     pl.BlockSpec((None, d, tf), lambda i, j: (layer, 0, j)),
                     pl.BlockSpec((None, tf, d), lambda i, j: (layer, j, 0))]
    else:
        in_specs += [pl.BlockSpec((d, tf), lambda i, j: (0, j)),
                     pl.BlockSpec((d, tf), lambda i, j: (0, j)),
                     pl.BlockSpec((tf, d), lambda i, j: (j, 0))]
    in_specs.append(pl.BlockSpec((1, d), lambda i, j: (0, 0)))
    out_specs = [pl.BlockSpec((tm, d), tok, **once)]
    out_shape = [jax.ShapeDtypeStruct((t, d), F32)]
    if own_f32:
        out_specs += [pl.BlockSpec((d, tf), lambda i, j: (0, j)),
                      pl.BlockSpec((d, tf), lambda i, j: (0, j)),
                      pl.BlockSpec((tf, d), lambda i, j: (j, 0))]
        out_shape += [jax.ShapeDtypeStruct((d, dff), BF16), jax.ShapeDtypeStruct((d, dff), BF16),
                      jax.ShapeDtypeStruct((dff, d), BF16)]
    cj = lambda j: (j * tf) // tf_cast
    for w in cast_tiles:
        if w.shape[1:] == (d, dff):
            in_specs.append(pl.BlockSpec((None, rb, tf_cast), lambda i, j: (layer, i + tile_lo, cj(j))))
            out_specs.append(pl.BlockSpec((rb, tf_cast), lambda i, j: (i + tile_lo, cj(j))))
        else:
            assert w.shape[1:] == (dff, d) and rb % LANES == 0
            in_specs.append(pl.BlockSpec((None, tf_cast, rb), lambda i, j: (layer, cj(j), i + tile_lo)))
            out_specs.append(pl.BlockSpec((tf_cast, rb), lambda i, j: (cj(j), i + tile_lo)))
        out_shape.append(jax.ShapeDtypeStruct(w.shape[1:], BF16))
    tr_kj, tr_in = 0, ()
    if w_in_t is not None:
        n_main = w_in_t.shape[1] - gate_w
        sp = TR_SPLIT
        assert w_in_t.shape[2] == d and gate0 % LANES == 0 and n_main % (LANES * n_i) == 0 and d % (sp * LANES) == 0
        tr_kj = sp * n_main // (LANES * n_i)
        assert tr_kj <= n_ff
        job = lambda i, j: (i + tile_lo) * tr_kj + jnp.minimum(j, tr_kj - 1)
        in_specs += [pl.BlockSpec((None, LANES, d // sp), lambda i, j: (layer, job(i, j) // sp, job(i, j) % sp)),
                     pl.BlockSpec((None, LANES, d // sp), lambda i, j: (layer, job(i, j) // sp + 1, job(i, j) % sp))]
        out_specs.append(pl.BlockSpec((d // sp, LANES), lambda i, j: (job(i, j) % sp, job(i, j) // sp)))
        out_shape.append(jax.ShapeDtypeStruct((d, n_main), BF16))
        tr_in = (w_in_t, w_in_t)
    aliases = {}
    if carry:
        n_before = len(in_specs)
        keep = [0] + list(range(len(out_shape) - len(carry) + 1, len(out_shape)))
        assert len(keep) == len(carry) and not own_f32
        for k, (c, o) in enumerate(zip(carry, keep)):
            assert c.shape == out_shape[o].shape and c.dtype == out_shape[o].dtype
            in_specs.append(pl.BlockSpec(memory_space=pl.ANY))
            aliases[n_before + k] = o
    body = functools.partial(_ffn_body, n_ff=n_ff, final_norm=final_norm, own_f32=own_f32, n_tile=len(cast_tiles),
                             tr_kj=tr_kj, tile_lo=tile_lo, gate0=gate0, gate_w=gate_w, n_carry=len(carry),
                             fuse_prologue=not (cast_tiles or tr_kj))
    outs = pl.pallas_call(
        body,
        grid=(n_tiles, n_ff),
        in_specs=in_specs,
        out_specs=out_specs,
        out_shape=out_shape,
        input_output_aliases=aliases,
        scratch_shapes=[pltpu.VMEM((tm, d), BF16)],
        compiler_params=_params(("parallel", "arbitrary")),
        name=("ffn_final" if final_norm else "ffn") + ("_head" if own_f32 else ""),
    )(x, g, w1, w3, w2, gf, *cast_tiles, *tr_in, *carry)
    return outs[0], outs[1:]


def _inproj_body(*refs, n_cast):
    x_ref, g_ref, w_ref, wg_ref, bg_ref = refs[:5]
    cast_src = refs[5:5 + n_cast]
    qkvo_ref, u_ref, gates_ref = refs[5 + n_cast:8 + n_cast]
    cast_dst = refs[8 + n_cast:8 + 2 * n_cast]
    hn_ref = refs[-1]
    j = pl.program_id(1)
    nh = MLSTM_HEADS

    @pl.when(j == 0)
    def _():
        hn = _rms(x_ref[...], g_ref[...]).astype(BF16)
        hn_ref[...] = hn
        gt = lax.dot_general(wg_ref[...], hn, (((1,), (1,)), ((), ())), preferred_element_type=F32)
        gt = gt + bg_ref[...]
        for h in range(nh):
            gates_ref[h] = jnp.concatenate([gt[h:h + 1, :], gt[nh + h:nh + h + 1, :]], axis=0)

    @pl.when(j < 3)
    def _():
        qkvo_ref[...] = jnp.dot(hn_ref[...], w_ref[...], preferred_element_type=F32).astype(BF16)

    @pl.when(j == 3)
    def _():
        u_ref[...] = jnp.dot(hn_ref[...], w_ref[...], preferred_element_type=F32)

    for src, dst in zip(cast_src, cast_dst):
        dst[...] = src[...].astype(BF16)


def _inproj(x, g, w_main, wg_t, bg, *, tm=1024, layer=0, cast_rows=()):
    t, d = x.shape
    w = w_main.shape[1] // 4
    n_i, n_j = t // tm, 4
    in_specs = [
        pl.BlockSpec((tm, d), lambda i, j: (i, 0)),
        pl.BlockSpec((1, d), lambda i, j: (0, 0)),
        pl.BlockSpec((d, w), lambda i, j: (0, j)),
        pl.BlockSpec((2 * MLSTM_HEADS, d), lambda i, j: (0, 0)),
        pl.BlockSpec((2 * MLSTM_HEADS, 1), lambda i, j: (0, 0)),
    ]
    out_specs = [pl.BlockSpec((tm, w), lambda i, j: (i, jnp.minimum(j, 2))),
                 pl.BlockSpec((tm, w), lambda i, j: (i, 0)),
                 pl.BlockSpec((MLSTM_HEADS, 2, tm), lambda i, j: (0, 0, i))]
    out_shape = [
        jax.ShapeDtypeStruct((t, 3 * w), BF16),
        jax.ShapeDtypeStruct((t, w), F32),
        jax.ShapeDtypeStruct((MLSTM_HEADS, 2, t), F32),
    ]
    for a in cast_rows:
        n_rows, n_cols = a.shape[1:]
        rr = n_rows // (n_i * n_j)
        assert n_rows % (n_i * n_j) == 0 and rr % 16 == 0
        in_specs.append(pl.BlockSpec((None, rr, n_cols), lambda i, j: (layer, i * n_j + j, 0)))
        out_specs.append(pl.BlockSpec((rr, n_cols), lambda i, j: (i * n_j + j, 0)))
        out_shape.append(jax.ShapeDtypeStruct((n_rows, n_cols), BF16))
    outs = pl.pallas_call(
        functools.partial(_inproj_body, n_cast=len(cast_rows)),
        grid=(n_i, n_j),
        in_specs=in_specs,
        out_specs=out_specs,
        out_shape=out_shape,
        scratch_shapes=[pltpu.VMEM((tm, d), BF16)],
        compiler_params=_params(("parallel", "arbitrary")),
        name="inproj",
    )(x, g, w_main, wg_t, bg, *cast_rows)
    return outs[:3], outs[3:]


def _split3(x):
    hi = x.astype(BF16).astype(F32)
    r1 = x - hi
    mid = r1.astype(BF16).astype(F32)
    lo = (r1 - mid).astype(BF16).astype(F32)
    return [hi, mid, lo]


def _log_sigmoid(x):
    return jnp.minimum(x, 0.0) - jnp.log(1.0 + jnp.exp(-jnp.abs(x)))


def _mlstm_body(q_ref, k_ref, v_ref, og_ref, g_ref, cwq_ref, cwk_ref, cbq_ref, cbk_ref, nw_ref,
                sel_ref, out_ref, xp_ref, qs_ref, ks_ref, cols_ref, *, seq, dk, dv):
    L = MLSTM_CHUNK
    nc = seq // L

    def conv_silu(x_ref, w_ref, b_ref, dst_ref, scale):
        xp_ref[0:SUBLANES, :] = jnp.zeros((SUBLANES, dk), F32)
        xp_ref[SUBLANES:, :] = x_ref[...].astype(F32)
        w = w_ref[...]
        acc = b_ref[...] + xp_ref[pl.ds(SUBLANES, seq), :] * w[3:4, :]
        for d in (1, 2, 3):
            acc = acc + xp_ref[pl.ds(SUBLANES - d, seq), :] * w[3 - d:4 - d, :]
        y = acc * jax.nn.sigmoid(acc)
        if scale != 1.0:
            y = y * scale
        dst_ref[...] = y.astype(BF16)

    conv_silu(q_ref, cwq_ref, cbq_ref, qs_ref, dk ** -0.5)
    conv_silu(k_ref, cwk_ref, cbk_ref, ks_ref, 1.0)

    rows = lax.broadcasted_iota(jnp.int32, (L, L), 0)
    cols = lax.broadcasted_iota(jnp.int32, (L, L), 1)
    causal = cols <= rows
    tri = jnp.where(rows <= cols, 1.0, 0.0).astype(BF16)

    ncp = -(-nc // SUBLANES) * SUBLANES
    pad = [jnp.zeros((ncp - nc, L), F32)] if ncp > nc else []
    g = g_ref[...]
    li_all = jnp.concatenate([g[0:1, c * L:(c + 1) * L] for c in range(nc)] + pad, axis=0)
    lf_all = jnp.concatenate([_log_sigmoid(g[1:2, c * L:(c + 1) * L]) for c in range(nc)] + pad, axis=0)
    parts = jnp.concatenate(_split3(lf_all), axis=0).astype(BF16)
    cs = jnp.dot(parts, tri, preferred_element_type=F32)
    b_all = cs[0:ncp] + cs[ncp:2 * ncp] + cs[2 * ncp:3 * ncp]
    d_all = li_all - b_all
    at = jnp.concatenate(_split3(b_all) + _split3(d_all), axis=0).astype(BF16)
    cols_ref[...] = lax.dot_general(at, sel_ref[...], (((0,), (0,)), ((), ())), preferred_element_type=F32)

    ext = 2 * LANES
    rep = lambda a: jnp.concatenate([a] * (dv // LANES), axis=1)
    ones_l = jnp.ones((L, ext), BF16)
    mean_w = jnp.full((dv, LANES), 1.0 / dv, BF16)
    ct_ext = jnp.zeros((dk, dv + ext), F32)
    m_prev = jnp.zeros((1, 1), F32)
    nw = nw_ref[...]

    for c in range(nc):
        r0 = c * L
        q = qs_ref[pl.ds(r0, L), :]
        k = ks_ref[pl.ds(r0, L), :]
        v_ext = jnp.concatenate([v_ref[pl.ds(r0, L), :], ones_l], axis=1)
        d_row = d_all[c:c + 1, :]
        b_end = b_all[c:c + 1, L - 1:L]
        b_c = cols_ref[:, c * LANES:(c + 1) * LANES]
        d_c = cols_ref[:, (nc + c) * LANES:(nc + c + 1) * LANES]

        cm_col = jnp.max(jnp.where(causal, d_row, -jnp.inf), axis=-1, keepdims=True)
        m_t = b_c + jnp.maximum(m_prev, cm_col)
        a_t = jnp.exp(b_c + m_prev - m_t)
        e_mt = jnp.exp(-m_t)
        arg = jnp.concatenate([b_c - m_t] * (L // LANES), axis=1) + d_row
        d_mat = jnp.where(causal, jnp.exp(arg), 0.0)

        qk = lax.dot_general(q, k, (((1,), (1,)), ((), ())), preferred_element_type=F32)
        s_b = (qk * d_mat).astype(BF16)
        sv = jnp.dot(s_b, v_ext, preferred_element_type=F32)
        qc = jnp.dot(q, ct_ext.astype(BF16), preferred_element_type=F32)
        num = sv[:, :dv] + rep(a_t) * qc[:, :dv]
        den = sv[:, dv:dv + LANES] + a_t * qc[:, dv:dv + LANES]
        inv = 1.0 / jnp.maximum(jnp.abs(den), e_mt)
        h = num * rep(inv)
        msq = jnp.dot((h * h).astype(BF16), mean_w, preferred_element_type=F32)
        hn = h * rep(lax.rsqrt(msq + EPS)) * nw
        og = og_ref[pl.ds(r0, L), :].astype(F32)
        out_ref[pl.ds(r0, L), :] = (hn * jax.nn.sigmoid(og)).astype(BF16)

        w_end = b_end + d_row
        m_loc = jnp.max(w_end, axis=-1, keepdims=True)
        e_c = jnp.exp(b_end + d_c - m_loc)
        ke = (k.astype(F32) * e_c).astype(BF16)
        c_loc = lax.dot_general(ke, v_ext, (((0,), (0,)), ((), ())), preferred_element_type=F32)
        m_new = jnp.maximum(b_end + m_prev, m_loc)
        ct_ext = jnp.exp(b_end + m_prev - m_new) * ct_ext + jnp.exp(m_loc - m_new) * c_loc
        m_prev = m_new


def _mlstm_selector(nc):
    ncp = -(-nc // SUBLANES) * SUBLANES
    sel = np.zeros((6 * ncp, 2 * nc * LANES), np.float32)
    for kind in range(2):
        for part in range(3):
            for c in range(nc):
                sel[(kind * 3 + part) * ncp + c, (kind * nc + c) * LANES:(kind * nc + c + 1) * LANES] = 1.0
    return jnp.asarray(sel, BF16)


def _mlstm(qkvo, gates, conv_w, conv_b, norm_w, *, batch, seq):
    t = qkvo.shape[0]
    w = qkvo.shape[1] // 3
    nh = MLSTM_HEADS
    dk = w // (2 * nh)
    dv = w // nh
    assert dk == LANES and dv % LANES == 0 and seq % MLSTM_CHUNK == 0 and MLSTM_CHUNK % LANES == 0
    nc = seq // MLSTM_CHUNK
    sel = _mlstm_selector(nc)
    body = functools.partial(_mlstm_body, seq=seq, dk=dk, dv=dv)
    return pl.pallas_call(
        body,
        grid=(batch, nh),
        in_specs=[
            pl.BlockSpec((seq, dk), lambda b, h: (b, h)),
            pl.BlockSpec((seq, dk), lambda b, h: (b, nh + h)),
            pl.BlockSpec((seq, dv), lambda b, h: (b, nh + h)),
            pl.BlockSpec((seq, dv), lambda b, h: (b, 2 * nh + h)),
            pl.BlockSpec((None, 2, seq), lambda b, h: (h, 0, b)),
            pl.BlockSpec((conv_w.shape[0], dk), lambda b, h: (0, h)),
            pl.BlockSpec((conv_w.shape[0], dk), lambda b, h: (0, nh + h)),
            pl.BlockSpec((1, dk), lambda b, h: (0, h)),
            pl.BlockSpec((1, dk), lambda b, h: (0, nh + h)),
            pl.BlockSpec((1, dv), lambda b, h: (0, h)),
            pl.BlockSpec(sel.shape, lambda b, h: (0, 0)),
        ],
        out_specs=pl.BlockSpec((seq, dv), lambda b, h: (b, h)),
        out_shape=jax.ShapeDtypeStruct((t, nh * dv), BF16),
        scratch_shapes=[
            pltpu.VMEM((seq + SUBLANES, dk), F32),
            pltpu.VMEM((seq, dk), BF16),
            pltpu.VMEM((seq, dk), BF16),
            pltpu.VMEM((MLSTM_CHUNK, 2 * nc * LANES), F32),
        ],
        compiler_params=_params(("parallel", "parallel")),
        name="mlstm",
    )(qkvo, qkvo, qkvo, qkvo, gates, conv_w, conv_w, conv_b, conv_b, norm_w, sel)


def _cmul(ar, ai, br, bi):
    return ar * br - ai * bi, ar * bi + ai * br


def _s5pack_body(lr_ref, li_ref, ldt_ref, br_ref, bi_ref, cr_ref, ci_ref, tile_ref,
                 bm_ref, cm_ref, dm_ref, lpr_ref, lpi_ref):
    R = S5_LIFT
    cb = S5_GROUPS_PER_BLOCK * S5_GROUP
    half = S5_GROUPS_PER_BLOCK * S5_STATE
    lr = lr_ref[...]
    li = li_ref[...]
    dt = jnp.exp(ldt_ref[...])
    mag = jnp.exp(lr * dt)
    ang = li * dt
    lbr = mag * jnp.cos(ang)
    lbi = mag * jnp.sin(ang)
    nr = lbr - 1.0
    den = lr * lr + li * li
    cr = (nr * lr + lbi * li) / den
    ci = (lbi * lr - nr * li) / den
    pr, pi_ = _cmul(cr, ci, br_ref[...], bi_ref[...])
    qr, qi = cr_ref[...], ci_ref[...]

    row = lax.broadcasted_iota(jnp.int32, (cb, half), 0)
    lane = lax.broadcasted_iota(jnp.int32, (cb, half), 1)
    same_group = (row // S5_GROUP) == (lane // S5_STATE)
    tile = tile_ref[...]

    def blockdiag(a):
        return jnp.where(same_group, jnp.dot(a.astype(BF16), tile, preferred_element_type=F32), 0.0)

    def pair(re, im):
        return jnp.concatenate([blockdiag(re), blockdiag(im)], axis=1).astype(BF16)

    c0 = pair(qr, -qi)
    lag_rows = []
    for m in range(R):
        rows_m = pair(pr, pi_)
        bm_ref[(R - 1 - m) * cb:(R - m) * cb, :] = rows_m
        lag_rows.append(rows_m)
        pr, pi_ = _cmul(pr, pi_, lbr, lbi)
        qr, qi = _cmul(qr, qi, lbr, lbi)
        cm_ref[m * cb:(m + 1) * cb, :] = pair(qr, -qi)
    dlag = [lax.dot_general(lag_rows[m], c0, (((1,), (1,)), ((), ())), preferred_element_type=F32).astype(BF16)
            for m in range(R)]
    zero = jnp.zeros((cb, cb), BF16)
    for rp in range(R):
        for r in range(R):
            dm_ref[rp * cb:(rp + 1) * cb, r * cb:(r + 1) * cb] = dlag[r - rp] if r >= rp else zero

    wr, wi = lbr, lbi
    for _ in range(R - 1):
        wr, wi = _cmul(wr, wi, lbr, lbi)
    first = same_group & ((row % S5_GROUP) == 0)

    def lane_vector(a):
        t = sum(jnp.dot(part.astype(BF16), tile, preferred_element_type=F32) for part in _split3(a))
        return jnp.sum(jnp.where(first, t, 0.0), axis=0, keepdims=True)

    lpr_ref[...] = lane_vector(wr)
    lpi_ref[...] = lane_vector(wi)


def _s5_weights(lam_re, lam_im, log_dt, b_re, b_im, c_re, c_im):
    g, p = lam_re.shape
    hch = b_re.shape[-1]
    gpb = S5_GROUPS_PER_BLOCK
    ngb = g // gpb
    R = S5_LIFT
    cb, half = gpb * hch, gpb * p
    rep = lambda a: jnp.repeat(a, hch, axis=0)
    to_rows = lambda a: jnp.transpose(a, (0, 2, 1)).reshape(g * hch, p)
    ldt = jnp.broadcast_to(log_dt[:, None], (g, p))
    tile = jnp.asarray(np.tile(np.eye(p, dtype=np.float32), (1, gpb)), BF16)
    blk = pl.BlockSpec((cb, p), lambda i: (i, 0))
    return pl.pallas_call(
        _s5pack_body,
        grid=(ngb,),
        in_specs=[blk] * 7 + [pl.BlockSpec((p, half), lambda i: (0, 0))],
        out_specs=[pl.BlockSpec((None, R * cb, 2 * half), lambda i: (i, 0, 0)),
                   pl.BlockSpec((None, R * cb, 2 * half), lambda i: (i, 0, 0)),
                   pl.BlockSpec((None, R * cb, R * cb), lambda i: (i, 0, 0)),
                   pl.BlockSpec((None, 1, half), lambda i: (i, 0, 0)),
                   pl.BlockSpec((None, 1, half), lambda i: (i, 0, 0))],
        out_shape=[jax.ShapeDtypeStruct((ngb, R * cb, 2 * half), BF16),
                   jax.ShapeDtypeStruct((ngb, R * cb, 2 * half), BF16),
                   jax.ShapeDtypeStruct((ngb, R * cb, R * cb), BF16),
                   jax.ShapeDtypeStruct((ngb, 1, half), F32),
                   jax.ShapeDtypeStruct((ngb, 1, half), F32)],
        name="s5pack",
    )(rep(lam_re), rep(lam_im), rep(ldt), to_rows(b_re), to_rows(b_im),
      c_re.reshape(g * hch, p), c_im.reshape(g * hch, p), tile)


def _s5_body(u_ref, bm_ref, cm_ref, dm_ref, lr_ref, li_ref, dsk_ref, wglu_ref, bglu_ref, o_ref,
             usb_ref, sb_ref, y_ref, st_ref, *, ts, ngb):
    R = S5_LIFT
    nk = ts // R
    rows = nk * SUBLANES
    half = S5_GROUPS_PER_BLOCK * S5_STATE
    cb = S5_GROUPS_PER_BLOCK * S5_GROUP

    @pl.when(pl.program_id(0) == 0)
    def _():
        st_ref[...] = jnp.zeros_like(st_ref)

    def srow(s):
        return (s % R) * rows + (s // R) * SUBLANES

    for s0 in range(0, ts, SUBLANES):
        blk = jnp.swapaxes(u_ref[:, s0:s0 + SUBLANES, :], 0, 1)
        for s in range(s0, s0 + SUBLANES):
            usb_ref[srow(s):srow(s) + SUBLANES, :] = blk[s - s0]

    def lhs(gb):
        return jnp.concatenate([usb_ref[r * rows:(r + 1) * rows, gb * cb:(gb + 1) * cb] for r in range(R)],
                               axis=1).astype(BF16)

    def project_in(gb):
        sb_ref[gb % 2, SUBLANES:, :] = jnp.dot(lhs(gb), bm_ref[gb], preferred_element_type=F32)

    project_in(0)
    for gb in range(ngb):
        if gb + 1 < ngb:
            project_in(gb + 1)
        buf = sb_ref.at[gb % 2]
        lrb = jnp.broadcast_to(lr_ref[gb], (SUBLANES, half))
        lib = jnp.broadcast_to(li_ref[gb], (SUBLANES, half))
        xr = st_ref[gb, :, 0:half]
        xi = st_ref[gb, :, half:2 * half]
        buf[0:SUBLANES, 0:half] = xr
        buf[0:SUBLANES, half:2 * half] = xi
        for k in range(nk):
            rs = slice((k + 1) * SUBLANES, (k + 2) * SUBLANES)
            nr = lrb * xr - lib * xi + buf[rs, 0:half]
            ni = lrb * xi + lib * xr + buf[rs, half:2 * half]
            buf[rs, 0:half] = nr
            buf[rs, half:2 * half] = ni
            xr, xi = nr, ni
        st_ref[gb, :, 0:half] = xr
        st_ref[gb, :, half:2 * half] = xi

        yo = lax.dot_general(buf[0:rows, :].astype(BF16), cm_ref[gb], (((1,), (1,)), ((), ())),
                             preferred_element_type=F32)
        yd = jnp.dot(lhs(gb), dm_ref[gb], preferred_element_type=F32)
        for r in range(R):
            ug = usb_ref[r * rows:(r + 1) * rows, gb * cb:(gb + 1) * cb]
            yg = yo[:, r * cb:(r + 1) * cb] + yd[:, r * cb:(r + 1) * cb] + dsk_ref[:, gb * cb:(gb + 1) * cb] * ug
            y_ref[r * rows:(r + 1) * rows, gb * cb:(gb + 1) * cb] = jax.nn.gelu(yg)

    y = y_ref[...]
    z = jnp.dot(y.astype(BF16), wglu_ref[...], preferred_element_type=F32) + bglu_ref[...]
    usb_ref[...] = y * jax.nn.sigmoid(z)

    for s0 in range(0, ts, SUBLANES):
        blk = jnp.stack([usb_ref[srow(s):srow(s) + SUBLANES, :] for s in range(s0, s0 + SUBLANES)], axis=0)
        o_ref[:, s0:s0 + SUBLANES, :] = jnp.swapaxes(blk, 0, 1)


def _s5(u, bm, cm, dm, lpr, lpi, dsk, wglu, bglu, *, ts=128):
    batch, seq, width = u.shape
    ngb = bm.shape[0]
    half = S5_GROUPS_PER_BLOCK * S5_STATE
    assert batch == SUBLANES and ts % S5_LIFT == 0 and ts % SUBLANES == 0 and seq % ts == 0
    body = functools.partial(_s5_body, ts=ts, ngb=ngb)
    full = lambda a: pl.BlockSpec(a.shape, lambda i, n=a.ndim: (0,) * n, pipeline_mode=pl.Buffered(1))
    return pl.pallas_call(
        body,
        grid=(seq // ts,),
        in_specs=[pl.BlockSpec((batch, ts, width), lambda i: (0, i, 0)),
                  full(bm), full(cm), full(dm), full(lpr), full(lpi), full(dsk), full(wglu), full(bglu)],
        out_specs=pl.BlockSpec((batch, ts, width), lambda i: (0, i, 0)),
        out_shape=jax.ShapeDtypeStruct((batch, seq, width), F32),
        scratch_shapes=[
            pltpu.VMEM((ts * batch, width), F32),
            pltpu.VMEM((2, (ts // S5_LIFT + 1) * batch, 2 * half), F32),
            pltpu.VMEM((ts * batch, width), F32),
            pltpu.VMEM((ngb, batch, 2 * half), F32),
        ],
        compiler_params=_params(("arbitrary",)),
        name="s5",
    )(u, bm, cm, dm, lpr, lpi, dsk, wglu, bglu)


def _outproj_body(x_ref, hm_ref, hs_ref, wa_ref, wb_ref, o_ref):
    acc = jnp.dot(hm_ref[...], wa_ref[...], preferred_element_type=F32)
    acc = acc + jnp.dot(hs_ref[...].astype(BF16), wb_ref[...], preferred_element_type=F32)
    o_ref[...] = x_ref[...] + acc


def _outproj(x, hm, hs, wo, *, tm=512):
    t, d = x.shape
    w = hm.shape[1]
    return pl.pallas_call(
        _outproj_body,
        grid=(t // tm,),
        in_specs=[
            pl.BlockSpec((tm, d), lambda i: (i, 0)),
            pl.BlockSpec((tm, w), lambda i: (i, 0)),
            pl.BlockSpec((tm, w), lambda i: (i, 0)),
            pl.BlockSpec((w, d), lambda i: (0, 0)),
            pl.BlockSpec((hs.shape[1], d), lambda i: (w // hs.shape[1], 0)),
        ],
        out_specs=pl.BlockSpec((tm, d), lambda i: (i, 0)),
        out_shape=jax.ShapeDtypeStruct((t, d), F32),
        compiler_params=_params(("parallel",)),
        name="outproj",
    )(x, hm, hs, wo, wo)


def kernel(x, ffn1_norm, ffn1_w1, ffn1_w3, ffn1_w2, mix_norm, w_in, conv_w, conv_b, b_i, b_f, mlstm_norm, lam_re, lam_im, log_dt, b_re, b_im, c_re, c_im, d_skip, w_glu, b_glu, w_out, ffn2_norm, ffn2_w1, ffn2_w3, ffn2_w2, final_norm):
    batch, seq, d = x.shape
    depth = ffn1_norm.shape[0]
    nh = MLSTM_HEADS
    qk_cols = conv_w.shape[-1]
    mw = mlstm_norm.shape[-1]
    gate0 = qk_cols + 2 * mw
    sw = w_glu.shape[-1]
    row = lambda a: a.reshape(1, -1).astype(F32)
    xt = x.reshape(batch * seq, d)
    gfin = row(final_norm)
    w_in_t = jnp.swapaxes(w_in, 1, 2)
    for l in range(depth):
        jobs = dict(layer=l, cast_tiles=(ffn2_w1, ffn2_w3, ffn2_w2), w_in_t=w_in_t, gate0=gate0, gate_w=2 * nh)
        x_head, head = _ffn(xt, row(ffn1_norm[l]), ffn1_w1, ffn1_w3, ffn1_w2, gfin, final_norm=False, tf=256,
                            n_tiles=1, own_f32=True, **jobs)
        xt, (w1b, w3b, w2b, w_main) = _ffn(xt, row(ffn1_norm[l]), head[0], head[1], head[2], gfin, final_norm=False,
                                           tile_lo=1, carry=(x_head,) + tuple(head[3:]), **jobs)

        wg_t = w_in_t[l, gate0:gate0 + 2 * nh, :].astype(BF16)
        bg = jnp.concatenate([b_i[l], b_f[l]]).reshape(2 * nh, 1).astype(F32)
        (qkvo, u, gates), (wo, wglu) = _inproj(xt, row(mix_norm[l]), w_main, wg_t, bg, layer=l,
                                               cast_rows=(w_out, w_glu))

        hm = _mlstm(qkvo, gates, conv_w[l].astype(F32), row(conv_b[l]), row(mlstm_norm[l]), batch=batch, seq=seq)

        bm, cm, dm, lpr, lpi = _s5_weights(lam_re[l], lam_im[l], log_dt[l], b_re[l], b_im[l], c_re[l], c_im[l])
        hs = _s5(u.reshape(batch, seq, sw), bm, cm, dm, lpr, lpi, row(d_skip[l]), wglu, row(b_glu[l]))

        xt = _outproj(xt, hm, hs.reshape(batch * seq, sw), wo)

        xt, _ = _ffn(xt, row(ffn2_norm[l]), w1b, w3b, w2b, gfin, final_norm=(l == depth - 1))
    return xt.reshape(batch, seq, d)
```

```python
import functools

import jax
import jax.numpy as jnp
import numpy as np
from jax import lax
from jax.experimental import pallas as pl
from jax.experimental.pallas import tpu as pltpu

F32 = jnp.float32
BF16 = jnp.bfloat16

EPS = 1e-6
MLSTM_HEADS = 4
S5_GROUP = 16
S5_STATE = 64
S5_GROUPS_PER_BLOCK = 8
S5_LIFT = 2
LANES = 128
SUBLANES = 8
MLSTM_CHUNK = 256
TR_SPLIT = 4
VMEM_LIMIT_BYTES = 58 * 1024 * 1024


def _rms(x, g):
    return x * lax.rsqrt(jnp.mean(x * x, axis=-1, keepdims=True) + EPS) * g


def _params(sem):
    return pltpu.CompilerParams(dimension_semantics=sem, vmem_limit_bytes=VMEM_LIMIT_BYTES)


def _ffn_body(*refs, n_ff, final_norm, own_f32, n_tile, tr_kj, tile_lo, gate0, gate_w, n_carry, fuse_prologue):
    x_ref, g_ref, w1_ref, w3_ref, w2_ref, gf_ref = refs[:6]
    n_tr = 2 if tr_kj else 0
    n_in = 6 + n_tile + n_tr + n_carry
    tile_src = refs[6:6 + n_tile]
    o_ref = refs[n_in]
    n_own = 3 if own_f32 else 0
    own_dst = refs[n_in + 1:n_in + 1 + n_own]
    tile_dst = refs[n_in + 1 + n_own:n_in + 1 + n_own + n_tile]
    xn_ref = refs[-1]
    j = pl.program_id(1)

    def swiglu_step(xn):
        w1, w3, w2 = w1_ref[...], w3_ref[...], w2_ref[...]
        if own_f32:
            w1, w3, w2 = w1.astype(BF16), w3.astype(BF16), w2.astype(BF16)
            own_dst[0][...] = w1
            own_dst[1][...] = w3
            own_dst[2][...] = w2
        a = jnp.dot(xn, w1, preferred_element_type=F32)
        b = jnp.dot(xn, w3, preferred_element_type=F32)
        h = (a * jax.nn.sigmoid(a)) * (b * 0.5)
        return jnp.dot(h.astype(BF16), w2, preferred_element_type=F32)

    if fuse_prologue:
        @pl.when(j == 0)
        def _():
            x = x_ref[...]
            xn = _rms(x, g_ref[...]).astype(BF16)
            xn_ref[...] = xn
            o_ref[...] = x + swiglu_step(xn)

        @pl.when(j > 0)
        def _():
            o_ref[...] += swiglu_step(xn_ref[...])
    else:
        @pl.when(j == 0)
        def _():
            x = x_ref[...]
            xn_ref[...] = _rms(x, g_ref[...]).astype(BF16)
            o_ref[...] = x

        o_ref[...] += swiglu_step(xn_ref[...])

    if final_norm:
        @pl.when(j == n_ff - 1)
        def _():
            o_ref[...] = _rms(o_ref[...], gf_ref[...])

    for src, dst in zip(tile_src, tile_dst):
        dst[...] = src[...].astype(BF16)

    if tr_kj:
        ta_ref, tb_ref = refs[6 + n_tile], refs[7 + n_tile]
        main_ref = refs[-2]
        m = ((pl.program_id(0) + tile_lo) * tr_kj + j) // TR_SPLIT

        @pl.when((j < tr_kj) & (m < gate0 // LANES))
        def _():
            main_ref[...] = ta_ref[...].T.astype(BF16)

        @pl.when((j < tr_kj) & (m >= gate0 // LANES))
        def _():
            shifted = jnp.concatenate([ta_ref[gate_w:, :], tb_ref[:gate_w, :]], axis=0)
            main_ref[...] = shifted.T.astype(BF16)


def _ffn(x, g, w1, w3, w2, gf, *, final_norm, tm=1024, tf=512, tf_cast=512, layer=0, tile_lo=0, n_tiles=None,
         own_f32=False, carry=(), cast_tiles=(), w_in_t=None, gate0=None, gate_w=0):
    t, d = x.shape
    dff = w1.shape[-1] if w1.shape[-2] == d else w1.shape[-2]
    n_i, n_ff = t // tm, dff // tf
    n_tiles = n_i - tile_lo if n_tiles is None else n_tiles
    rb = d // n_i
    once = dict(pipeline_mode=pl.Buffered(1)) if n_tiles == 1 else {}
    tok = lambda i, j: (i + tile_lo, 0)
    in_specs = [pl.BlockSpec((tm, d), tok, **once), pl.BlockSpec((1, d), lambda i, j: (0, 0))]
    if own_f32:
        in_specs += [pl.BlockSpec((None, d, tf), lambda i, j: (layer, 0, j)),
                     pl.BlockSpec((None, d, tf), lambda i, j: (layer, 0, j)),
                     pl.BlockSpec((None, tf, d), lambda i, j: (layer, j, 0))]
    else:
        in_specs += [pl.BlockSpec((d, tf), lambda i, j: (0, j)),
                     pl.BlockSpec((d, tf), lambda i, j: (0, j)),
                     pl.BlockSpec((tf, d), lambda i, j: (j, 0))]
    in_specs.append(pl.BlockSpec((1, d), lambda i, j: (0, 0)))
    out_specs = [pl.BlockSpec((tm, d), tok, **once)]
    out_shape = [jax.ShapeDtypeStruct((t, d), F32)]
    if own_f32:
        out_specs += [pl.BlockSpec((d, tf), lambda i, j: (0, j)),
                      pl.BlockSpec((d, tf), lambda i, j: (0, j)),
                      pl.BlockSpec((tf, d), lambda i, j: (j, 0))]
        out_shape += [jax.ShapeDtypeStruct((d, dff), BF16), jax.ShapeDtypeStruct((d, dff), BF16),
                      jax.ShapeDtypeStruct((dff, d), BF16)]
    cj = lambda j: (j * tf) // tf_cast
    for w in cast_tiles:
        if w.shape[1:] == (d, dff):
            in_specs.append(pl.BlockSpec((None, rb, tf_cast), lambda i, j: (layer, i + tile_lo, cj(j))))
            out_specs.append(pl.BlockSpec((rb, tf_cast), lambda i, j: (i + tile_lo, cj(j))))
        else:
            assert w.shape[1:] == (dff, d) and rb % LANES == 0
            in_specs.append(pl.BlockSpec((None, tf_cast, rb), lambda i, j: (layer, cj(j), i + tile_lo)))
            out_specs.append(pl.BlockSpec((tf_cast, rb), lambda i, j: (cj(j), i + tile_lo)))
        out_shape.append(jax.ShapeDtypeStruct(w.shape[1:], BF16))
    tr_kj, tr_in = 0, ()
    if w_in_t is not None:
        n_main = w_in_t.shape[1] - gate_w
        sp = TR_SPLIT
        assert w_in_t.shape[2] == d and gate0 % LANES == 0 and n_main % (LANES * n_i) == 0 and d % (sp * LANES) == 0
        tr_kj = sp * n_main // (LANES * n_i)
        assert tr_kj <= n_ff
        job = lambda i, j: (i + tile_lo) * tr_kj + jnp.minimum(j, tr_kj - 1)
        in_specs += [pl.BlockSpec((None, LANES, d // sp), lambda i, j: (layer, job(i, j) // sp, job(i, j) % sp)),
                     pl.BlockSpec((None, LANES, d // sp), lambda i, j: (layer, job(i, j) // sp + 1, job(i, j) % sp))]
        out_specs.append(pl.BlockSpec((d // sp, LANES), lambda i, j: (job(i, j) % sp, job(i, j) // sp)))
        out_shape.append(jax.ShapeDtypeStruct((d, n_main), BF16))
        tr_in = (w_in_t, w_in_t)
    aliases = {}
    if carry:
        n_before = len(in_specs)
        keep = [0] + list(range(len(out_shape) - len(carry) + 1, len(out_shape)))
        assert len(keep) == len(carry) and not own_f32
        for k, (c, o) in enumerate(zip(carry, keep)):
            assert c.shape == out_shape[o].shape and c.dtype == out_shape[o].dtype
            in_specs.append(pl.BlockSpec(memory_space=pl.ANY))
            aliases[n_before + k] = o
    body = functools.partial(_ffn_body, n_ff=n_ff, final_norm=final_norm, own_f32=own_f32, n_tile=len(cast_tiles),
                             tr_kj=tr_kj, tile_lo=tile_lo, gate0=gate0, gate_w=gate_w, n_carry=len(carry),
                             fuse_prologue=not (cast_tiles or tr_kj))
    outs = pl.pallas_call(
        body,
        grid=(n_tiles, n_ff),
        in_specs=in_specs,
        out_specs=out_specs,
        out_shape=out_shape,
        input_output_aliases=aliases,
        scratch_shapes=[pltpu.VMEM((tm, d), BF16)],
        compiler_params=_params(("parallel", "arbitrary")),
        name=("ffn_final" if final_norm else "ffn") + ("_head" if own_f32 else ""),
    )(x, g, w1, w3, w2, gf, *cast_tiles, *tr_in, *carry)
    return outs[0], outs[1:]


def _inproj_body(*refs, n_cast):
    x_ref, g_ref, w_ref, wg_ref, bg_ref = refs[:5]
    cast_src = refs[5:5 + n_cast]
    qkvo_ref, u_ref, gates_ref = refs[5 + n_cast:8 + n_cast]
    cast_dst = refs[8 + n_cast:8 + 2 * n_cast]
    hn_ref = refs[-1]
    j = pl.program_id(1)
    nh = MLSTM_HEADS

    @pl.when(j == 0)
    def _():
        hn = _rms(x_ref[...], g_ref[...]).astype(BF16)
        hn_ref[...] = hn
        gt = lax.dot_general(wg_ref[...], hn, (((1,), (1,)), ((), ())), preferred_element_type=F32)
        gt = gt + bg_ref[...]
        for h in range(nh):
            gates_ref[h] = jnp.concatenate([gt[h:h + 1, :], gt[nh + h:nh + h + 1, :]], axis=0)

    @pl.when(j < 3)
    def _():
        qkvo_ref[...] = jnp.dot(hn_ref[...], w_ref[...], preferred_element_type=F32).astype(BF16)

    @pl.when(j == 3)
    def _():
        u_ref[...] = jnp.dot(hn_ref[...], w_ref[...], preferred_element_type=F32)

    for src, dst in zip(cast_src, cast_dst):
        dst[...] = src[...].astype(BF16)


def _inproj(x, g, w_main, wg_t, bg, *, tm=1024, layer=0, cast_rows=()):
    t, d = x.shape
    w = w_main.shape[1] // 4
    n_i, n_j = t // tm, 4
    in_specs = [
        pl.BlockSpec((tm, d), lambda i, j: (i, 0)),
        pl.BlockSpec((1, d), lambda i, j: (0, 0)),
        pl.BlockSpec((d, w), lambda i, j: (0, j)),
        pl.BlockSpec((2 * MLSTM_HEADS, d), lambda i, j: (0, 0)),
        pl.BlockSpec((2 * MLSTM_HEADS, 1), lambda i, j: (0, 0)),
    ]
    out_specs = [pl.BlockSpec((tm, w), lambda i, j: (i, jnp.minimum(j, 2))),
                 pl.BlockSpec((tm, w), lambda i, j: (i, 0)),
                 pl.BlockSpec((MLSTM_HEADS, 2, tm), lambda i, j: (0, 0, i))]
    out_shape = [
        jax.ShapeDtypeStruct((t, 3 * w), BF16),
        jax.ShapeDtypeStruct((t, w), F32),
        jax.ShapeDtypeStruct((MLSTM_HEADS, 2, t), F32),
    ]
    for a in cast_rows:
        n_rows, n_cols = a.shape[1:]
        rr = n_rows // (n_i * n_j)
        assert n_rows % (n_i * n_j) == 0 and rr % 16 == 0
        in_specs.append(pl.BlockSpec((None, rr, n_cols), lambda i, j: (layer, i * n_j + j, 0)))
        out_specs.append(pl.BlockSpec((rr, n_cols), lambda i, j: (i * n_j + j, 0)))
        out_shape.append(jax.ShapeDtypeStruct((n_rows, n_cols), BF16))
    outs = pl.pallas_call(
        functools.partial(_inproj_body, n_cast=len(cast_rows)),
        grid=(n_i, n_j),
        in_specs=in_specs,
        out_specs=out_specs,
        out_shape=out_shape,
        scratch_shapes=[pltpu.VMEM((tm, d), BF16)],
        compiler_params=_params(("parallel", "arbitrary")),
        name="inproj",
    )(x, g, w_main, wg_t, bg, *cast_rows)
    return outs[:3], outs[3:]


def _split3(x):
    hi = x.astype(BF16).astype(F32)
    r1 = x - hi
    mid = r1.astype(BF16).astype(F32)
    lo = (r1 - mid).astype(BF16).astype(F32)
    return [hi, mid, lo]


def _log_sigmoid(x):
    return jnp.minimum(x, 0.0) - jnp.log(1.0 + jnp.exp(-jnp.abs(x)))


def _mlstm_body(q_ref, k_ref, v_ref, og_ref, g_ref, cwq_ref, cwk_ref, cbq_ref, cbk_ref, nw_ref,
                sel_ref, out_ref, xq_ref, xk_ref, qs_ref, ks_ref, cols_ref, *, seq, dk, dv):
    L = MLSTM_CHUNK
    nc = seq // L

    def conv_silu(x_ref, w_ref, b_ref, xp_ref, dst_ref, scale):
        xp_ref[0:SUBLANES, :] = jnp.zeros((SUBLANES, dk), F32)
        xp_ref[SUBLANES:, :] = x_ref[...].astype(F32)
        w = w_ref[...]
        acc = b_ref[...] + xp_ref[pl.ds(SUBLANES, seq), :] * w[3:4, :]
        for d in (1, 2, 3):
            acc = acc + xp_ref[pl.ds(SUBLANES - d, seq), :] * w[3 - d:4 - d, :]
        y = acc * jax.nn.sigmoid(acc)
        if scale != 1.0:
            y = y * scale
        dst_ref[...] = y.astype(BF16)

    conv_silu(q_ref, cwq_ref, cbq_ref, xq_ref, qs_ref, dk ** -0.5)
    conv_silu(k_ref, cwk_ref, cbk_ref, xk_ref, ks_ref, 1.0)

    rows = lax.broadcasted_iota(jnp.int32, (L, L), 0)
    cols = lax.broadcasted_iota(jnp.int32, (L, L), 1)
    causal = cols <= rows
    tri = jnp.where(rows <= cols, 1.0, 0.0).astype(BF16)

    ncp = -(-nc // SUBLANES) * SUBLANES
    pad = [jnp.zeros((ncp - nc, L), F32)] if ncp > nc else []
    g = g_ref[...]
    li_all = jnp.concatenate([g[0:1, c * L:(c + 1) * L] for c in range(nc)] + pad, axis=0)
    lf_all = jnp.concatenate([_log_sigmoid(g[1:2, c * L:(c + 1) * L]) for c in range(nc)] + pad, axis=0)
    parts = jnp.concatenate(_split3(lf_all), axis=0).astype(BF16)
    cs = jnp.dot(parts, tri, preferred_element_type=F32)
    b_all = cs[0:ncp] + cs[ncp:2 * ncp] + cs[2 * ncp:3 * ncp]
    d_all = li_all - b_all
    at = jnp.concatenate(_split3(b_all) + _split3(d_all), axis=0).astype(BF16)
    cols_ref[...] = lax.dot_general(at, sel_ref[...], (((0,), (0,)), ((), ())), preferred_element_type=F32)

    ext = 2 * LANES
    rep = lambda a: jnp.concatenate([a] * (dv // LANES), axis=1)
    ones_l = jnp.ones((L, ext), BF16)
    mean_w = jnp.full((dv, LANES), 1.0 / dv, BF16)
    ct_ext = jnp.zeros((dk, dv + ext), F32)
    m_prev = jnp.zeros((1, 1), F32)
    nw = nw_ref[...]

    for c in range(nc):
        r0 = c * L
        q = qs_ref[pl.ds(r0, L), :]
        k = ks_ref[pl.ds(r0, L), :]
        v_ext = jnp.concatenate([v_ref[pl.ds(r0, L), :], ones_l], axis=1)
        d_row = d_all[c:c + 1, :]
        b_end = b_all[c:c + 1, L - 1:L]
        b_c = cols_ref[:, c * LANES:(c + 1) * LANES]
        d_c = cols_ref[:, (nc + c) * LANES:(nc + c + 1) * LANES]

        cm_col = jnp.max(jnp.where(causal, d_row, -jnp.inf), axis=-1, keepdims=True)
        m_t = b_c + jnp.maximum(m_prev, cm_col)
        a_t = jnp.exp(b_c + m_prev - m_t)
        e_mt = jnp.exp(-m_t)
        arg = jnp.concatenate([b_c - m_t] * (L // LANES), axis=1) + d_row
        d_mat = jnp.where(causal, jnp.exp(arg), 0.0)

        qk = lax.dot_general(q, k, (((1,), (1,)), ((), ())), preferred_element_type=F32)
        s_b = (qk * d_mat).astype(BF16)
        sv = jnp.dot(s_b, v_ext, preferred_element_type=F32)
        qc = jnp.dot(q, ct_ext.astype(BF16), preferred_element_type=F32)
        num = sv[:, :dv] + rep(a_t) * qc[:, :dv]
        den = sv[:, dv:dv + LANES] + a_t * qc[:, dv:dv + LANES]
        inv = 1.0 / jnp.maximum(jnp.abs(den), e_mt)
        h = num * rep(inv)
        msq = jnp.dot((h * h).astype(BF16), mean_w, preferred_element_type=F32)
        hn = h * rep(lax.rsqrt(msq + EPS)) * nw
        og = og_ref[pl.ds(r0, L), :].astype(F32)
        out_ref[pl.ds(r0, L), :] = (hn * jax.nn.sigmoid(og)).astype(BF16)

        w_end = b_end + d_row
        m_loc = jnp.max(w_end, axis=-1, keepdims=True)
        e_c = jnp.exp(b_end + d_c - m_loc)
        ke = (k.astype(F32) * e_c).astype(BF16)
        c_loc = lax.dot_general(ke, v_ext, (((0,), (0,)), ((), ())), preferred_element_type=F32)
        m_new = jnp.maximum(b_end + m_prev, m_loc)
        ct_ext = jnp.exp(b_end + m_prev - m_new) * ct_ext + jnp.exp(m_loc - m_new) * c_loc
        m_prev = m_new


def _mlstm_selector(nc):
    ncp = -(-nc // SUBLANES) * SUBLANES
    sel = np.zeros((6 * ncp, 2 * nc * LANES), np.float32)
    for kind in range(2):
        for part in range(3):
            for c in range(nc):
                sel[(kind * 3 + part) * ncp + c, (kind * nc + c) * LANES:(kind * nc + c + 1) * LANES] = 1.0
    return jnp.asarray(sel, BF16)


def _mlstm(qkvo, gates, conv_w, conv_b, norm_w, *, batch, seq):
    t = qkvo.shape[0]
    w = qkvo.shape[1] // 3
    nh = MLSTM_HEADS
    dk = w // (2 * nh)
    dv = w // nh
    assert dk == LANES and dv % LANES == 0 and seq % MLSTM_CHUNK == 0 and MLSTM_CHUNK % LANES == 0
    nc = seq // MLSTM_CHUNK
    sel = _mlstm_selector(nc)
    body = functools.partial(_mlstm_body, seq=seq, dk=dk, dv=dv)
    return pl.pallas_call(
        body,
        grid=(batch, nh),
        in_specs=[
            pl.BlockSpec((seq, dk), lambda b, h: (b, h)),
            pl.BlockSpec((seq, dk), lambda b, h: (b, nh + h)),
            pl.BlockSpec((seq, dv), lambda b, h: (b, nh + h)),
            pl.BlockSpec((seq, dv), lambda b, h: (b, 2 * nh + h)),
            pl.BlockSpec((None, 2, seq), lambda b, h: (h, 0, b)),
            pl.BlockSpec((conv_w.shape[0], dk), lambda b, h: (0, h)),
            pl.BlockSpec((conv_w.shape[0], dk), lambda b, h: (0, nh + h)),
            pl.BlockSpec((1, dk), lambda b, h: (0, h)),
            pl.BlockSpec((1, dk), lambda b, h: (0, nh + h)),
            pl.BlockSpec((1, dv), lambda b, h: (0, h)),
            pl.BlockSpec(sel.shape, lambda b, h: (0, 0)),
        ],
        out_specs=pl.BlockSpec((seq, dv), lambda b, h: (b, h)),
        out_shape=jax.ShapeDtypeStruct((t, nh * dv), BF16),
        scratch_shapes=[
            pltpu.VMEM((seq + SUBLANES, dk), F32),
            pltpu.VMEM((seq + SUBLANES, dk), F32),
            pltpu.VMEM((seq, dk), BF16),
            pltpu.VMEM((seq, dk), BF16),
            pltpu.VMEM((MLSTM_CHUNK, 2 * nc * LANES), F32),
        ],
        compiler_params=_params(("parallel", "parallel")),
        name="mlstm",
    )(qkvo, qkvo, qkvo, qkvo, gates, conv_w, conv_w, conv_b, conv_b, norm_w, sel)


def _cmul(ar, ai, br, bi):
    return ar * br - ai * bi, ar * bi + ai * br


def _s5pack_body(lr_ref, li_ref, ldt_ref, br_ref, bi_ref, cr_ref, ci_ref, tile_ref,
                 bm_ref, cm_ref, dm_ref, lpr_ref, lpi_ref):
    R = S5_LIFT
    cb = S5_GROUPS_PER_BLOCK * S5_GROUP
    half = S5_GROUPS_PER_BLOCK * S5_STATE
    lr = lr_ref[...]
    li = li_ref[...]
    dt = jnp.exp(ldt_ref[...])
    mag = jnp.exp(lr * dt)
    ang = li * dt
    lbr = mag * jnp.cos(ang)
    lbi = mag * jnp.sin(ang)
    nr = lbr - 1.0
    den = lr * lr + li * li
    cr = (nr * lr + lbi * li) / den
    ci = (lbi * lr - nr * li) / den
    pr, pi_ = _cmul(cr, ci, br_ref[...], bi_ref[...])
    qr, qi = cr_ref[...], ci_ref[...]

    row = lax.broadcasted_iota(jnp.int32, (cb, half), 0)
    lane = lax.broadcasted_iota(jnp.int32, (cb, half), 1)
    same_group = (row // S5_GROUP) == (lane // S5_STATE)
    tile = tile_ref[...]

    def blockdiag(a):
        return jnp.where(same_group, jnp.dot(a.astype(BF16), tile, preferred_element_type=F32), 0.0)

    def pair(re, im):
        return jnp.concatenate([blockdiag(re), blockdiag(im)], axis=1).astype(BF16)

    c0 = pair(qr, -qi)
    lag_rows = []
    for m in range(R):
        rows_m = pair(pr, pi_)
        bm_ref[(R - 1 - m) * cb:(R - m) * cb, :] = rows_m
        lag_rows.append(rows_m)
        pr, pi_ = _cmul(pr, pi_, lbr, lbi)
        qr, qi = _cmul(qr, qi, lbr, lbi)
        cm_ref[m * cb:(m + 1) * cb, :] = pair(qr, -qi)
    dlag = [lax.dot_general(lag_rows[m], c0, (((1,), (1,)), ((), ())), preferred_element_type=F32).astype(BF16)
            for m in range(R)]
    zero = jnp.zeros((cb, cb), BF16)
    for rp in range(R):
        for r in range(R):
            dm_ref[rp * cb:(rp + 1) * cb, r * cb:(r + 1) * cb] = dlag[r - rp] if r >= rp else zero

    wr, wi = lbr, lbi
    for _ in range(R - 1):
        wr, wi = _cmul(wr, wi, lbr, lbi)
    first = same_group & ((row % S5_GROUP) == 0)

    def lane_vector(a):
        t = sum(jnp.dot(part.astype(BF16), tile, preferred_element_type=F32) for part in _split3(a))
        return jnp.sum(jnp.where(first, t, 0.0), axis=0, keepdims=True)

    lpr_ref[...] = lane_vector(wr)
    lpi_ref[...] = lane_vector(wi)


def _s5_weights(lam_re, lam_im, log_dt, b_re, b_im, c_re, c_im):
    g, p = lam_re.shape
    hch = b_re.shape[-1]
    gpb = S5_GROUPS_PER_BLOCK
    ngb = g // gpb
    R = S5_LIFT
    cb, half = gpb * hch, gpb * p
    rep = lambda a: jnp.repeat(a, hch, axis=0)
    to_rows = lambda a: jnp.transpose(a, (0, 2, 1)).reshape(g * hch, p)
    ldt = jnp.broadcast_to(log_dt[:, None], (g, p))
    tile = jnp.asarray(np.tile(np.eye(p, dtype=np.float32), (1, gpb)), BF16)
    blk = pl.BlockSpec((cb, p), lambda i: (i, 0))
    return pl.pallas_call(
        _s5pack_body,
        grid=(ngb,),
        in_specs=[blk] * 7 + [pl.BlockSpec((p, half), lambda i: (0, 0))],
        out_specs=[pl.BlockSpec((None, R * cb, 2 * half), lambda i: (i, 0, 0)),
                   pl.BlockSpec((None, R * cb, 2 * half), lambda i: (i, 0, 0)),
                   pl.BlockSpec((None, R * cb, R * cb), lambda i: (i, 0, 0)),
                   pl.BlockSpec((None, 1, half), lambda i: (i, 0, 0)),
                   pl.BlockSpec((None, 1, half), lambda i: (i, 0, 0))],
        out_shape=[jax.ShapeDtypeStruct((ngb, R * cb, 2 * half), BF16),
                   jax.ShapeDtypeStruct((ngb, R * cb, 2 * half), BF16),
                   jax.ShapeDtypeStruct((ngb, R * cb, R * cb), BF16),
                   jax.ShapeDtypeStruct((ngb, 1, half), F32),
                   jax.ShapeDtypeStruct((ngb, 1, half), F32)],
        name="s5pack",
    )(rep(lam_re), rep(lam_im), rep(ldt), to_rows(b_re), to_rows(b_im),
      c_re.reshape(g * hch, p), c_im.reshape(g * hch, p), tile)


def _s5_body(u_ref, bm_ref, cm_ref, dm_ref, lr_ref, li_ref, dsk_ref, wglu_ref, bglu_ref, o_ref,
             usb_ref, sb_ref, y_ref, st_ref, *, ts, ngb):
    R = S5_LIFT
    nk = ts // R
    rows = nk * SUBLANES
    half = S5_GROUPS_PER_BLOCK * S5_STATE
    cb = S5_GROUPS_PER_BLOCK * S5_GROUP

    @pl.when(pl.program_id(0) == 0)
    def _():
        st_ref[...] = jnp.zeros_like(st_ref)

    def srow(s):
        return (s % R) * rows + (s // R) * SUBLANES

    for s0 in range(0, ts, SUBLANES):
        blk = jnp.swapaxes(u_ref[:, s0:s0 + SUBLANES, :], 0, 1)
        for s in range(s0, s0 + SUBLANES):
            usb_ref[srow(s):srow(s) + SUBLANES, :] = blk[s - s0]

    def lhs(gb):
        return jnp.concatenate([usb_ref[r * rows:(r + 1) * rows, gb * cb:(gb + 1) * cb] for r in range(R)],
                               axis=1).astype(BF16)

    def project_in(gb):
        sb_ref[gb % 2, SUBLANES:, :] = jnp.dot(lhs(gb), bm_ref[gb], preferred_element_type=F32)

    project_in(0)
    for gb in range(ngb):
        if gb + 1 < ngb:
            project_in(gb + 1)
        buf = sb_ref.at[gb % 2]
        lrb = jnp.broadcast_to(lr_ref[gb], (SUBLANES, half))
        lib = jnp.broadcast_to(li_ref[gb], (SUBLANES, half))
        xr = st_ref[gb, :, 0:half]
        xi = st_ref[gb, :, half:2 * half]
        buf[0:SUBLANES, 0:half] = xr
        buf[0:SUBLANES, half:2 * half] = xi
        for k in range(nk):
            rs = slice((k + 1) * SUBLANES, (k + 2) * SUBLANES)
            nr = lrb * xr - lib * xi + buf[rs, 0:half]
            ni = lrb * xi + lib * xr + buf[rs, half:2 * half]
            buf[rs, 0:half] = nr
            buf[rs, half:2 * half] = ni
            xr, xi = nr, ni
        st_ref[gb, :, 0:half] = xr
        st_ref[gb, :, half:2 * half] = xi

        yo = lax.dot_general(buf[0:rows, :].astype(BF16), cm_ref[gb], (((1,), (1,)), ((), ())),
                             preferred_element_type=F32)
        yd = jnp.dot(lhs(gb), dm_ref[gb], preferred_element_type=F32)
        for r in range(R):
            ug = usb_ref[r * rows:(r + 1) * rows, gb * cb:(gb + 1) * cb]
            yg = yo[:, r * cb:(r + 1) * cb] + yd[:, r * cb:(r + 1) * cb] + dsk_ref[:, gb * cb:(gb + 1) * cb] * ug
            y_ref[r * rows:(r + 1) * rows, gb * cb:(gb + 1) * cb] = jax.nn.gelu(yg)

    y = y_ref[...]
    z = jnp.dot(y.astype(BF16), wglu_ref[...], preferred_element_type=F32) + bglu_ref[...]
    usb_ref[...] = y * jax.nn.sigmoid(z)

    for s0 in range(0, ts, SUBLANES):
        blk = jnp.stack([usb_ref[srow(s):srow(s) + SUBLANES, :] for s in range(s0, s0 + SUBLANES)], axis=0)
        o_ref[:, s0:s0 + SUBLANES, :] = jnp.swapaxes(blk, 0, 1)


def _s5(u, bm, cm, dm, lpr, lpi, dsk, wglu, bglu, *, ts=128):
    batch, seq, width = u.shape
    ngb = bm.shape[0]
    half = S5_GROUPS_PER_BLOCK * S5_STATE
    assert batch == SUBLANES and ts % S5_LIFT == 0 and ts % SUBLANES == 0 and seq % ts == 0
    body = functools.partial(_s5_body, ts=ts, ngb=ngb)
    full = lambda a: pl.BlockSpec(a.shape, lambda i, n=a.ndim: (0,) * n, pipeline_mode=pl.Buffered(1))
    return pl.pallas_call(
        body,
        grid=(seq // ts,),
        in_specs=[pl.BlockSpec((batch, ts, width), lambda i: (0, i, 0)),
                  full(bm), full(cm), full(dm), full(lpr), full(lpi), full(dsk), full(wglu), full(bglu)],
        out_specs=pl.BlockSpec((batch, ts, width), lambda i: (0, i, 0)),
        out_shape=jax.ShapeDtypeStruct((batch, seq, width), F32),
        scratch_shapes=[
            pltpu.VMEM((ts * batch, width), F32),
            pltpu.VMEM((2, (ts // S5_LIFT + 1) * batch, 2 * half), F32),
            pltpu.VMEM((ts * batch, width), F32),
            pltpu.VMEM((ngb, batch, 2 * half), F32),
        ],
        compiler_params=_params(("arbitrary",)),
        name="s5",
    )(u, bm, cm, dm, lpr, lpi, dsk, wglu, bglu)


def _outproj_body(x_ref, hm_ref, hs_ref, wa_ref, wb_ref, o_ref):
    acc = jnp.dot(hm_ref[...], wa_ref[...], preferred_element_type=F32)
    acc = acc + jnp.dot(hs_ref[...].astype(BF16), wb_ref[...], preferred_element_type=F32)
    o_ref[...] = x_ref[...] + acc


def _outproj(x, hm, hs, wo, *, tm=512):
    t, d = x.shape
    w = hm.shape[1]
    return pl.pallas_call(
        _outproj_body,
        grid=(t // tm,),
        in_specs=[
            pl.BlockSpec((tm, d), lambda i: (i, 0)),
            pl.BlockSpec((tm, w), lambda i: (i, 0)),
            pl.BlockSpec((tm, w), lambda i: (i, 0)),
            pl.BlockSpec((w, d), lambda i: (0, 0)),
            pl.BlockSpec((hs.shape[1], d), lambda i: (w // hs.shape[1], 0)),
        ],
        out_specs=pl.BlockSpec((tm, d), lambda i: (i, 0)),
        out_shape=jax.ShapeDtypeStruct((t, d), F32),
        compiler_params=_params(("parallel",)),
        name="outproj",
    )(x, hm, hs, wo, wo)


def kernel(x, ffn1_norm, ffn1_w1, ffn1_w3, ffn1_w2, mix_norm, w_in, conv_w, conv_b, b_i, b_f, mlstm_norm, lam_re, lam_im, log_dt, b_re, b_im, c_re, c_im, d_skip, w_glu, b_glu, w_out, ffn2_norm, ffn2_w1, ffn2_w3, ffn2_w2, final_norm):
    batch, seq, d = x.shape
    depth = ffn1_norm.shape[0]
    nh = MLSTM_HEADS
    qk_cols = conv_w.shape[-1]
    mw = mlstm_norm.shape[-1]
    gate0 = qk_cols + 2 * mw
    sw = w_glu.shape[-1]
    row = lambda a: a.reshape(1, -1).astype(F32)
    xt = x.reshape(batch * seq, d)
    gfin = row(final_norm)
    w_in_t = jnp.swapaxes(w_in, 1, 2)
    for l in range(depth):
        jobs = dict(layer=l, cast_tiles=(ffn2_w1, ffn2_w3, ffn2_w2), w_in_t=w_in_t, gate0=gate0, gate_w=2 * nh)
        x_head, head = _ffn(xt, row(ffn1_norm[l]), ffn1_w1, ffn1_w3, ffn1_w2, gfin, final_norm=False, tf=256,
                            n_tiles=1, own_f32=True, **jobs)
        xt, (w1b, w3b, w2b, w_main) = _ffn(xt, row(ffn1_norm[l]), head[0], head[1], head[2], gfin, final_norm=False,
                                           tile_lo=1, carry=(x_head,) + tuple(head[3:]), **jobs)

        wg_t = w_in_t[l, gate0:gate0 + 2 * nh, :].astype(BF16)
        bg = jnp.concatenate([b_i[l], b_f[l]]).reshape(2 * nh, 1).astype(F32)
        (qkvo, u, gates), (wo, wglu) = _inproj(xt, row(mix_norm[l]), w_main, wg_t, bg, layer=l,
                                               cast_rows=(w_out, w_glu))

        hm = _mlstm(qkvo, gates, conv_w[l].astype(F32), row(conv_b[l]), row(mlstm_norm[l]), batch=batch, seq=seq)

        bm, cm, dm, lpr, lpi = _s5_weights(lam_re[l], lam_im[l], log_dt[l], b_re[l], b_im[l], c_re[l], c_im[l])
        hs = _s5(u.reshape(batch, seq, sw), bm, cm, dm, lpr, lpi, row(d_skip[l]), wglu, row(b_glu[l]))

        xt = _outproj(xt, hm, hs.reshape(batch * seq, sw), wo)

        xt, _ = _ffn(xt, row(ffn2_norm[l]), w1b, w3b, w2b, gfin, final_norm=(l == depth - 1))
    return xt.reshape(batch, seq, d)
```
